```python
import jax, jax.numpy as jnp
from jax import lax
import numpy as np

D_MODEL = 2048
BATCH = 4
SEQ = 4096
DEPTH = 2

ATT_HEADS = 8
ATT_KV_HEADS = 2
ATT_HEAD_DIM = 128
IDX_HEADS = 8
IDX_HEAD_DIM = 64
INDEX_TOPK = 256
Q_BLOCK = 128
ROPE_THETA = 500000.0
ROPE_FRACTION = 4
MASK_VALUE = -1e30
CONV_WIDTH = 1024
CONV_K = 3
HG_HEADS = 8
HG_KEY_DIM = 128
HG_VAL_DIM = 128
HG_CHUNK = 64
F_MIN = 1e-30
N_BRANCHES = 3
D_FF = 4 * D_MODEL
NORM_EPS = 1e-6
ADA_CHUNKS = 6

ATT_Q_W = ATT_HEADS * ATT_HEAD_DIM
ATT_KV_W = ATT_KV_HEADS * ATT_HEAD_DIM
IDX_Q_W = IDX_HEADS * IDX_HEAD_DIM
HG_K_W = HG_HEADS * HG_KEY_DIM
HG_V_W = HG_HEADS * HG_VAL_DIM
IN_SPLITS = (ATT_Q_W, ATT_KV_W, ATT_KV_W, IDX_Q_W, IDX_HEAD_DIM, IDX_HEADS,
             CONV_WIDTH, CONV_WIDTH, CONV_WIDTH, HG_K_W, HG_K_W, HG_V_W, HG_V_W,
             N_BRANCHES * D_MODEL)
D_IN = sum(IN_SPLITS)

kernel_name = "hybrid_dsa_shortconv_hgrn2_gated_block"


def split_cols(a, sizes):
    offs = np.cumsum(sizes)[:-1].tolist()
    return jnp.split(a, offs, axis=-1)


def rms_norm(x, g):
    x32 = x.astype(jnp.float32)
    y = x32 * lax.rsqrt(jnp.mean(x32 * x32, axis=-1, keepdims=True) + NORM_EPS)
    return (y * g.astype(jnp.float32)).astype(x.dtype)


def partial_rope(x, positions):
    rot = x.shape[-1] // ROPE_FRACTION
    half = rot // 2
    inv_freq = ROPE_THETA ** (-jnp.arange(half, dtype=jnp.float32) / half)
    ang = positions.astype(jnp.float32)[:, :, None, None] * inv_freq
    cos, sin = jnp.cos(ang), jnp.sin(ang)
    x1 = x[..., :half].astype(jnp.float32)
    x2 = x[..., half:rot].astype(jnp.float32)
    return jnp.concatenate([(x1 * cos - x2 * sin).astype(x.dtype),
                            (x2 * cos + x1 * sin).astype(x.dtype),
                            x[..., rot:]], axis=-1)


def dsa_attention(q, k, v, qi, ki, wi, topk):
    B, S, H, dh = q.shape
    G = k.shape[2]
    R = H // G
    nblk = S // Q_BLOCK
    scale = dh ** -0.5
    w_scale = (IDX_HEADS ** -0.5) * (IDX_HEAD_DIM ** -0.5)
    key_pos = jnp.arange(S)
    gather = jax.vmap(lambda kk, ii: kk[ii])

    def to_blocks(a):
        return a.reshape(B, nblk, Q_BLOCK, *a.shape[2:]).swapaxes(0, 1)

    def block(args):
        qb, qib, wib, blk = args
        qpos = blk * Q_BLOCK + jnp.arange(Q_BLOCK)
        dots = jnp.einsum('bqhd,bsd->bqhs', qib, ki).astype(jnp.float32)
        score = jnp.einsum('bqhs,bqh->bqs', jax.nn.relu(dots),
                           wib.astype(jnp.float32) * w_scale)
        causal = key_pos[None, :] <= qpos[:, None]
        score = jnp.where(causal[None], score, MASK_VALUE)
        _, idx = lax.top_k(score, topk)
        valid = idx <= qpos[None, :, None]
        kg = gather(k, idx)
        vg = gather(v, idx)
        qg = qb.reshape(B, Q_BLOCK, G, R, dh)
        s = jnp.einsum('bqgrd,bqkgd->bqgrk', qg, kg).astype(jnp.float32) * scale
        s = jnp.where(valid[:, :, None, None, :], s, MASK_VALUE)
        p = jax.nn.softmax(s, axis=-1).astype(vg.dtype)
        o = jnp.einsum('bqgrk,bqkgd->bqgrd', p, vg)
        return o.reshape(B, Q_BLOCK, H, dh)

    out = lax.map(block, (to_blocks(q), to_blocks(qi), to_blocks(wi), jnp.arange(nblk)))
    return out.swapaxes(0, 1).reshape(B, S, H, dh)


def short_conv(cb, cc, cx, w):
    S = cx.shape[1]
    u = cc * cx
    up = jnp.pad(u, ((0, 0), (CONV_K - 1, 0), (0, 0)))
    conv = sum(up[:, j:j + S] * w[j] for j in range(CONV_K))
    return cb * conv


def gated_linear_recurrence(q, k, v, logf):
    B, S, H, dk = q.shape
    dv = v.shape[-1]
    C = HG_CHUNK
    n = S // C

    def chunks(a):
        return a.reshape(B, n, C, H, a.shape[-1]).transpose(1, 0, 3, 2, 4)

    tri = jnp.tril(jnp.ones((C, C), bool))[None, None, :, :, None]

    def step(state, inp):
        qc, kc, vc, gc = inp
        b = jnp.cumsum(gc, axis=2)
        diff = b[:, :, :, None, :] - b[:, :, None, :, :]
        decay = jnp.where(tri, jnp.exp(jnp.where(tri, diff, 0.0)), 0.0)
        attn = jnp.einsum('bhtsd,bhsd->bhts', qc[:, :, :, None, :] * decay, kc)
        out = (jnp.einsum('bhts,bhsv->bhtv', attn, vc)
               + jnp.einsum('bhtd,bhdv->bhtv', qc * jnp.exp(b), state))
        b_last = b[:, :, -1:, :]
        state = (jnp.exp(b_last[:, :, 0, :, None]) * state
                 + jnp.einsum('bhsd,bhsv->bhdv', kc * jnp.exp(b_last - b), vc))
        return state, out

    s0 = jnp.zeros((B, H, dk, dv), jnp.float32)
    _, out = lax.scan(step, s0, (chunks(q), chunks(k), chunks(v), chunks(logf)))
    return out.transpose(1, 0, 3, 2, 4).reshape(B, S, H, dv)


def hgrn2_mixer(hq, hf, hi, hg, lb, norm_g):
    B, S, _ = hq.shape
    lb = lb.reshape(HG_HEADS, HG_KEY_DIM).astype(jnp.float32)
    z = hf.reshape(B, S, HG_HEADS, HG_KEY_DIM).astype(jnp.float32)
    f = lb + (1.0 - lb) * jax.nn.sigmoid(z)
    logf = jnp.log(jnp.maximum(f, F_MIN))
    kin = 1.0 - f
    q = hq.reshape(B, S, HG_HEADS, HG_KEY_DIM).astype(jnp.float32)
    v = hi.reshape(B, S, HG_HEADS, HG_VAL_DIM).astype(jnp.float32)
    o = gated_linear_recurrence(q, kin, v, logf)
    gate = jax.nn.silu(hg.reshape(B, S, HG_HEADS, HG_VAL_DIM).astype(jnp.float32))
    o = rms_norm(o, norm_g) * gate
    return o.reshape(B, S, HG_V_W).astype(hq.dtype)


def hybrid_mixer(h, positions, w_in, conv_w, lb, hg_norm_g, w_o_attn, w_o_conv,
                 w_o_hgrn, w_out, topk):
    B, S, _ = h.shape
    proj = h @ w_in
    (q, k, v, qi, ki, wi, cb, cc, cx, hq, hf, hi, hg, gates) = split_cols(proj, IN_SPLITS)
    q = partial_rope(q.reshape(B, S, ATT_HEADS, ATT_HEAD_DIM), positions)
    k = partial_rope(k.reshape(B, S, ATT_KV_HEADS, ATT_HEAD_DIM), positions)
    v = v.reshape(B, S, ATT_KV_HEADS, ATT_HEAD_DIM)
    qi = partial_rope(qi.reshape(B, S, IDX_HEADS, IDX_HEAD_DIM), positions)
    ki = partial_rope(ki[:, :, None, :], positions)[:, :, 0]
    y_att = dsa_attention(q, k, v, qi, ki, wi, topk).reshape(B, S, ATT_Q_W) @ w_o_attn
    y_conv = short_conv(cb, cc, cx, conv_w) @ w_o_conv
    y_hg = hgrn2_mixer(hq, hf, hi, hg, lb, hg_norm_g) @ w_o_hgrn
    g_att, g_conv, g_hg = jnp.split(jax.nn.sigmoid(gates), N_BRANCHES, axis=-1)
    merged = g_att * y_att + g_conv * y_conv + g_hg * y_hg
    return merged @ w_out


def sq_relu_mlp(h, w1, w2):
    return jnp.square(jax.nn.relu(h @ w1)) @ w2


def setup_inputs(seed: int = 0) -> dict:
    key = jax.random.key(seed)
    ks = jax.random.split(key, 20)
    f32 = jnp.float32

    def nrm(k, shape, scale):
        return jax.random.normal(k, shape, f32) * scale

    return {
        "x": nrm(ks[0], (BATCH, SEQ, D_MODEL), 1.0),
        "c": nrm(ks[1], (BATCH, D_MODEL), 1.0),
        "positions": jnp.tile(jnp.arange(SEQ, dtype=jnp.int32)[None, :], (BATCH, 1)),
        "ada_w": nrm(ks[2], (DEPTH, D_MODEL, ADA_CHUNKS * D_MODEL), 0.5 * D_MODEL ** -0.5),
        "ada_b": nrm(ks[3], (DEPTH, ADA_CHUNKS * D_MODEL), 0.02),
        "norm_mix_g": 1.0 + nrm(ks[4], (DEPTH, D_MODEL), 0.02),
        "w_in": nrm(ks[5], (DEPTH, D_MODEL, D_IN), D_MODEL ** -0.5),
        "conv_w": nrm(ks[6], (DEPTH, CONV_K, CONV_WIDTH), CONV_K ** -0.5),
        "hg_lower_bounds": nrm(ks[7], (DEPTH, HG_K_W), 0.1),
        "hg_norm_g": 1.0 + nrm(ks[8], (DEPTH, HG_VAL_DIM), 0.02),
        "w_o_attn": nrm(ks[9], (DEPTH, ATT_Q_W, D_MODEL), ATT_Q_W ** -0.5),
        "w_o_conv": nrm(ks[10], (DEPTH, CONV_WIDTH, D_MODEL), CONV_WIDTH ** -0.5),
        "w_o_hgrn": nrm(ks[11], (DEPTH, HG_V_W, D_MODEL), HG_V_W ** -0.5),
        "w_out": nrm(ks[12], (DEPTH, D_MODEL, D_MODEL), D_MODEL ** -0.5),
        "norm_mlp_g": 1.0 + nrm(ks[13], (DEPTH, D_MODEL), 0.02),
        "w_mlp1": nrm(ks[14], (DEPTH, D_MODEL, D_FF), D_MODEL ** -0.5),
        "w_mlp2": nrm(ks[15], (DEPTH, D_FF, D_MODEL), D_FF ** -0.5),
        "final_norm_g": 1.0 + nrm(ks[16], (D_MODEL,), 0.02),
    }


def reference(x, c, positions, ada_w, ada_b, norm_mix_g, w_in, conv_w, hg_lower_bounds,
              hg_norm_g, w_o_attn, w_o_conv, w_o_hgrn, w_out, norm_mlp_g, w_mlp1, w_mlp2,
              final_norm_g):
    S = x.shape[1]
    topk = min(INDEX_TOPK, S // 4)
    lb_all = jax.nn.softmax(hg_lower_bounds.astype(jnp.float32), axis=0)
    lb_all = jnp.cumsum(lb_all, axis=0) - lb_all[0]
    c_act = jax.nn.silu(c)
    for l in range(DEPTH):
        mod = c_act @ ada_w[l] + ada_b[l]
        sh1, sc1, g1, sh2, sc2, g2 = [m[:, None, :] for m in jnp.split(mod, ADA_CHUNKS, axis=-1)]
        h = rms_norm(x, norm_mix_g[l]) * (1.0 + sc1) + sh1
        y = hybrid_mixer(h, positions, w_in[l], conv_w[l], lb_all[l], hg_norm_g[l],
                         w_o_attn[l], w_o_conv[l], w_o_hgrn[l], w_out[l], topk)
        x = x + g1 * y
        h = rms_norm(x, norm_mlp_g[l]) * (1.0 + sc2) + sh2
        x = x + g2 * sq_relu_mlp(h, w_mlp1[l], w_mlp2[l])
    return rms_norm(x, final_norm_g)
```

```python
import functools

import numpy as np
import jax
import jax.numpy as jnp
from jax import lax
from jax.experimental import pallas as pl
from jax.experimental.pallas import tpu as pltpu

ATT_HEADS = 8
ATT_KV_HEADS = 2
ATT_HEAD_DIM = 128
IDX_HEADS = 8
IDX_HEAD_DIM = 64
INDEX_TOPK = 256
ROPE_THETA = 500000.0
ROPE_FRACTION = 4
MASK_VALUE = -1e30
CONV_WIDTH = 1024
CONV_K = 3
HG_HEADS = 8
HG_KEY_DIM = 128
HG_VAL_DIM = 128
F_MIN = 1e-30
N_BRANCHES = 3
NORM_EPS = 1e-6
ADA_CHUNKS = 6

ATT_Q_W = ATT_HEADS * ATT_HEAD_DIM
ATT_KV_W = ATT_KV_HEADS * ATT_HEAD_DIM
IDX_Q_W = IDX_HEADS * IDX_HEAD_DIM
HG_W = HG_HEADS * HG_KEY_DIM

LANES = 128
SUBLANES = 8
VMEM_LIMIT_BYTES = 56 * 1024 * 1024

INT_MIN = -2 ** 31
HG_CHUNK = 64
HG_LEVELS = 6

BF16 = jnp.bfloat16
F32 = jnp.float32


def _cparams(sem):
    return pltpu.CompilerParams(dimension_semantics=sem, vmem_limit_bytes=VMEM_LIMIT_BYTES)


def _dot(a, b):
    return jnp.dot(a, b, preferred_element_type=F32)


def _dot_nt(a, b):
    return lax.dot_general(a, b, (((1,), (1,)), ((), ())), preferred_element_type=F32)


def _sigmoid(x):
    return 1.0 / (1.0 + jnp.exp(-x))


def _mod_kernel(c_ref, w_ref, b_ref, o_ref):
    c = c_ref[...]
    ca = (c * _sigmoid(c)).astype(BF16)
    o_ref[...] = _dot(ca, w_ref[...].astype(BF16)) + b_ref[...]


def _modulation(c, ada_w, ada_b):
    depth, d, n = ada_w.shape
    b = c.shape[0]
    bp = -(-b // SUBLANES) * SUBLANES
    cp = jnp.pad(c, ((0, bp - b), (0, 0)))
    tn = 1024
    out = pl.pallas_call(
        _mod_kernel,
        grid=(depth, n // tn),
        in_specs=[
            pl.BlockSpec((bp, d), lambda l, j: (0, 0)),
            pl.BlockSpec((None, d, tn), lambda l, j: (l, 0, j)),
            pl.BlockSpec((None, 1, tn), lambda l, j: (l, 0, j)),
        ],
        out_specs=pl.BlockSpec((None, bp, tn), lambda l, j: (l, 0, j)),
        out_shape=jax.ShapeDtypeStruct((depth, bp, n), F32),
        compiler_params=_cparams(("arbitrary", "arbitrary")),
        name="adaln_mod",
    )(cp, ada_w, ada_b.reshape(depth, 1, n))
    return out[:, :b]


def _norm_kernel(x_ref, g_ref, *rest, modulate):
    if modulate:
        sh_ref, sc_ref, o_ref = rest
    else:
        (o_ref,) = rest
    x = x_ref[...]
    y = x * lax.rsqrt(jnp.mean(x * x, axis=-1, keepdims=True) + NORM_EPS) * g_ref[...]
    if modulate:
        y = y * (1.0 + sc_ref[0]) + sh_ref[0]
    o_ref[...] = y.astype(o_ref.dtype)


def _norm(x2, g, seq, shift=None, scale=None, out_dtype=BF16, tm=512):
    m, d = x2.shape
    modulate = shift is not None
    tpb = seq // tm
    in_specs = [pl.BlockSpec((tm, d), lambda i: (i, 0)),
                pl.BlockSpec((1, d), lambda i: (0, 0))]
    args = [x2, g.reshape(1, d)]
    if modulate:
        row = pl.BlockSpec((1, 1, d), lambda i: (i // tpb, 0, 0))
        in_specs += [row, row]
        args += [shift[:, None, :], scale[:, None, :]]
    return pl.pallas_call(
        functools.partial(_norm_kernel, modulate=modulate),
        grid=(m // tm,),
        in_specs=in_specs,
        out_specs=pl.BlockSpec((tm, d), lambda i: (i, 0)),
        out_shape=jax.ShapeDtypeStruct((m, d), out_dtype),
        compiler_params=_cparams(("arbitrary",)),
        name="rmsnorm",
    )(*args)


def _mm_kernel(*refs, epilogue, nk, has_prev):
    a_ref, w_ref = refs[0], refs[1]
    pos = 2
    extra = []
    n_extra = {"none": 0, "relu2": 0, "residual": 2, "gated": 2 if has_prev else 1}[epilogue]
    for _ in range(n_extra):
        extra.append(refs[pos])
        pos += 1
    o_ref = refs[pos]
    acc_ref = refs[pos + 1] if nk > 1 else None

    def finish(y):
        if epilogue == "relu2":
            y = jnp.square(jnp.maximum(y, 0.0))
        elif epilogue == "residual":
            y = extra[0][...] + extra[1][0] * y
        elif epilogue == "gated":
            y = _sigmoid(extra[0][...]) * y
            if has_prev:
                y = extra[1][...] + y
        o_ref[...] = y.astype(o_ref.dtype)

    if nk == 1:
        finish(_dot(a_ref[...], w_ref[...]))
    else:
        k = pl.program_id(2)

        @pl.when(k == 0)
        def _():
            acc_ref[...] = jnp.zeros_like(acc_ref)

        acc_ref[...] += _dot(a_ref[...], w_ref[...])

        @pl.when(k == nk - 1)
        def _():
            finish(acc_ref[...])


def _matmul(a, w, *, out_dtype, epilogue="none", tm=1024, tn=512, tk=None, a_col=0,
            res=None, gate_row=None, seq=None, gate=None, gate_off=0, prev=None):
    m = a.shape[0]
    kdim, n = w.shape
    tk = kdim if tk is None else tk
    nk = kdim // tk
    tn = min(tn, n)
    a_off = a_col * nk
    in_specs = [pl.BlockSpec((tm, tk), lambda i, j, k: (i, a_off + k)),
                pl.BlockSpec((tk, tn), lambda i, j, k: (k, j))]
    args = [a, w]
    if epilogue == "residual":
        tpb = seq // tm
        in_specs += [pl.BlockSpec((tm, tn), lambda i, j, k: (i, j)),
                     pl.BlockSpec((1, 1, tn), lambda i, j, k: (i // tpb, 0, j))]
        args += [res, gate_row[:, None, :]]
    elif epilogue == "gated":
        assert gate_off % tn == 0
        g_off = gate_off // tn
        in_specs += [pl.BlockSpec((tm, tn), lambda i, j, k: (i, g_off + j))]
        args += [gate]
        if prev is not None:
            in_specs += [pl.BlockSpec((tm, tn), lambda i, j, k: (i, j))]
            args += [prev]
    scratch = [pltpu.VMEM((tm, tn), F32)] if nk > 1 else []
    return pl.pallas_call(
        functools.partial(_mm_kernel, epilogue=epilogue, nk=nk, has_prev=prev is not None),
        grid=(m // tm, n // tn, nk),
        in_specs=in_specs,
        out_specs=pl.BlockSpec((tm, tn), lambda i, j, k: (i, j)),
        out_shape=jax.ShapeDtypeStruct((m, n), out_dtype),
        scratch_shapes=scratch,
        compiler_params=_cparams(("arbitrary", "arbitrary", "arbitrary")),
        name="matmul_" + epilogue,
    )(*args)


def _rope_rows(period, half, rot):
    j = np.arange(LANES) % period
    inv = ROPE_THETA ** (-(np.arange(half, dtype=np.float32)) / np.float32(half))
    inv_row = np.where(j < rot, inv.astype(np.float32)[j % half], 0.0).astype(np.float32)
    sign_row = np.where(j < half, -1.0, np.where(j < rot, 1.0, 0.0)).astype(np.float32)
    first_row = (j < half).astype(np.float32)
    return np.stack([inv_row, sign_row, first_row])[:, None, :]


def _rope_apply(x, cos_t, sin_s, first, half):
    up = pltpu.roll(x, LANES - half, axis=1)
    dn = pltpu.roll(x, half, axis=1)
    sw = jnp.where(first > 0.5, up, dn)
    return x * cos_t + sw * sin_s


def _rope_kernel(pos_ref, rows_a_ref, rows_i_ref, q_ref, kv_ref, qi_ref, sm_ref,
                 qo_ref, ko_ref, vo_ref, qio_ref, kio_ref, wio_ref, *, w_scale):
    pos = pos_ref[...].astype(F32)
    ang_a = pos * rows_a_ref[0]
    cos_a, sin_a = jnp.cos(ang_a), jnp.sin(ang_a) * rows_a_ref[1]
    first_a = rows_a_ref[2]
    ang_i = pos * rows_i_ref[0]
    cos_i, sin_i = jnp.cos(ang_i), jnp.sin(ang_i) * rows_i_ref[1]
    first_i = rows_i_ref[2]
    half_a = ATT_HEAD_DIM // ROPE_FRACTION // 2
    half_i = IDX_HEAD_DIM // ROPE_FRACTION // 2
    for h in range(ATT_HEADS):
        sl = slice(h * LANES, (h + 1) * LANES)
        qo_ref[:, sl] = _rope_apply(q_ref[:, sl], cos_a, sin_a, first_a, half_a).astype(BF16)
    for h in range(ATT_KV_HEADS):
        sl = slice(h * LANES, (h + 1) * LANES)
        ko_ref[:, sl] = _rope_apply(kv_ref[:, sl], cos_a, sin_a, first_a, half_a).astype(BF16)
    vo_ref[...] = kv_ref[:, ATT_KV_W:].astype(BF16)
    for h in range(IDX_Q_W // LANES):
        sl = slice(h * LANES, (h + 1) * LANES)
        qio_ref[:, sl] = _rope_apply(qi_ref[:, sl], cos_i, sin_i, first_i, half_i).astype(BF16)
    sm = sm_ref[...]
    ki = _rope_apply(sm, cos_i, sin_i, first_i, half_i)
    kio_ref[...] = ki[:, :IDX_HEAD_DIM].astype(BF16)
    wio_ref[...] = sm[:, IDX_HEAD_DIM:IDX_HEAD_DIM + IDX_HEADS] * w_scale


def _rope_split(proj, small, positions, tm=512):
    m = proj.shape[0]
    rows_a = jnp.asarray(_rope_rows(ATT_HEAD_DIM, ATT_HEAD_DIM // ROPE_FRACTION // 2,
                                    ATT_HEAD_DIM // ROPE_FRACTION))
    rows_i = jnp.asarray(_rope_rows(IDX_HEAD_DIM, IDX_HEAD_DIM // ROPE_FRACTION // 2,
                                    IDX_HEAD_DIM // ROPE_FRACTION))
    w_scale = (IDX_HEADS ** -0.5) * (IDX_HEAD_DIM ** -0.5)
    rows_spec = pl.BlockSpec((3, 1, LANES), lambda i: (0, 0, 0))
    outs = pl.pallas_call(
        functools.partial(_rope_kernel, w_scale=w_scale),
        grid=(m // tm,),
        in_specs=[
            pl.BlockSpec((tm, 1), lambda i: (i, 0)),
            rows_spec, rows_spec,
            pl.BlockSpec((tm, ATT_Q_W), lambda i: (i, 0)),
            pl.BlockSpec((tm, 2 * ATT_KV_W), lambda i: (i, ATT_Q_W // (2 * ATT_KV_W))),
            pl.BlockSpec((tm, IDX_Q_W), lambda i: (i, (ATT_Q_W + 2 * ATT_KV_W) // IDX_Q_W)),
            pl.BlockSpec((tm, LANES), lambda i: (i, 0)),
        ],
        out_specs=[
            pl.BlockSpec((tm, ATT_Q_W), lambda i: (i, 0)),
            pl.BlockSpec((tm, ATT_KV_W), lambda i: (i, 0)),
            pl.BlockSpec((tm, ATT_KV_W), lambda i: (i, 0)),
            pl.BlockSpec((tm, IDX_Q_W), lambda i: (i, 0)),
            pl.BlockSpec((tm, IDX_HEAD_DIM), lambda i: (i, 0)),
            pl.BlockSpec((tm, IDX_HEADS), lambda i: (i, 0)),
        ],
        out_shape=[
            jax.ShapeDtypeStruct((m, ATT_Q_W), BF16),
            jax.ShapeDtypeStruct((m, ATT_KV_W), BF16),
            jax.ShapeDtypeStruct((m, ATT_KV_W), BF16),
            jax.ShapeDtypeStruct((m, IDX_Q_W), BF16),
            jax.ShapeDtypeStruct((m, IDX_HEAD_DIM), BF16),
            jax.ShapeDtypeStruct((m, IDX_HEADS), F32),
        ],
        compiler_params=_cparams(("arbitrary",)),
        name="rope_split",
    )(positions.reshape(m, 1), rows_a, rows_i, proj, proj, proj, small)
    return outs


def _attn_kernel(qi_ref, wi_ref, ki_ref, q_ref, k_ref, v_ref, o_ref, skey_ref, *, tq, kc, topk, seq):
    i = pl.program_id(1)
    nchunk = lax.shift_right_logical((i + 1) * tq + (kc - 1), int(np.log2(kc)))
    qpos = i * tq + lax.broadcasted_iota(jnp.int32, (tq, 1), 0)
    lane_iota = lax.broadcasted_iota(jnp.int32, (tq, LANES), 1)
    ncol = kc // LANES

    qi = qi_ref[0]
    wi = wi_ref[0]
    qi_h = [qi[:, h * IDX_HEAD_DIM:(h + 1) * IDX_HEAD_DIM] for h in range(IDX_HEADS)]
    wi_h = [wi[:, h:h + 1] for h in range(IDX_HEADS)]

    def score_chunk(c, carry):
        k0 = pl.multiple_of(c * kc, kc)
        kic = ki_ref[0, pl.ds(k0, kc), :]
        acc = jnp.zeros((tq, kc), F32)
        for h in range(IDX_HEADS):
            acc = acc + jnp.maximum(_dot_nt(qi_h[h], kic), 0.0) * wi_h[h]
        kpos = k0 + lax.broadcasted_iota(jnp.int32, (tq, kc), 1)
        bits = pltpu.bitcast(acc, jnp.int32)
        skey = jnp.where(bits < 0, bits ^ jnp.int32(0x7FFFFFFF), bits)
        skey_ref[:, pl.ds(k0, kc)] = jnp.where(kpos <= qpos, skey, jnp.int32(INT_MIN))
        return carry

    lax.fori_loop(0, nchunk, score_chunk, 0)

    def count_where(pred_fn):
        def chunk_body(c, part):
            k0 = pl.multiple_of(c * kc, kc)
            x = skey_ref[:, pl.ds(k0, kc)]
            for j in range(ncol):
                part = part + jnp.where(pred_fn(x[:, j * LANES:(j + 1) * LANES], k0 + j * LANES), 1, 0)
            return part
        part = lax.fori_loop(0, nchunk, chunk_body, jnp.zeros((tq, LANES), jnp.int32))
        return jnp.sum(part, axis=1, keepdims=True)

    def bit_body(b, prefix):
        cand_u = prefix | lax.shift_left(jnp.int32(1), 31 - b)
        cand_s = cand_u ^ jnp.int32(INT_MIN)
        cnt = count_where(lambda x, _: x >= cand_s)
        return jnp.where(cnt >= topk, cand_u, prefix)

    prefix = lax.fori_loop(0, 32, bit_body, jnp.zeros((tq, 1), jnp.int32))
    thr = prefix ^ jnp.int32(INT_MIN)

    n_gt = count_where(lambda x, _: x > thr)
    n_eq = count_where(lambda x, _: x == thr)
    need = topk - n_gt
    excess = jnp.where((prefix != 0) & (n_eq > need), 1, 0)
    any_excess = jnp.max(excess) > 0
    idx_bits = int(np.log2(seq))

    def tie_search():
        def tie_bit(b, x):
            cand = x | lax.shift_left(jnp.int32(1), idx_bits - 1 - b)
            below = count_where(lambda xk, base: (xk == thr) & ((base + lane_iota) < cand))
            return jnp.where(below < need, cand, x)
        return lax.fori_loop(0, idx_bits, tie_bit, jnp.zeros((tq, 1), jnp.int32))

    tie_last = lax.cond(any_excess, tie_search, lambda: jnp.full((tq, 1), seq, jnp.int32))

    groups = ATT_KV_HEADS
    rep = ATT_HEADS // ATT_KV_HEADS
    scale = ATT_HEAD_DIM ** -0.5
    q = q_ref[0]
    q_g = [jnp.concatenate([q[:, (g * rep + r) * LANES:(g * rep + r + 1) * LANES] for r in range(rep)], axis=0)
           for g in range(groups)]
    rows = rep * tq

    def attn_chunk(c, carry):
        k0 = pl.multiple_of(c * kc, kc)
        x = skey_ref[:, pl.ds(k0, kc)]
        kidx = k0 + lax.broadcasted_iota(jnp.int32, (tq, kc), 1)
        sel = (x > thr) | ((x == thr) & (kidx <= tie_last))
        sel = sel & (x != jnp.int32(INT_MIN))
        sel4 = jnp.concatenate([jnp.where(sel, 1.0, 0.0)] * rep, axis=0) > 0.5
        new = []
        for g in range(groups):
            m_old, l_old, acc_old = carry[g]
            kg = k_ref[0, pl.ds(k0, kc), g * LANES:(g + 1) * LANES]
            vg = v_ref[0, pl.ds(k0, kc), g * LANES:(g + 1) * LANES]
            s = _dot_nt(q_g[g], kg) * scale
            s = jnp.where(sel4, s, MASK_VALUE)
            m_new = jnp.maximum(m_old, jnp.max(s, axis=1, keepdims=True))
            p = jnp.where(sel4, jnp.exp(s - m_new), 0.0)
            alpha = jnp.exp(m_old - m_new)
            l_new = alpha * l_old + jnp.sum(p, axis=1, keepdims=True)
            acc_new = alpha * acc_old + _dot(p.astype(BF16), vg)
            new.append((m_new, l_new, acc_new))
        return tuple(new)

    init = tuple((jnp.full((rows, 1), MASK_VALUE, F32), jnp.zeros((rows, 1), F32),
                  jnp.zeros((rows, LANES), F32)) for _ in range(groups))
    fin = lax.fori_loop(0, nchunk, attn_chunk, init)
    for g in range(groups):
        _, l_f, acc_f = fin[g]
        o = acc_f / l_f
        for r in range(rep):
            h = g * rep + r
            o_ref[0, :, h * LANES:(h + 1) * LANES] = o[r * tq:(r + 1) * tq].astype(o_ref.dtype)


def _dsa_attention(q, k, v, qi, ki, wi, batch, seq, topk, tq=128, kc=512):
    nblk = seq // tq
    r3 = lambda a: a.reshape(batch, seq, a.shape[-1])
    qblk = lambda w: pl.BlockSpec((1, tq, w), lambda b, i: (b, i, 0))
    full = lambda w: pl.BlockSpec((1, seq, w), lambda b, i: (b, 0, 0))
    out = pl.pallas_call(
        functools.partial(_attn_kernel, tq=tq, kc=kc, topk=topk, seq=seq),
        grid=(batch, nblk),
        in_specs=[qblk(IDX_Q_W), qblk(IDX_HEADS), full(IDX_HEAD_DIM),
                  qblk(ATT_Q_W), full(ATT_KV_W), full(ATT_KV_W)],
        out_specs=qblk(ATT_Q_W),
        out_shape=jax.ShapeDtypeStruct((batch, seq, ATT_Q_W), BF16),
        scratch_shapes=[pltpu.VMEM((tq, seq), jnp.int32)],
        compiler_params=_cparams(("arbitrary", "arbitrary")),
        name="dsa_attention",
    )(r3(qi), r3(wi), r3(ki), r3(q), r3(k), r3(v))
    return out.reshape(batch * seq, ATT_Q_W)


def _conv_kernel(cb_ref, cc_ref, cx_ref, hc_ref, hx_ref, w_ref, o_ref, *, tm, seq):
    i = pl.program_id(0)
    u = cc_ref[...] * cx_ref[...]
    halo = hc_ref[...] * hx_ref[...]
    halo = jnp.where((i * tm) % seq == 0, 0.0, halo)
    row = lax.broadcasted_iota(jnp.int32, u.shape, 0)
    u1 = pltpu.roll(u, 1, axis=0)
    u2 = pltpu.roll(u, 2, axis=0)
    h1 = halo[SUBLANES - 1:SUBLANES, :]
    h2 = halo[SUBLANES - 2:SUBLANES - 1, :]
    u1 = jnp.where(row == 0, h1, u1)
    u2 = jnp.where(row == 0, h2, jnp.where(row == 1, h1, u2))
    w = w_ref[...]
    conv = u2 * w[0:1, :] + u1 * w[1:2, :] + u * w[2:3, :]
    o_ref[...] = (cb_ref[...] * conv).astype(o_ref.dtype)


def _short_conv(proj, conv_w, seq, col0, tm=512, tc=512):
    m = proj.shape[0]
    nb = CONV_WIDTH // tc
    c0 = col0 // tc
    rpb = tm // SUBLANES
    main = lambda off: pl.BlockSpec((tm, tc), lambda i, j: (i, c0 + off * nb + j))
    halo = lambda off: pl.BlockSpec((SUBLANES, tc),
                                    lambda i, j: (jnp.maximum(i * rpb - 1, 0), c0 + off * nb + j))
    return pl.pallas_call(
        functools.partial(_conv_kernel, tm=tm, seq=seq),
        grid=(m // tm, nb),
        in_specs=[main(0), main(1), main(2), halo(1), halo(2),
                  pl.BlockSpec((CONV_K, tc), lambda i, j: (0, j))],
        out_specs=pl.BlockSpec((tm, tc), lambda i, j: (i, j)),
        out_shape=jax.ShapeDtypeStruct((m, CONV_WIDTH), BF16),
        compiler_params=_cparams(("arbitrary", "arbitrary")),
        name="short_conv",
    )(proj, proj, proj, proj, proj, conv_w)


def _hg_tables():
    c = HG_CHUNK
    t = np.arange(c)
    mats, masks = [], []
    for l in range(HG_LEVELS):
        h = 1 << l
        start = (t // (2 * h)) * (2 * h)
        p = start + h - 1
        right = (t - start) >= h
        u = t[None, :]
        r_m = right[:, None] & (u > p[:, None]) & (u <= t[:, None])
        l_m = (~right)[:, None] & (u > t[:, None]) & (u <= p[:, None])
        mats += [r_m, l_m]
        same = (t[:, None] // (2 * h)) == (t[None, :] // (2 * h))
        masks.append(same & right[:, None] & (~right)[None, :])
    mats.append(t[None, :] <= t[:, None])
    mats.append(t[None, :] > t[:, None])
    return (np.concatenate(mats, 0).astype(np.float32),
            np.stack(masks).astype(np.float32))


def _hg_kernel(hq_ref, hf_ref, hi_ref, hg_ref, lb_ref, g_ref, mat_ref, mask_ref, o_ref, *, seq):
    c = HG_CHUNK
    lb = lb_ref[...]
    g = g_ref[...]

    def chunk(ci, state_t):
        r0 = pl.multiple_of(ci * c, c)
        rows = pl.ds(r0, c)
        f = lb + (1.0 - lb) * _sigmoid(hf_ref[rows, :])
        logf = jnp.log(jnp.maximum(f, F_MIN))
        kin = 1.0 - f
        q = hq_ref[rows, :]
        v = hi_ref[rows, :]
        v16 = v.astype(BF16)
        e = jnp.exp(jnp.dot(mat_ref[...], logf, precision=lax.Precision.HIGHEST, preferred_element_type=F32))
        attn = jnp.zeros((c, c), F32)
        for l in range(HG_LEVELS):
            ql = (q * e[(2 * l) * c:(2 * l + 1) * c]).astype(BF16)
            kl = (kin * e[(2 * l + 1) * c:(2 * l + 2) * c]).astype(BF16)
            attn = attn + _dot_nt(ql, kl) * mask_ref[l]
        e_b = e[2 * HG_LEVELS * c:(2 * HG_LEVELS + 1) * c]
        e_k = e[(2 * HG_LEVELS + 1) * c:(2 * HG_LEVELS + 2) * c]
        diag = jnp.sum(q * kin, axis=1, keepdims=True)
        out = (_dot(attn.astype(BF16), v16) + diag * v
               + _dot_nt((q * e_b).astype(BF16), state_t.astype(BF16)))
        kv_t = lax.dot_general(v16, (kin * e_k).astype(BF16), (((0,), (0,)), ((), ())),
                               preferred_element_type=F32)
        state_t = state_t * e_b[c - 1:c, :] + kv_t
        y = out * lax.rsqrt(jnp.mean(out * out, axis=-1, keepdims=True) + NORM_EPS) * g
        gate = hg_ref[rows, :]
        o_ref[rows, :] = (y * (gate * _sigmoid(gate))).astype(o_ref.dtype)
        return state_t

    lax.fori_loop(0, seq // c, chunk, jnp.zeros((HG_VAL_DIM, HG_KEY_DIM), F32))


def _hgrn2(proj, lb, norm_g, batch, seq, col0):
    m = proj.shape[0]
    c0 = col0 // LANES
    mat, masks = _hg_tables()
    col = lambda off: pl.BlockSpec((seq, LANES), lambda b, h: (b, c0 + off * HG_HEADS + h))
    return pl.pallas_call(
        functools.partial(_hg_kernel, seq=seq),
        grid=(batch, HG_HEADS),
        in_specs=[col(0), col(1), col(2), col(3),
                  pl.BlockSpec((1, LANES), lambda b, h: (0, h)),
                  pl.BlockSpec((1, LANES), lambda b, h: (0, 0)),
                  pl.BlockSpec(mat.shape, lambda b, h: (0, 0)),
                  pl.BlockSpec(masks.shape, lambda b, h: (0, 0, 0))],
        out_specs=pl.BlockSpec((seq, LANES), lambda b, h: (b, h)),
        out_shape=jax.ShapeDtypeStruct((m, HG_W), BF16),
        compiler_params=_cparams(("arbitrary", "arbitrary")),
        name="hgrn2",
    )(proj, proj, proj, proj, lb.reshape(1, HG_W), norm_g.reshape(1, HG_VAL_DIM),
      jnp.asarray(mat), jnp.asarray(masks))


def _lb_kernel(x_ref, o_ref):
    x = x_ref[...]
    e = jnp.exp(x - jnp.max(x, axis=0, keepdims=True))
    p = e / jnp.sum(e, axis=0, keepdims=True)
    depth = x.shape[0]
    run = jnp.zeros_like(p[0:1])
    for l in range(depth):
        run = run + p[l:l + 1]
        o_ref[l:l + 1, :] = run - p[0:1]


def _lower_bounds(hg_lower_bounds):
    return pl.pallas_call(
        _lb_kernel,
        out_shape=jax.ShapeDtypeStruct(hg_lower_bounds.shape, F32),
        name="hg_lower_bounds",
    )(hg_lower_bounds.astype(F32))


_COL_Q = 0
_COL_CONV = ATT_Q_W + 2 * ATT_KV_W + IDX_Q_W
_COL_HG = _COL_CONV + 3 * CONV_WIDTH
_COL_GATE = _COL_HG + 4 * HG_W


def _split_w_in(w_in, d_model):
    sizes = (ATT_Q_W, ATT_KV_W, ATT_KV_W, IDX_Q_W, IDX_HEAD_DIM, IDX_HEADS,
             CONV_WIDTH, CONV_WIDTH, CONV_WIDTH, HG_W, HG_W, HG_W, HG_W, N_BRANCHES * d_model)
    offs = np.cumsum((0,) + sizes)
    part = lambda n: w_in[:, offs[n]:offs[n + 1]]
    main = jnp.concatenate([part(0), part(1), part(2), part(3)] + [part(n) for n in range(6, 14)],
                           axis=1).astype(BF16)
    small = jnp.concatenate([part(4), part(5)], axis=1)
    small = jnp.pad(small, ((0, 0), (0, LANES - small.shape[1]))).astype(BF16)
    return main, small


def kernel(x, c, positions, ada_w, ada_b, norm_mix_g, w_in, conv_w, hg_lower_bounds, hg_norm_g,
           w_o_attn, w_o_conv, w_o_hgrn, w_out, norm_mlp_g, w_mlp1, w_mlp2, final_norm_g):
    batch, seq, d = x.shape
    depth = ada_w.shape[0]
    m = batch * seq
    topk = min(INDEX_TOPK, seq // 4)
    lb_all = _lower_bounds(hg_lower_bounds)
    mod = _modulation(c, ada_w, ada_b)
    x2 = x.reshape(m, d)
    for l in range(depth):
        sh1, sc1, g1, sh2, sc2, g2 = [mod[l, :, n * d:(n + 1) * d] for n in range(ADA_CHUNKS)]
        w_main, w_small = _split_w_in(w_in[l], d)
        h = _norm(x2, norm_mix_g[l], seq, sh1, sc1)
        proj = _matmul(h, w_main, out_dtype=F32, tn=768)
        small = _matmul(h, w_small, out_dtype=F32)
        q, k, v, qi, ki, wi = _rope_split(proj, small, positions)
        att = _dsa_attention(q, k, v, qi, ki, wi, batch, seq, topk)
        cv = _short_conv(proj, conv_w[l], seq, _COL_CONV)
        hg = _hgrn2(proj, lb_all[l], hg_norm_g[l], batch, seq, _COL_HG)
        mrg = _matmul(att, w_o_attn[l].astype(BF16), out_dtype=F32, epilogue="gated",
                      gate=proj, gate_off=_COL_GATE)
        mrg = _matmul(cv, w_o_conv[l].astype(BF16), out_dtype=F32, epilogue="gated",
                      gate=proj, gate_off=_COL_GATE + d, prev=mrg)
        mrg = _matmul(hg, w_o_hgrn[l].astype(BF16), out_dtype=BF16, epilogue="gated",
                      gate=proj, gate_off=_COL_GATE + 2 * d, prev=mrg)
        x2 = _matmul(mrg, w_out[l].astype(BF16), out_dtype=F32, epilogue="residual",
                     res=x2, gate_row=g1, seq=seq)
        h = _norm(x2, norm_mlp_g[l], seq, sh2, sc2)
        a = _matmul(h, w_mlp1[l].astype(BF16), out_dtype=BF16, epilogue="relu2")
        x2 = _matmul(a, w_mlp2[l].astype(BF16), out_dtype=F32, epilogue="residual", tk=2048,
                     res=x2, gate_row=g2, seq=seq)
    out = _norm(x2, final_norm_g, seq, out_dtype=x.dtype)
    return out.reshape(batch, seq, d)
```

```python
import functools

import numpy as np
import jax
import jax.numpy as jnp
from jax import lax
from jax.experimental import pallas as pl
from jax.experimental.pallas import tpu as pltpu

ATT_HEADS = 8
ATT_KV_HEADS = 2
ATT_HEAD_DIM = 128
IDX_HEADS = 8
IDX_HEAD_DIM = 64
INDEX_TOPK = 256
ROPE_THETA = 500000.0
ROPE_FRACTION = 4
MASK_VALUE = -1e30
CONV_WIDTH = 1024
CONV_K = 3
HG_HEADS = 8
HG_KEY_DIM = 128
HG_VAL_DIM = 128
F_MIN = 1e-30
N_BRANCHES = 3
NORM_EPS = 1e-6
ADA_CHUNKS = 6

ATT_Q_W = ATT_HEADS * ATT_HEAD_DIM
ATT_KV_W = ATT_KV_HEADS * ATT_HEAD_DIM
IDX_Q_W = IDX_HEADS * IDX_HEAD_DIM
HG_W = HG_HEADS * HG_KEY_DIM

LANES = 128
SUBLANES = 8
VMEM_LIMIT_BYTES = 56 * 1024 * 1024

INT_MIN = -2 ** 31
HALF_BITS = 16
HG_CHUNK = 64
HG_LEVELS = 6
HG_SPLIT = 3

BF16 = jnp.bfloat16
F32 = jnp.float32


def _cparams(sem):
    return pltpu.CompilerParams(dimension_semantics=sem, vmem_limit_bytes=VMEM_LIMIT_BYTES)


def _dot(a, b):
    return jnp.dot(a, b, preferred_element_type=F32)


def _dot_nt(a, b):
    return lax.dot_general(a, b, (((1,), (1,)), ((), ())), preferred_element_type=F32)


def _sigmoid(x):
    return 1.0 / (1.0 + jnp.exp(-x))


def _mod_kernel(c_ref, w_ref, b_ref, o_ref):
    c = c_ref[...]
    ca = (c * _sigmoid(c)).astype(BF16)
    o_ref[...] = _dot(ca, w_ref[...].astype(BF16)) + b_ref[...]


def _modulation(c, ada_w, ada_b):
    depth, d, n = ada_w.shape
    b = c.shape[0]
    bp = -(-b // SUBLANES) * SUBLANES
    cp = jnp.pad(c, ((0, bp - b), (0, 0)))
    tn = 1024
    out = pl.pallas_call(
        _mod_kernel,
        grid=(depth, n // tn),
        in_specs=[
            pl.BlockSpec((bp, d), lambda l, j: (0, 0)),
            pl.BlockSpec((None, d, tn), lambda l, j: (l, 0, j)),
            pl.BlockSpec((None, 1, tn), lambda l, j: (l, 0, j)),
        ],
        out_specs=pl.BlockSpec((None, bp, tn), lambda l, j: (l, 0, j)),
        out_shape=jax.ShapeDtypeStruct((depth, bp, n), F32),
        compiler_params=_cparams(("arbitrary", "arbitrary")),
        name="adaln_mod",
    )(cp, ada_w, ada_b.reshape(depth, 1, n))
    return out[:, :b]


def _norm_kernel(x_ref, g_ref, *rest, modulate):
    if modulate:
        sh_ref, sc_ref, o_ref = rest
    else:
        (o_ref,) = rest
    x = x_ref[...]
    y = x * lax.rsqrt(jnp.mean(x * x, axis=-1, keepdims=True) + NORM_EPS) * g_ref[...]
    if modulate:
        y = y * (1.0 + sc_ref[0]) + sh_ref[0]
    o_ref[...] = y.astype(o_ref.dtype)


def _norm(x2, g, seq, shift=None, scale=None, out_dtype=BF16, tm=512):
    m, d = x2.shape
    modulate = shift is not None
    tpb = seq // tm
    in_specs = [pl.BlockSpec((tm, d), lambda i: (i, 0)),
                pl.BlockSpec((1, d), lambda i: (0, 0))]
    args = [x2, g.reshape(1, d)]
    if modulate:
        row = pl.BlockSpec((1, 1, d), lambda i: (i // tpb, 0, 0))
        in_specs += [row, row]
        args += [shift[:, None, :], scale[:, None, :]]
    return pl.pallas_call(
        functools.partial(_norm_kernel, modulate=modulate),
        grid=(m // tm,),
        in_specs=in_specs,
        out_specs=pl.BlockSpec((tm, d), lambda i: (i, 0)),
        out_shape=jax.ShapeDtypeStruct((m, d), out_dtype),
        compiler_params=_cparams(("arbitrary",)),
        name="rmsnorm",
    )(*args)


def _mm_kernel(*refs, epilogue, nk, has_prev):
    a_ref, w_ref = refs[0], refs[1]
    pos = 2
    extra = []
    n_extra = {"none": 0, "relu2": 0, "residual": 2, "gated": 2 if has_prev else 1}[epilogue]
    for _ in range(n_extra):
        extra.append(refs[pos])
        pos += 1
    o_ref = refs[pos]
    acc_ref = refs[pos + 1] if nk > 1 else None

    def finish(y):
        if epilogue == "relu2":
            y = jnp.square(jnp.maximum(y, 0.0))
        elif epilogue == "residual":
            y = extra[0][...] + extra[1][0] * y
        elif epilogue == "gated":
            y = _sigmoid(extra[0][...]) * y
            if has_prev:
                y = extra[1][...] + y
        o_ref[...] = y.astype(o_ref.dtype)

    if nk == 1:
        finish(_dot(a_ref[...], w_ref[...]))
    else:
        k = pl.program_id(2)

        @pl.when(k == 0)
        def _():
            acc_ref[...] = jnp.zeros_like(acc_ref)

        acc_ref[...] += _dot(a_ref[...], w_ref[...])

        @pl.when(k == nk - 1)
        def _():
            finish(acc_ref[...])


def _matmul(a, w, *, out_dtype, epilogue="none", tm=1024, tn=512, tk=None, a_col=0,
            res=None, gate_row=None, seq=None, gate=None, gate_off=0, prev=None):
    m = a.shape[0]
    kdim, n = w.shape
    tk = kdim if tk is None else tk
    nk = kdim // tk
    tn = min(tn, n)
    a_off = a_col * nk
    in_specs = [pl.BlockSpec((tm, tk), lambda i, j, k: (i, a_off + k)),
                pl.BlockSpec((tk, tn), lambda i, j, k: (k, j))]
    args = [a, w]
    if epilogue == "residual":
        tpb = seq // tm
        in_specs += [pl.BlockSpec((tm, tn), lambda i, j, k: (i, j)),
                     pl.BlockSpec((1, 1, tn), lambda i, j, k: (i // tpb, 0, j))]
        args += [res, gate_row[:, None, :]]
    elif epilogue == "gated":
        assert gate_off % tn == 0
        g_off = gate_off // tn
        in_specs += [pl.BlockSpec((tm, tn), lambda i, j, k: (i, g_off + j))]
        args += [gate]
        if prev is not None:
            in_specs += [pl.BlockSpec((tm, tn), lambda i, j, k: (i, j))]
            args += [prev]
    scratch = [pltpu.VMEM((tm, tn), F32)] if nk > 1 else []
    return pl.pallas_call(
        functools.partial(_mm_kernel, epilogue=epilogue, nk=nk, has_prev=prev is not None),
        grid=(m // tm, n // tn, nk),
        in_specs=in_specs,
        out_specs=pl.BlockSpec((tm, tn), lambda i, j, k: (i, j)),
        out_shape=jax.ShapeDtypeStruct((m, n), out_dtype),
        scratch_shapes=scratch,
        compiler_params=_cparams(("arbitrary", "arbitrary", "arbitrary")),
        name="matmul_" + epilogue,
    )(*args)


def _rope_rows(period, half, rot):
    j = np.arange(LANES) % period
    inv = ROPE_THETA ** (-(np.arange(half, dtype=np.float32)) / np.float32(half))
    inv_row = np.where(j < rot, inv.astype(np.float32)[j % half], 0.0).astype(np.float32)
    sign_row = np.where(j < half, -1.0, np.where(j < rot, 1.0, 0.0)).astype(np.float32)
    first_row = (j < half).astype(np.float32)
    return np.stack([inv_row, sign_row, first_row])[:, None, :]


def _rope_apply(x, cos_t, sin_s, first, half):
    up = pltpu.roll(x, LANES - half, axis=1)
    dn = pltpu.roll(x, half, axis=1)
    sw = jnp.where(first > 0.5, up, dn)
    return x * cos_t + sw * sin_s


def _rope_kernel(pos_ref, rows_a_ref, rows_i_ref, q_ref, kv_ref, qi_ref, sm_ref,
                 qo_ref, ko_ref, vo_ref, qio_ref, kio_ref, wio_ref, *, w_scale):
    pos = pos_ref[...].astype(F32)
    ang_a = pos * rows_a_ref[0]
    cos_a, sin_a = jnp.cos(ang_a), jnp.sin(ang_a) * rows_a_ref[1]
    first_a = rows_a_ref[2]
    ang_i = pos * rows_i_ref[0]
    cos_i, sin_i = jnp.cos(ang_i), jnp.sin(ang_i) * rows_i_ref[1]
    first_i = rows_i_ref[2]
    half_a = ATT_HEAD_DIM // ROPE_FRACTION // 2
    half_i = IDX_HEAD_DIM // ROPE_FRACTION // 2
    for h in range(ATT_HEADS):
        sl = slice(h * LANES, (h + 1) * LANES)
        qo_ref[:, sl] = _rope_apply(q_ref[:, sl], cos_a, sin_a, first_a, half_a).astype(BF16)
    for h in range(ATT_KV_HEADS):
        sl = slice(h * LANES, (h + 1) * LANES)
        ko_ref[:, sl] = _rope_apply(kv_ref[:, sl], cos_a, sin_a, first_a, half_a).astype(BF16)
    vo_ref[...] = kv_ref[:, ATT_KV_W:].astype(BF16)
    for h in range(IDX_Q_W // LANES):
        sl = slice(h * LANES, (h + 1) * LANES)
        qio_ref[:, sl] = _rope_apply(qi_ref[:, sl], cos_i, sin_i, first_i, half_i).astype(BF16)
    sm = sm_ref[...]
    ki = _rope_apply(sm, cos_i, sin_i, first_i, half_i)
    kio_ref[...] = ki[:, :IDX_HEAD_DIM].astype(BF16)
    wio_ref[...] = sm[:, IDX_HEAD_DIM:IDX_HEAD_DIM + IDX_HEADS] * w_scale


def _rope_split(proj, small, positions, tm=512):
    m = proj.shape[0]
    rows_a = jnp.asarray(_rope_rows(ATT_HEAD_DIM, ATT_HEAD_DIM // ROPE_FRACTION // 2,
                                    ATT_HEAD_DIM // ROPE_FRACTION))
    rows_i = jnp.asarray(_rope_rows(IDX_HEAD_DIM, IDX_HEAD_DIM // ROPE_FRACTION // 2,
                                    IDX_HEAD_DIM // ROPE_FRACTION))
    w_scale = (IDX_HEADS ** -0.5) * (IDX_HEAD_DIM ** -0.5)
    rows_spec = pl.BlockSpec((3, 1, LANES), lambda i: (0, 0, 0))
    outs = pl.pallas_call(
        functools.partial(_rope_kernel, w_scale=w_scale),
        grid=(m // tm,),
        in_specs=[
            pl.BlockSpec((tm, 1), lambda i: (i, 0)),
            rows_spec, rows_spec,
            pl.BlockSpec((tm, ATT_Q_W), lambda i: (i, 0)),
            pl.BlockSpec((tm, 2 * ATT_KV_W), lambda i: (i, ATT_Q_W // (2 * ATT_KV_W))),
            pl.BlockSpec((tm, IDX_Q_W), lambda i: (i, (ATT_Q_W + 2 * ATT_KV_W) // IDX_Q_W)),
            pl.BlockSpec((tm, LANES), lambda i: (i, 0)),
        ],
        out_specs=[
            pl.BlockSpec((tm, ATT_Q_W), lambda i: (i, 0)),
            pl.BlockSpec((tm, ATT_KV_W), lambda i: (i, 0)),
            pl.BlockSpec((tm, ATT_KV_W), lambda i: (i, 0)),
            pl.BlockSpec((tm, IDX_Q_W), lambda i: (i, 0)),
            pl.BlockSpec((tm, IDX_HEAD_DIM), lambda i: (i, 0)),
            pl.BlockSpec((tm, IDX_HEADS), lambda i: (i, 0)),
        ],
        out_shape=[
            jax.ShapeDtypeStruct((m, ATT_Q_W), BF16),
            jax.ShapeDtypeStruct((m, ATT_KV_W), BF16),
            jax.ShapeDtypeStruct((m, ATT_KV_W), BF16),
            jax.ShapeDtypeStruct((m, IDX_Q_W), BF16),
            jax.ShapeDtypeStruct((m, IDX_HEAD_DIM), BF16),
            jax.ShapeDtypeStruct((m, IDX_HEADS), F32),
        ],
        compiler_params=_cparams(("arbitrary",)),
        name="rope_split",
    )(positions.reshape(m, 1), rows_a, rows_i, proj, proj, proj, small)
    return outs


def _attn_kernel(qi_ref, wi_ref, ki_ref, q_ref, k_ref, v_ref, o_ref, skey_ref, hi_ref, lo_ref,
                 da_ref, db_ref, *, tq, kc, topk, seq):
    i = pl.program_id(1)
    nchunk = lax.shift_right_logical((i + 1) * tq + (kc - 1), int(np.log2(kc)))
    qpos = i * tq + lax.broadcasted_iota(jnp.int32, (tq, 1), 0)
    lane_iota = lax.broadcasted_iota(jnp.int32, (tq, LANES), 1)
    ncol = kc // LANES
    i16 = jnp.int16
    half_bias = 1 << (HALF_BITS - 1)

    qi = qi_ref[0]
    wi = wi_ref[0]
    qi_s = jnp.concatenate([qi[:, h * IDX_HEAD_DIM:(h + 1) * IDX_HEAD_DIM] for h in range(IDX_HEADS)],
                           axis=0)
    wi_b = [jnp.broadcast_to(wi[:, h:h + 1], (tq, LANES)) for h in range(IDX_HEADS)]

    kh = kc // 2
    last_half = seq // kh - 1

    def score_dots(half, dst_ref):
        k0 = pl.multiple_of(jnp.minimum(half, last_half) * kh, kh)
        dst_ref[...] = _dot_nt(qi_s, ki_ref[0, pl.ds(k0, kh), :])

    def score_keys(half, src_ref):
        k0 = pl.multiple_of(half * kh, kh)
        for j in range(kh // LANES):
            cols = slice(j * LANES, (j + 1) * LANES)
            acc = jnp.zeros((tq, LANES), F32)
            for h in range(IDX_HEADS):
                acc = acc + jnp.maximum(src_ref[h * tq:(h + 1) * tq, cols], 0.0) * wi_b[h]
            kpos = k0 + j * LANES + lane_iota
            bits = pltpu.bitcast(acc, jnp.int32)
            skey = jnp.where(bits < 0, bits ^ jnp.int32(0x7FFFFFFF), bits)
            skey = jnp.where(kpos <= qpos, skey, jnp.int32(INT_MIN))
            dst = pl.ds(k0 + j * LANES, LANES)
            skey_ref[:, dst] = skey
            hi_ref[:, dst] = lax.shift_right_arithmetic(skey, HALF_BITS).astype(i16)
            lo_ref[:, dst] = ((skey & jnp.int32(2 * half_bias - 1)) - half_bias).astype(i16)

    def score_chunk(c, carry):
        score_dots(2 * c + 1, db_ref)
        score_keys(2 * c, da_ref)
        score_dots(2 * c + 2, da_ref)
        score_keys(2 * c + 1, db_ref)
        return carry

    score_dots(0, da_ref)
    lax.fori_loop(0, nchunk, score_chunk, 0)

    def count_where(pred_fn):
        def chunk_body(c, part):
            k0 = pl.multiple_of(c * kc, kc)
            x = skey_ref[:, pl.ds(k0, kc)]
            for j in range(ncol):
                part = part + jnp.where(pred_fn(x[:, j * LANES:(j + 1) * LANES], k0 + j * LANES), 1, 0)
            return part
        part = lax.fori_loop(0, nchunk, chunk_body, jnp.zeros((tq, LANES), jnp.int32))
        return jnp.sum(part, axis=1, keepdims=True)

    def count16(ref, pred_fn):
        def chunk_body(c, part):
            k0 = pl.multiple_of(c * kc, kc)
            x = ref[:, pl.ds(k0, kc)]
            for j in range(ncol):
                part = part + jnp.where(pred_fn(x[:, j * LANES:(j + 1) * LANES]), i16(1), i16(0))
            return part
        part = lax.fori_loop(0, nchunk, chunk_body, jnp.zeros((tq, LANES), i16))
        return jnp.sum(part.astype(jnp.int32), axis=1, keepdims=True)

    def search16(ref, want):
        def bit_body(b, prefix):
            cand = prefix | lax.shift_left(jnp.int32(1), HALF_BITS - 1 - b)
            cand16 = jnp.broadcast_to(cand - half_bias, (tq, LANES)).astype(i16)
            cnt = count16(ref, lambda x: x >= cand16)
            return jnp.where(cnt >= want, cand, prefix)
        return lax.fori_loop(0, HALF_BITS, bit_body, jnp.zeros((tq, 1), jnp.int32))

    p_hi = search16(hi_ref, topk)
    t_hi = jnp.broadcast_to(p_hi - half_bias, (tq, LANES)).astype(i16)
    n_hi_gt = count16(hi_ref, lambda x: x > t_hi)

    def bucket_chunk(c, carry):
        k0 = pl.multiple_of(c * kc, kc)
        for j in range(ncol):
            cols = pl.ds(k0 + j * LANES, LANES)
            lo_ref[:, cols] = jnp.where(hi_ref[:, cols] == t_hi, lo_ref[:, cols], i16(-half_bias))
        return carry

    lax.fori_loop(0, nchunk, bucket_chunk, 0)
    p_lo = search16(lo_ref, topk - n_hi_gt)
    prefix = lax.shift_left(p_hi, HALF_BITS) | p_lo
    thr = prefix ^ jnp.int32(INT_MIN)

    n_gt = count_where(lambda x, _: x > thr)
    n_eq = count_where(lambda x, _: x == thr)
    need = topk - n_gt
    excess = jnp.where((prefix != 0) & (n_eq > need), 1, 0)
    any_excess = jnp.max(excess) > 0
    idx_bits = int(np.log2(seq))

    def tie_search():
        def tie_bit(b, x):
            cand = x | lax.shift_left(jnp.int32(1), idx_bits - 1 - b)
            below = count_where(lambda xk, base: (xk == thr) & ((base + lane_iota) < cand))
            return jnp.where(below < need, cand, x)
        return lax.fori_loop(0, idx_bits, tie_bit, jnp.zeros((tq, 1), jnp.int32))

    tie_last = lax.cond(any_excess, tie_search, lambda: jnp.full((tq, 1), seq, jnp.int32))

    groups = ATT_KV_HEADS
    rep = ATT_HEADS // ATT_KV_HEADS
    scale2 = (ATT_HEAD_DIM ** -0.5) * float(np.log2(np.e))
    q = q_ref[0]
    q_g = [jnp.concatenate([q[:, (g * rep + r) * LANES:(g * rep + r + 1) * LANES] for r in range(rep)], axis=0)
           for g in range(groups)]

    def attn_chunk(c, carry):
        k0 = pl.multiple_of(c * kc, kc)
        x = skey_ref[:, pl.ds(k0, kc)]
        kidx = k0 + lax.broadcasted_iota(jnp.int32, (tq, kc), 1)
        sel = (x > thr) | ((x == thr) & (kidx <= tie_last))
        sel = sel & (x != jnp.int32(INT_MIN))
        bias = jnp.where(sel, 0.0, MASK_VALUE)
        new = []
        for g in range(groups):
            kg = k_ref[0, pl.ds(k0, kc), g * LANES:(g + 1) * LANES]
            vg = v_ref[0, pl.ds(k0, kc), g * LANES:(g + 1) * LANES]
            s = _dot_nt(q_g[g], kg)
            ps, stats = [], []
            for r in range(rep):
                m_old, l_old, _ = carry[g * rep + r]
                sm = s[r * tq:(r + 1) * tq] * scale2 + bias
                m_new = jnp.maximum(m_old, jnp.max(sm, axis=1, keepdims=True))
                p = jnp.exp2(sm - m_new)
                alpha = jnp.exp2(m_old - m_new)
                stats.append((m_new, alpha, alpha * l_old + jnp.sum(p, axis=1, keepdims=True)))
                ps.append(p.astype(BF16))
            pv = _dot(jnp.concatenate(ps, axis=0), vg)
            for r in range(rep):
                m_new, alpha, l_new = stats[r]
                acc_old = carry[g * rep + r][2]
                new.append((m_new, l_new, alpha * acc_old + pv[r * tq:(r + 1) * tq]))
        return tuple(new)

    init = tuple((jnp.full((tq, 1), MASK_VALUE, F32), jnp.zeros((tq, 1), F32),
                  jnp.zeros((tq, LANES), F32)) for _ in range(ATT_HEADS))
    fin = lax.fori_loop(0, nchunk, attn_chunk, init)
    for h in range(ATT_HEADS):
        _, l_f, acc_f = fin[h]
        o_ref[0, :, h * LANES:(h + 1) * LANES] = (acc_f / l_f).astype(o_ref.dtype)


def _dsa_attention(q, k, v, qi, ki, wi, batch, seq, topk, tq=128, kc=512):
    nblk = seq // tq
    r3 = lambda a: a.reshape(batch, seq, a.shape[-1])
    qblk = lambda w: pl.BlockSpec((1, tq, w), lambda b, i: (b, i, 0))
    full = lambda w: pl.BlockSpec((1, seq, w), lambda b, i: (b, 0, 0))
    out = pl.pallas_call(
        functools.partial(_attn_kernel, tq=tq, kc=kc, topk=topk, seq=seq),
        grid=(batch, nblk),
        in_specs=[qblk(IDX_Q_W), qblk(IDX_HEADS), full(IDX_HEAD_DIM),
                  qblk(ATT_Q_W), full(ATT_KV_W), full(ATT_KV_W)],
        out_specs=qblk(ATT_Q_W),
        out_shape=jax.ShapeDtypeStruct((batch, seq, ATT_Q_W), BF16),
        scratch_shapes=[pltpu.VMEM((tq, seq), jnp.int32), pltpu.VMEM((tq, seq), jnp.int16),
                        pltpu.VMEM((tq, seq), jnp.int16),
                        pltpu.VMEM((IDX_HEADS * tq, kc // 2), F32), pltpu.VMEM((IDX_HEADS * tq, kc // 2), F32)],
        compiler_params=_cparams(("arbitrary", "arbitrary")),
        name="dsa_attention",
    )(r3(qi), r3(wi), r3(ki), r3(q), r3(k), r3(v))
    return out.reshape(batch * seq, ATT_Q_W)


def _conv_kernel(cb_ref, cc_ref, cx_ref, hc_ref, hx_ref, w_ref, o_ref, *, tm, seq):
    i = pl.program_id(0)
    u = cc_ref[...] * cx_ref[...]
    halo = hc_ref[...] * hx_ref[...]
    halo = jnp.where((i * tm) % seq == 0, 0.0, halo)
    row = lax.broadcasted_iota(jnp.int32, u.shape, 0)
    u1 = pltpu.roll(u, 1, axis=0)
    u2 = pltpu.roll(u, 2, axis=0)
    h1 = halo[SUBLANES - 1:SUBLANES, :]
    h2 = halo[SUBLANES - 2:SUBLANES - 1, :]
    u1 = jnp.where(row == 0, h1, u1)
    u2 = jnp.where(row == 0, h2, jnp.where(row == 1, h1, u2))
    w = w_ref[...]
    conv = u2 * w[0:1, :] + u1 * w[1:2, :] + u * w[2:3, :]
    o_ref[...] = (cb_ref[...] * conv).astype(o_ref.dtype)


def _short_conv(proj, conv_w, seq, col0, tm=512, tc=512):
    m = proj.shape[0]
    nb = CONV_WIDTH // tc
    c0 = col0 // tc
    rpb = tm // SUBLANES
    main = lambda off: pl.BlockSpec((tm, tc), lambda i, j: (i, c0 + off * nb + j))
    halo = lambda off: pl.BlockSpec((SUBLANES, tc),
                                    lambda i, j: (jnp.maximum(i * rpb - 1, 0), c0 + off * nb + j))
    return pl.pallas_call(
        functools.partial(_conv_kernel, tm=tm, seq=seq),
        grid=(m // tm, nb),
        in_specs=[main(0), main(1), main(2), halo(1), halo(2),
                  pl.BlockSpec((CONV_K, tc), lambda i, j: (0, j))],
        out_specs=pl.BlockSpec((tm, tc), lambda i, j: (i, j)),
        out_shape=jax.ShapeDtypeStruct((m, CONV_WIDTH), BF16),
        compiler_params=_cparams(("arbitrary", "arbitrary")),
        name="short_conv",
    )(proj, proj, proj, proj, proj, conv_w)


def _hg_tables():
    c = HG_CHUNK
    t = np.arange(c)
    mats, masks = [], []
    for l in range(HG_LEVELS):
        h = 1 << l
        start = (t // (2 * h)) * (2 * h)
        p = start + h - 1
        right = (t - start) >= h
        u = t[None, :]
        r_m = right[:, None] & (u > p[:, None]) & (u <= t[:, None])
        l_m = (~right)[:, None] & (u > t[:, None]) & (u <= p[:, None])
        mats.append(r_m | l_m)
        same = (t[:, None] // (2 * h)) == (t[None, :] // (2 * h))
        masks.append(same & right[:, None] & (~right)[None, :])
    mats.append(t[None, :] <= t[:, None])
    mats.append(t[None, :] > t[:, None])
    mat = np.concatenate(mats, 0).astype(np.float32)
    return np.concatenate([mat] * HG_SPLIT, 1), np.stack(masks).astype(np.float32)


def _split3(x):
    hi = x.astype(BF16)
    r1 = x - hi.astype(F32)
    mid = r1.astype(BF16)
    lo = (r1 - mid.astype(F32)).astype(BF16)
    return jnp.concatenate([hi, mid, lo], axis=0)


def _hg_kernel(hq_ref, hf_ref, hi_ref, hg_ref, lb_ref, g_ref, mat_ref, mask_ref, o_ref, state_ref, *, ts):
    c = HG_CHUNK

    @pl.when(pl.program_id(1) == 0)
    def _():
        state_ref[...] = jnp.zeros_like(state_ref)

    g = g_ref[...]

    def head_chunk(h, rows):
        cols = slice(h * LANES, (h + 1) * LANES)
        lb = lb_ref[:, cols]
        f = lb + (1.0 - lb) * _sigmoid(hf_ref[rows, cols])
        logf = jnp.log(jnp.maximum(f, F_MIN))
        kin = 1.0 - f
        q = hq_ref[rows, cols]
        v = hi_ref[rows, cols]
        v16 = v.astype(BF16)
        e = jnp.exp(_dot(mat_ref[...], _split3(logf)))
        attn = jnp.zeros((c, c), F32)
        for l in range(HG_LEVELS):
            el = e[l * c:(l + 1) * c]
            attn = attn + _dot_nt((q * el).astype(BF16), (kin * el).astype(BF16)) * mask_ref[l]
        e_b = e[HG_LEVELS * c:(HG_LEVELS + 1) * c]
        e_k = e[(HG_LEVELS + 1) * c:(HG_LEVELS + 2) * c]
        state_t = state_ref[h]
        diag = jnp.sum(q * kin, axis=1, keepdims=True)
        out = (_dot(attn.astype(BF16), v16) + diag * v
               + _dot_nt((q * e_b).astype(BF16), state_t.astype(BF16)))
        kv_t = lax.dot_general(v16, (kin * e_k).astype(BF16), (((0,), (0,)), ((), ())),
                               preferred_element_type=F32)
        state_ref[h] = state_t * e_b[c - 1:c, :] + kv_t
        y = out * lax.rsqrt(jnp.mean(out * out, axis=-1, keepdims=True) + NORM_EPS) * g
        gate = hg_ref[rows, cols]
        o_ref[rows, cols] = (y * (gate * _sigmoid(gate))).astype(o_ref.dtype)

    def chunk(ci, carry):
        rows = pl.ds(pl.multiple_of(ci * c, c), c)
        for h in range(HG_HEADS):
            head_chunk(h, rows)
        return carry

    lax.fori_loop(0, ts // c, chunk, 0)


def _hgrn2(proj, lb, norm_g, batch, seq, col0, ts=512):
    m = proj.shape[0]
    c0 = col0 // HG_W
    spb = seq // ts
    mat, masks = _hg_tables()
    col = lambda off: pl.BlockSpec((ts, HG_W), lambda b, s: (b * spb + s, c0 + off))
    return pl.pallas_call(
        functools.partial(_hg_kernel, ts=ts),
        grid=(batch, spb),
        in_specs=[col(0), col(1), col(2), col(3),
                  pl.BlockSpec((1, HG_W), lambda b, s: (0, 0)),
                  pl.BlockSpec((1, LANES), lambda b, s: (0, 0)),
                  pl.BlockSpec(mat.shape, lambda b, s: (0, 0)),
                  pl.BlockSpec(masks.shape, lambda b, s: (0, 0, 0))],
        out_specs=pl.BlockSpec((ts, HG_W), lambda b, s: (b * spb + s, 0)),
        out_shape=jax.ShapeDtypeStruct((m, HG_W), BF16),
        scratch_shapes=[pltpu.VMEM((HG_HEADS, HG_VAL_DIM, HG_KEY_DIM), F32)],
        compiler_params=_cparams(("arbitrary", "arbitrary")),
        name="hgrn2",
    )(proj, proj, proj, proj, lb.reshape(1, HG_W), norm_g.reshape(1, HG_VAL_DIM),
      jnp.asarray(mat, BF16), jnp.asarray(masks))


def _lb_kernel(x_ref, o_ref):
    x = x_ref[...]
    e = jnp.exp(x - jnp.max(x, axis=0, keepdims=True))
    p = e / jnp.sum(e, axis=0, keepdims=True)
    depth = x.shape[0]
    run = jnp.zeros_like(p[0:1])
    for l in range(depth):
        run = run + p[l:l + 1]
        o_ref[l:l + 1, :] = run - p[0:1]


def _lower_bounds(hg_lower_bounds):
    return pl.pallas_call(
        _lb_kernel,
        out_shape=jax.ShapeDtypeStruct(hg_lower_bounds.shape, F32),
        name="hg_lower_bounds",
    )(hg_lower_bounds.astype(F32))


_COL_Q = 0
_COL_CONV = ATT_Q_W + 2 * ATT_KV_W + IDX_Q_W
_COL_HG = _COL_CONV + 3 * CONV_WIDTH
_COL_GATE = _COL_HG + 4 * HG_W


def _split_w_in(w_in, d_model):
    sizes = (ATT_Q_W, ATT_KV_W, ATT_KV_W, IDX_Q_W, IDX_HEAD_DIM, IDX_HEADS,
             CONV_WIDTH, CONV_WIDTH, CONV_WIDTH, HG_W, HG_W, HG_W, HG_W, N_BRANCHES * d_model)
    offs = np.cumsum((0,) + sizes)
    part = lambda n: w_in[:, offs[n]:offs[n + 1]]
    main = jnp.concatenate([part(0), part(1), part(2), part(3)] + [part(n) for n in range(6, 14)],
                           axis=1).astype(BF16)
    small = jnp.concatenate([part(4), part(5)], axis=1)
    small = jnp.pad(small, ((0, 0), (0, LANES - small.shape[1]))).astype(BF16)
    return main, small


def kernel(x, c, positions, ada_w, ada_b, norm_mix_g, w_in, conv_w, hg_lower_bounds, hg_norm_g,
           w_o_attn, w_o_conv, w_o_hgrn, w_out, norm_mlp_g, w_mlp1, w_mlp2, final_norm_g):
    batch, seq, d = x.shape
    depth = ada_w.shape[0]
    m = batch * seq
    topk = min(INDEX_TOPK, seq // 4)
    lb_all = _lower_bounds(hg_lower_bounds)
    mod = _modulation(c, ada_w, ada_b)
    x2 = x.reshape(m, d)
    for l in range(depth):
        sh1, sc1, g1, sh2, sc2, g2 = [mod[l, :, n * d:(n + 1) * d] for n in range(ADA_CHUNKS)]
        w_main, w_small = _split_w_in(w_in[l], d)
        h = _norm(x2, norm_mix_g[l], seq, sh1, sc1)
        proj = _matmul(h, w_main, out_dtype=F32, tn=768)
        small = _matmul(h, w_small, out_dtype=F32)
        q, k, v, qi, ki, wi = _rope_split(proj, small, positions)
        att = _dsa_attention(q, k, v, qi, ki, wi, batch, seq, topk)
        cv = _short_conv(proj, conv_w[l], seq, _COL_CONV)
        hg = _hgrn2(proj, lb_all[l], hg_norm_g[l], batch, seq, _COL_HG)
        mrg = _matmul(att, w_o_attn[l].astype(BF16), out_dtype=F32, epilogue="gated",
                      gate=proj, gate_off=_COL_GATE)
        mrg = _matmul(cv, w_o_conv[l].astype(BF16), out_dtype=F32, epilogue="gated",
                      gate=proj, gate_off=_COL_GATE + d, prev=mrg)
        mrg = _matmul(hg, w_o_hgrn[l].astype(BF16), out_dtype=BF16, epilogue="gated",
                      gate=proj, gate_off=_COL_GATE + 2 * d, prev=mrg)
        x2 = _matmul(mrg, w_out[l].astype(BF16), out_dtype=F32, epilogue="residual",
                     res=x2, gate_row=g1, seq=seq)
        h = _norm(x2, norm_mlp_g[l], seq, sh2, sc2)
        a = _matmul(h, w_mlp1[l].astype(BF16), out_dtype=BF16, epilogue="relu2")
        x2 = _matmul(a, w_mlp2[l].astype(BF16), out_dtype=F32, epilogue="residual", tk=2048,
                     res=x2, gate_row=g2, seq=seq)
    out = _norm(x2, final_norm_g, seq, out_dtype=x.dtype)
    return out.reshape(batch, seq, d)
```

```python
import functools

import numpy as np
import jax
import jax.numpy as jnp
from jax import lax
from jax.experimental import pallas as pl
from jax.experimental.pallas import tpu as pltpu

ATT_HEADS = 8
ATT_KV_HEADS = 2
ATT_HEAD_DIM = 128
IDX_HEADS = 8
IDX_HEAD_DIM = 64
INDEX_TOPK = 256
ROPE_THETA = 500000.0
ROPE_FRACTION = 4
MASK_VALUE = -1e30
CONV_WIDTH = 1024
CONV_K = 3
HG_HEADS = 8
HG_KEY_DIM = 128
HG_VAL_DIM = 128
F_MIN = 1e-30
N_BRANCHES = 3
NORM_EPS = 1e-6
ADA_CHUNKS = 6

ATT_Q_W = ATT_HEADS * ATT_HEAD_DIM
ATT_KV_W = ATT_KV_HEADS * ATT_HEAD_DIM
IDX_Q_W = IDX_HEADS * IDX_HEAD_DIM
HG_W = HG_HEADS * HG_KEY_DIM

LANES = 128
SUBLANES = 8
VMEM_LIMIT_BYTES = 56 * 1024 * 1024

INT_MIN = -2 ** 31
HALF_BITS = 16
HG_CHUNK = 64
HG_LEVELS = 6
HG_SPLIT = 3

BF16 = jnp.bfloat16
F32 = jnp.float32


def _cparams(sem):
    return pltpu.CompilerParams(dimension_semantics=sem, vmem_limit_bytes=VMEM_LIMIT_BYTES)


def _dot(a, b):
    return jnp.dot(a, b, preferred_element_type=F32)


def _dot_nt(a, b):
    return lax.dot_general(a, b, (((1,), (1,)), ((), ())), preferred_element_type=F32)


def _sigmoid(x):
    return 1.0 / (1.0 + jnp.exp(-x))


def _mod_kernel(c_ref, w_ref, b_ref, o_ref):
    c = c_ref[...]
    ca = (c * _sigmoid(c)).astype(BF16)
    o_ref[...] = _dot(ca, w_ref[...].astype(BF16)) + b_ref[...]


def _modulation(c, ada_w, ada_b):
    depth, d, n = ada_w.shape
    b = c.shape[0]
    bp = -(-b // SUBLANES) * SUBLANES
    cp = jnp.pad(c, ((0, bp - b), (0, 0)))
    tn = 1024
    out = pl.pallas_call(
        _mod_kernel,
        grid=(depth, n // tn),
        in_specs=[
            pl.BlockSpec((bp, d), lambda l, j: (0, 0)),
            pl.BlockSpec((None, d, tn), lambda l, j: (l, 0, j)),
            pl.BlockSpec((None, 1, tn), lambda l, j: (l, 0, j)),
        ],
        out_specs=pl.BlockSpec((None, bp, tn), lambda l, j: (l, 0, j)),
        out_shape=jax.ShapeDtypeStruct((depth, bp, n), F32),
        compiler_params=_cparams(("arbitrary", "arbitrary")),
        name="adaln_mod",
    )(cp, ada_w, ada_b.reshape(depth, 1, n))
    return out[:, :b]


def _norm_kernel(x_ref, g_ref, *rest, modulate):
    if modulate:
        sh_ref, sc_ref, o_ref = rest
    else:
        (o_ref,) = rest
    x = x_ref[...]
    y = x * lax.rsqrt(jnp.mean(x * x, axis=-1, keepdims=True) + NORM_EPS) * g_ref[...]
    if modulate:
        y = y * (1.0 + sc_ref[0]) + sh_ref[0]
    o_ref[...] = y.astype(o_ref.dtype)


def _norm(x2, g, seq, shift=None, scale=None, out_dtype=BF16, tm=512):
    m, d = x2.shape
    modulate = shift is not None
    tpb = seq // tm
    in_specs = [pl.BlockSpec((tm, d), lambda i: (i, 0)),
                pl.BlockSpec((1, d), lambda i: (0, 0))]
    args = [x2, g.reshape(1, d)]
    if modulate:
        row = pl.BlockSpec((1, 1, d), lambda i: (i // tpb, 0, 0))
        in_specs += [row, row]
        args += [shift[:, None, :], scale[:, None, :]]
    return pl.pallas_call(
        functools.partial(_norm_kernel, modulate=modulate),
        grid=(m // tm,),
        in_specs=in_specs,
        out_specs=pl.BlockSpec((tm, d), lambda i: (i, 0)),
        out_shape=jax.ShapeDtypeStruct((m, d), out_dtype),
        compiler_params=_cparams(("arbitrary",)),
        name="rmsnorm",
    )(*args)


def _mm_kernel(*refs, epilogue, nk, has_prev):
    a_ref, w_ref = refs[0], refs[1]
    pos = 2
    extra = []
    n_extra = {"none": 0, "relu2": 0, "residual": 2, "gated": 2 if has_prev else 1}[epilogue]
    for _ in range(n_extra):
        extra.append(refs[pos])
        pos += 1
    o_ref = refs[pos]
    acc_ref = refs[pos + 1] if nk > 1 else None

    def finish(y):
        if epilogue == "relu2":
            y = jnp.square(jnp.maximum(y, 0.0))
        elif epilogue == "residual":
            y = extra[0][...] + extra[1][0] * y
        elif epilogue == "gated":
            y = _sigmoid(extra[0][...]) * y
            if has_prev:
                y = extra[1][...] + y
        o_ref[...] = y.astype(o_ref.dtype)

    if nk == 1:
        finish(_dot(a_ref[...], w_ref[...]))
    else:
        k = pl.program_id(2)

        @pl.when(k == 0)
        def _():
            acc_ref[...] = jnp.zeros_like(acc_ref)

        acc_ref[...] += _dot(a_ref[...], w_ref[...])

        @pl.when(k == nk - 1)
        def _():
            finish(acc_ref[...])


def _matmul(a, w, *, out_dtype, epilogue="none", tm=1024, tn=512, tk=None, a_col=0,
            res=None, gate_row=None, seq=None, gate=None, gate_off=0, prev=None):
    m = a.shape[0]
    kdim, n = w.shape
    tk = kdim if tk is None else tk
    nk = kdim // tk
    tn = min(tn, n)
    a_off = a_col * nk
    in_specs = [pl.BlockSpec((tm, tk), lambda i, j, k: (i, a_off + k)),
                pl.BlockSpec((tk, tn), lambda i, j, k: (k, j))]
    args = [a, w]
    if epilogue == "residual":
        tpb = seq // tm
        in_specs += [pl.BlockSpec((tm, tn), lambda i, j, k: (i, j)),
                     pl.BlockSpec((1, 1, tn), lambda i, j, k: (i // tpb, 0, j))]
        args += [res, gate_row[:, None, :]]
    elif epilogue == "gated":
        assert gate_off % tn == 0
        g_off = gate_off // tn
        in_specs += [pl.BlockSpec((tm, tn), lambda i, j, k: (i, g_off + j))]
        args += [gate]
        if prev is not None:
            in_specs += [pl.BlockSpec((tm, tn), lambda i, j, k: (i, j))]
            args += [prev]
    scratch = [pltpu.VMEM((tm, tn), F32)] if nk > 1 else []
    return pl.pallas_call(
        functools.partial(_mm_kernel, epilogue=epilogue, nk=nk, has_prev=prev is not None),
        grid=(m // tm, n // tn, nk),
        in_specs=in_specs,
        out_specs=pl.BlockSpec((tm, tn), lambda i, j, k: (i, j)),
        out_shape=jax.ShapeDtypeStruct((m, n), out_dtype),
        scratch_shapes=scratch,
        compiler_params=_cparams(("arbitrary", "arbitrary", "arbitrary")),
        name="matmul_" + epilogue,
    )(*args)


def _rope_rows(period, half, rot):
    j = np.arange(LANES) % period
    inv = ROPE_THETA ** (-(np.arange(half, dtype=np.float32)) / np.float32(half))
    inv_row = np.where(j < rot, inv.astype(np.float32)[j % half], 0.0).astype(np.float32)
    sign_row = np.where(j < half, -1.0, np.where(j < rot, 1.0, 0.0)).astype(np.float32)
    first_row = (j < half).astype(np.float32)
    return np.stack([inv_row, sign_row, first_row])[:, None, :]


def _rope_apply(x, cos_t, sin_s, first, half):
    up = pltpu.roll(x, LANES - half, axis=1)
    dn = pltpu.roll(x, half, axis=1)
    sw = jnp.where(first > 0.5, up, dn)
    return x * cos_t + sw * sin_s


def _rope_kernel(pos_ref, rows_a_ref, rows_i_ref, q_ref, kv_ref, qi_ref, sm_ref,
                 qo_ref, ko_ref, vo_ref, qio_ref, kio_ref, wio_ref, *, w_scale):
    pos = pos_ref[...].astype(F32)
    ang_a = pos * rows_a_ref[0]
    cos_a, sin_a = jnp.cos(ang_a), jnp.sin(ang_a) * rows_a_ref[1]
    first_a = rows_a_ref[2]
    ang_i = pos * rows_i_ref[0]
    cos_i, sin_i = jnp.cos(ang_i), jnp.sin(ang_i) * rows_i_ref[1]
    first_i = rows_i_ref[2]
    half_a = ATT_HEAD_DIM // ROPE_FRACTION // 2
    half_i = IDX_HEAD_DIM // ROPE_FRACTION // 2
    for h in range(ATT_HEADS):
        sl = slice(h * LANES, (h + 1) * LANES)
        qo_ref[:, sl] = _rope_apply(q_ref[:, sl], cos_a, sin_a, first_a, half_a).astype(BF16)
    for h in range(ATT_KV_HEADS):
        sl = slice(h * LANES, (h + 1) * LANES)
        ko_ref[:, sl] = _rope_apply(kv_ref[:, sl], cos_a, sin_a, first_a, half_a).astype(BF16)
    vo_ref[...] = kv_ref[:, ATT_KV_W:].T.astype(BF16)
    for h in range(IDX_Q_W // LANES):
        sl = slice(h * LANES, (h + 1) * LANES)
        qio_ref[:, sl] = _rope_apply(qi_ref[:, sl], cos_i, sin_i, first_i, half_i).astype(BF16)
    sm = sm_ref[...]
    ki = _rope_apply(sm, cos_i, sin_i, first_i, half_i)
    kio_ref[...] = ki[:, :IDX_HEAD_DIM].astype(BF16)
    wio_ref[...] = sm.T[IDX_HEAD_DIM:IDX_HEAD_DIM + IDX_HEADS, :] * w_scale


def _rope_split(proj, small, positions, tm=512):
    m = proj.shape[0]
    rows_a = jnp.asarray(_rope_rows(ATT_HEAD_DIM, ATT_HEAD_DIM // ROPE_FRACTION // 2,
                                    ATT_HEAD_DIM // ROPE_FRACTION))
    rows_i = jnp.asarray(_rope_rows(IDX_HEAD_DIM, IDX_HEAD_DIM // ROPE_FRACTION // 2,
                                    IDX_HEAD_DIM // ROPE_FRACTION))
    w_scale = (IDX_HEADS ** -0.5) * (IDX_HEAD_DIM ** -0.5)
    rows_spec = pl.BlockSpec((3, 1, LANES), lambda i: (0, 0, 0))
    outs = pl.pallas_call(
        functools.partial(_rope_kernel, w_scale=w_scale),
        grid=(m // tm,),
        in_specs=[
            pl.BlockSpec((tm, 1), lambda i: (i, 0)),
            rows_spec, rows_spec,
            pl.BlockSpec((tm, ATT_Q_W), lambda i: (i, 0)),
            pl.BlockSpec((tm, 2 * ATT_KV_W), lambda i: (i, ATT_Q_W // (2 * ATT_KV_W))),
            pl.BlockSpec((tm, IDX_Q_W), lambda i: (i, (ATT_Q_W + 2 * ATT_KV_W) // IDX_Q_W)),
            pl.BlockSpec((tm, LANES), lambda i: (i, 0)),
        ],
        out_specs=[
            pl.BlockSpec((tm, ATT_Q_W), lambda i: (i, 0)),
            pl.BlockSpec((tm, ATT_KV_W), lambda i: (i, 0)),
            pl.BlockSpec((ATT_KV_W, tm), lambda i: (0, i)),
            pl.BlockSpec((tm, IDX_Q_W), lambda i: (i, 0)),
            pl.BlockSpec((tm, IDX_HEAD_DIM), lambda i: (i, 0)),
            pl.BlockSpec((IDX_HEADS, tm), lambda i: (0, i)),
        ],
        out_shape=[
            jax.ShapeDtypeStruct((m, ATT_Q_W), BF16),
            jax.ShapeDtypeStruct((m, ATT_KV_W), BF16),
            jax.ShapeDtypeStruct((ATT_KV_W, m), BF16),
            jax.ShapeDtypeStruct((m, IDX_Q_W), BF16),
            jax.ShapeDtypeStruct((m, IDX_HEAD_DIM), BF16),
            jax.ShapeDtypeStruct((IDX_HEADS, m), F32),
        ],
        compiler_params=_cparams(("arbitrary",)),
        name="rope_split",
    )(positions.reshape(m, 1), rows_a, rows_i, proj, proj, proj, small)
    return outs


def _attn_kernel(qi_ref, wi_ref, ki_ref, q_ref, k_ref, vt_ref, o_ref, skey_ref, hi_ref, lo_ref,
                 da_ref, db_ref, sa_ref, sb_ref, *, tq, kc, topk, seq):
    i = pl.program_id(1)
    nchunk = lax.shift_right_logical((i + 1) * tq + (kc - 1), int(np.log2(kc)))
    qpos = i * tq + lax.broadcasted_iota(jnp.int32, (1, tq), 1)
    i16 = jnp.int16
    half_bias = 1 << (HALF_BITS - 1)
    pk = 2 * SUBLANES
    rb = 4 * pk

    qi = qi_ref[0]
    wi = wi_ref[...]
    qi_s = jnp.concatenate([qi[:, h * IDX_HEAD_DIM:(h + 1) * IDX_HEAD_DIM] for h in range(IDX_HEADS)],
                           axis=0)

    kh = kc // 2
    last_half = seq // kh - 1

    def score_dots(half, dst_ref):
        k0 = pl.multiple_of(jnp.minimum(half, last_half) * kh, kh)
        dst_ref[...] = _dot_nt(ki_ref[0, pl.ds(k0, kh), :], qi_s)

    def score_keys(half, src_ref):
        k0 = pl.multiple_of(half * kh, kh)
        for j in range(kh // rb):
            rows = slice(j * rb, (j + 1) * rb)
            acc = jnp.zeros((rb, tq), F32)
            for h in range(IDX_HEADS):
                acc = acc + jnp.maximum(src_ref[rows, h * tq:(h + 1) * tq], 0.0) * wi[h:h + 1, :]
            kpos = k0 + j * rb + lax.broadcasted_iota(jnp.int32, (rb, tq), 0)
            bits = pltpu.bitcast(acc, jnp.int32)
            skey = jnp.where(bits < 0, bits ^ jnp.int32(0x7FFFFFFF), bits)
            skey = jnp.where(kpos <= qpos, skey, jnp.int32(INT_MIN))
            dst = pl.ds(k0 + j * rb, rb)
            skey_ref[dst, :] = skey
            hi_ref[dst, :] = lax.shift_right_arithmetic(skey, HALF_BITS).astype(i16)
            lo_ref[dst, :] = ((skey & jnp.int32(2 * half_bias - 1)) - half_bias).astype(i16)

    def score_chunk(c, carry):
        score_dots(2 * c + 1, db_ref)
        score_keys(2 * c, da_ref)
        score_dots(2 * c + 2, da_ref)
        score_keys(2 * c + 1, db_ref)
        return carry

    score_dots(0, da_ref)
    lax.fori_loop(0, nchunk, score_chunk, 0)

    nacc = 4

    def count_where(pred_fn):
        row_iota = lax.broadcasted_iota(jnp.int32, (SUBLANES, tq), 0)

        def chunk_body(c, parts):
            k0 = pl.multiple_of(c * kc, kc)
            parts = list(parts)
            xs = skey_ref[pl.ds(k0, kc), :]
            for j in range(kc // SUBLANES):
                x = xs[j * SUBLANES:(j + 1) * SUBLANES]
                parts[j % nacc] = parts[j % nacc] + jnp.where(pred_fn(x, k0 + j * SUBLANES + row_iota), 1, 0)
            return tuple(parts)
        zero = jnp.zeros((SUBLANES, tq), jnp.int32)
        parts = lax.fori_loop(0, nchunk, chunk_body, (zero,) * nacc)
        return jnp.sum(sum(parts[1:], parts[0]), axis=0, keepdims=True)

    def count16(ref, pred_fn):
        def chunk_body(c, parts):
            k0 = pl.multiple_of(c * kc, kc)
            parts = list(parts)
            xs = ref[pl.ds(k0, kc), :]
            for j in range(kc // pk):
                parts[j % nacc] = parts[j % nacc] + jnp.where(pred_fn(xs[j * pk:(j + 1) * pk]), i16(1), i16(0))
            return tuple(parts)
        zero = jnp.zeros((pk, tq), i16)
        parts = lax.fori_loop(0, nchunk, chunk_body, (zero,) * nacc)
        return jnp.sum(sum(parts[1:], parts[0]).astype(jnp.int32), axis=0, keepdims=True)

    def search16(ref, want):
        def bit_body(b, prefix):
            cand = prefix | lax.shift_left(jnp.int32(1), HALF_BITS - 1 - b)
            cand16 = jnp.broadcast_to(cand - half_bias, (pk, tq)).astype(i16)
            cnt = count16(ref, lambda x: x >= cand16)
            return jnp.where(cnt >= want, cand, prefix)
        return lax.fori_loop(0, HALF_BITS, bit_body, jnp.zeros((1, tq), jnp.int32))

    p_hi = search16(hi_ref, topk)
    t_hi = jnp.broadcast_to(p_hi - half_bias, (pk, tq)).astype(i16)
    n_hi_gt = count16(hi_ref, lambda x: x > t_hi)

    def bucket_chunk(c, carry):
        rows = pl.ds(pl.multiple_of(c * kc, kc), kc)
        t_hi_c = jnp.concatenate([t_hi] * (kc // pk), axis=0)
        lo_ref[rows, :] = jnp.where(hi_ref[rows, :] == t_hi_c, lo_ref[rows, :], i16(-half_bias))
        return carry

    lax.fori_loop(0, nchunk, bucket_chunk, 0)
    p_lo = search16(lo_ref, topk - n_hi_gt)
    prefix = lax.shift_left(p_hi, HALF_BITS) | p_lo
    thr = prefix ^ jnp.int32(INT_MIN)

    n_gt = count_where(lambda x, _: x > thr)
    n_eq = count_where(lambda x, _: x == thr)
    need = topk - n_gt
    excess = jnp.where((prefix != 0) & (n_eq > need), 1, 0)
    any_excess = jnp.max(excess) > 0
    idx_bits = int(np.log2(seq))

    def tie_search():
        def tie_bit(b, x):
            cand = x | lax.shift_left(jnp.int32(1), idx_bits - 1 - b)
            below = count_where(lambda xk, kidx: (xk == thr) & (kidx < cand))
            return jnp.where(below < need, cand, x)
        return lax.fori_loop(0, idx_bits, tie_bit, jnp.zeros((1, tq), jnp.int32))

    tie_last = lax.cond(any_excess, tie_search, lambda: jnp.full((1, tq), seq, jnp.int32))

    groups = ATT_KV_HEADS
    rep = ATT_HEADS // ATT_KV_HEADS
    scale2 = (ATT_HEAD_DIM ** -0.5) * float(np.log2(np.e))
    q = q_ref[0]
    q_g = [jnp.concatenate([q[:, (g * rep + r) * LANES:(g * rep + r + 1) * LANES] for r in range(rep)], axis=0)
           for g in range(groups)]

    def qk_dots(half, dst_ref):
        k0 = pl.multiple_of(jnp.minimum(half, last_half) * kh, kh)
        for g in range(groups):
            dst_ref[g] = _dot_nt(k_ref[0, pl.ds(k0, kh), g * LANES:(g + 1) * LANES], q_g[g])

    def softmax_pv(half, src_ref, carry):
        k0 = pl.multiple_of(half * kh, kh)
        x = skey_ref[pl.ds(k0, kh), :]
        kidx = k0 + lax.broadcasted_iota(jnp.int32, (kh, tq), 0)
        sel = (x > thr) | ((x == thr) & (kidx <= tie_last))
        sel = sel & (x != jnp.int32(INT_MIN))
        bias = jnp.where(sel, 0.0, MASK_VALUE)
        bias = jnp.concatenate([bias] * rep, axis=1)
        new = []
        for g in range(groups):
            m_old, l_old, acc_old = carry[g]
            vtg = vt_ref[g * LANES:(g + 1) * LANES, pl.ds(k0, kh)]
            sm = src_ref[g] * scale2 + bias
            m_new = jnp.maximum(m_old, jnp.max(sm, axis=0, keepdims=True))
            p = jnp.exp2(sm - m_new)
            alpha = jnp.exp2(m_old - m_new)
            l_new = alpha * l_old + jnp.sum(p, axis=0, keepdims=True)
            acc_new = alpha * acc_old + _dot(vtg, p.astype(BF16))
            new.append((m_new, l_new, acc_new))
        return tuple(new)

    def attn_chunk(c, carry):
        qk_dots(2 * c + 1, sb_ref)
        carry = softmax_pv(2 * c, sa_ref, carry)
        qk_dots(2 * c + 2, sa_ref)
        return softmax_pv(2 * c + 1, sb_ref, carry)

    init = tuple((jnp.full((1, rep * tq), MASK_VALUE, F32), jnp.zeros((1, rep * tq), F32),
                  jnp.zeros((LANES, rep * tq), F32)) for _ in range(groups))
    qk_dots(0, sa_ref)
    fin = lax.fori_loop(0, nchunk, attn_chunk, init)
    for g in range(groups):
        _, l_f, acc_f = fin[g]
        o_t = acc_f / l_f
        for r in range(rep):
            h = g * rep + r
            o_ref[0, :, h * LANES:(h + 1) * LANES] = o_t[:, r * tq:(r + 1) * tq].T.astype(o_ref.dtype)


def _dsa_attention(q, k, vt, qi, ki, wit, batch, seq, topk, tq=128, kc=512):
    nblk = seq // tq
    rep = ATT_HEADS // ATT_KV_HEADS
    r3 = lambda a: a.reshape(batch, seq, a.shape[-1])
    qblk = lambda w: pl.BlockSpec((1, tq, w), lambda b, i: (b, i, 0))
    full = lambda w: pl.BlockSpec((1, seq, w), lambda b, i: (b, 0, 0))
    out = pl.pallas_call(
        functools.partial(_attn_kernel, tq=tq, kc=kc, topk=topk, seq=seq),
        grid=(batch, nblk),
        in_specs=[qblk(IDX_Q_W), pl.BlockSpec((IDX_HEADS, tq), lambda b, i: (0, b * nblk + i)),
                  full(IDX_HEAD_DIM), qblk(ATT_Q_W), full(ATT_KV_W),
                  pl.BlockSpec((ATT_KV_W, seq), lambda b, i: (0, b))],
        out_specs=qblk(ATT_Q_W),
        out_shape=jax.ShapeDtypeStruct((batch, seq, ATT_Q_W), BF16),
        scratch_shapes=[pltpu.VMEM((seq, tq), jnp.int32), pltpu.VMEM((seq, tq), jnp.int16),
                        pltpu.VMEM((seq, tq), jnp.int16),
                        pltpu.VMEM((kc // 2, IDX_HEADS * tq), F32), pltpu.VMEM((kc // 2, IDX_HEADS * tq), F32),
                        pltpu.VMEM((ATT_KV_HEADS, kc // 2, rep * tq), F32),
                        pltpu.VMEM((ATT_KV_HEADS, kc // 2, rep * tq), F32)],
        compiler_params=_cparams(("arbitrary", "arbitrary")),
        name="dsa_attention",
    )(r3(qi), wit, r3(ki), r3(q), r3(k), vt)
    return out.reshape(batch * seq, ATT_Q_W)


def _conv_kernel(cb_ref, cc_ref, cx_ref, hc_ref, hx_ref, w_ref, o_ref, *, tm, seq):
    i = pl.program_id(0)
    u = cc_ref[...] * cx_ref[...]
    halo = hc_ref[...] * hx_ref[...]
    halo = jnp.where((i * tm) % seq == 0, 0.0, halo)
    row = lax.broadcasted_iota(jnp.int32, u.shape, 0)
    u1 = pltpu.roll(u, 1, axis=0)
    u2 = pltpu.roll(u, 2, axis=0)
    h1 = halo[SUBLANES - 1:SUBLANES, :]
    h2 = halo[SUBLANES - 2:SUBLANES - 1, :]
    u1 = jnp.where(row == 0, h1, u1)
    u2 = jnp.where(row == 0, h2, jnp.where(row == 1, h1, u2))
    w = w_ref[...]
    conv = u2 * w[0:1, :] + u1 * w[1:2, :] + u * w[2:3, :]
    o_ref[...] = (cb_ref[...] * conv).astype(o_ref.dtype)


def _short_conv(proj, conv_w, seq, col0, tm=512, tc=512):
    m = proj.shape[0]
    nb = CONV_WIDTH // tc
    c0 = col0 // tc
    rpb = tm // SUBLANES
    main = lambda off: pl.BlockSpec((tm, tc), lambda i, j: (i, c0 + off * nb + j))
    halo = lambda off: pl.BlockSpec((SUBLANES, tc),
                                    lambda i, j: (jnp.maximum(i * rpb - 1, 0), c0 + off * nb + j))
    return pl.pallas_call(
        functools.partial(_conv_kernel, tm=tm, seq=seq),
        grid=(m // tm, nb),
        in_specs=[main(0), main(1), main(2), halo(1), halo(2),
                  pl.BlockSpec((CONV_K, tc), lambda i, j: (0, j))],
        out_specs=pl.BlockSpec((tm, tc), lambda i, j: (i, j)),
        out_shape=jax.ShapeDtypeStruct((m, CONV_WIDTH), BF16),
        compiler_params=_cparams(("arbitrary", "arbitrary")),
        name="short_conv",
    )(proj, proj, proj, proj, proj, conv_w)


def _hg_tables():
    c = HG_CHUNK
    t = np.arange(c)
    mats, masks = [], []
    for l in range(HG_LEVELS):
        h = 1 << l
        start = (t // (2 * h)) * (2 * h)
        p = start + h - 1
        right = (t - start) >= h
        u = t[None, :]
        r_m = right[:, None] & (u > p[:, None]) & (u <= t[:, None])
        l_m = (~right)[:, None] & (u > t[:, None]) & (u <= p[:, None])
        mats.append(r_m | l_m)
        same = (t[:, None] // (2 * h)) == (t[None, :] // (2 * h))
        masks.append(same & right[:, None] & (~right)[None, :])
    mats.append(t[None, :] <= t[:, None])
    mats.append(t[None, :] > t[:, None])
    mat = np.concatenate(mats, 0).astype(np.float32)
    return np.concatenate([mat] * HG_SPLIT, 1), np.stack(masks).astype(np.float32)


def _split3(x):
    hi = x.astype(BF16)
    r1 = x - hi.astype(F32)
    mid = r1.astype(BF16)
    lo = (r1 - mid.astype(F32)).astype(BF16)
    return jnp.concatenate([hi, mid, lo], axis=0)


def _hg_kernel(hq_ref, hf_ref, hi_ref, hg_ref, lb_ref, g_ref, mat_ref, mask_ref, o_ref, state_ref, *, ts):
    c = HG_CHUNK

    @pl.when(pl.program_id(1) == 0)
    def _():
        state_ref[...] = jnp.zeros_like(state_ref)

    g = g_ref[...]

    def head_chunk(h, rows):
        cols = slice(h * LANES, (h + 1) * LANES)
        lb = lb_ref[:, cols]
        f = lb + (1.0 - lb) * _sigmoid(hf_ref[rows, cols])
        logf = jnp.log(jnp.maximum(f, F_MIN))
        kin = 1.0 - f
        q = hq_ref[rows, cols]
        v = hi_ref[rows, cols]
        v16 = v.astype(BF16)
        e = jnp.exp(_dot(mat_ref[...], _split3(logf)))
        attn = jnp.zeros((c, c), F32)
        for l in range(HG_LEVELS):
            el = e[l * c:(l + 1) * c]
            attn = attn + _dot_nt((q * el).astype(BF16), (kin * el).astype(BF16)) * mask_ref[l]
        e_b = e[HG_LEVELS * c:(HG_LEVELS + 1) * c]
        e_k = e[(HG_LEVELS + 1) * c:(HG_LEVELS + 2) * c]
        state_t = state_ref[h]
        diag = jnp.sum(q * kin, axis=1, keepdims=True)
        out = (_dot(attn.astype(BF16), v16) + diag * v
               + _dot_nt((q * e_b).astype(BF16), state_t.astype(BF16)))
        kv_t = lax.dot_general(v16, (kin * e_k).astype(BF16), (((0,), (0,)), ((), ())),
                               preferred_element_type=F32)
        state_ref[h] = state_t * e_b[c - 1:c, :] + kv_t
        y = out * lax.rsqrt(jnp.mean(out * out, axis=-1, keepdims=True) + NORM_EPS) * g
        gate = hg_ref[rows, cols]
        o_ref[rows, cols] = (y * (gate * _sigmoid(gate))).astype(o_ref.dtype)

    def chunk(ci, carry):
        rows = pl.ds(pl.multiple_of(ci * c, c), c)
        for h in range(HG_HEADS):
            head_chunk(h, rows)
        return carry

    lax.fori_loop(0, ts // c, chunk, 0)


def _hgrn2(proj, lb, norm_g, batch, seq, col0, ts=512):
    m = proj.shape[0]
    c0 = col0 // HG_W
    spb = seq // ts
    mat, masks = _hg_tables()
    col = lambda off: pl.BlockSpec((ts, HG_W), lambda b, s: (b * spb + s, c0 + off))
    return pl.pallas_call(
        functools.partial(_hg_kernel, ts=ts),
        grid=(batch, spb),
        in_specs=[col(0), col(1), col(2), col(3),
                  pl.BlockSpec((1, HG_W), lambda b, s: (0, 0)),
                  pl.BlockSpec((1, LANES), lambda b, s: (0, 0)),
                  pl.BlockSpec(mat.shape, lambda b, s: (0, 0)),
                  pl.BlockSpec(masks.shape, lambda b, s: (0, 0, 0))],
        out_specs=pl.BlockSpec((ts, HG_W), lambda b, s: (b * spb + s, 0)),
        out_shape=jax.ShapeDtypeStruct((m, HG_W), BF16),
        scratch_shapes=[pltpu.VMEM((HG_HEADS, HG_VAL_DIM, HG_KEY_DIM), F32)],
        compiler_params=_cparams(("arbitrary", "arbitrary")),
        name="hgrn2",
    )(proj, proj, proj, proj, lb.reshape(1, HG_W), norm_g.reshape(1, HG_VAL_DIM),
      jnp.asarray(mat, BF16), jnp.asarray(masks))


def _lb_kernel(x_ref, o_ref):
    x = x_ref[...]
    e = jnp.exp(x - jnp.max(x, axis=0, keepdims=True))
    p = e / jnp.sum(e, axis=0, keepdims=True)
    depth = x.shape[0]
    run = jnp.zeros_like(p[0:1])
    for l in range(depth):
        run = run + p[l:l + 1]
        o_ref[l:l + 1, :] = run - p[0:1]


def _lower_bounds(hg_lower_bounds):
    return pl.pallas_call(
        _lb_kernel,
        out_shape=jax.ShapeDtypeStruct(hg_lower_bounds.shape, F32),
        name="hg_lower_bounds",
    )(hg_lower_bounds.astype(F32))


_COL_Q = 0
_COL_CONV = ATT_Q_W + 2 * ATT_KV_W + IDX_Q_W
_COL_HG = _COL_CONV + 3 * CONV_WIDTH
_COL_GATE = _COL_HG + 4 * HG_W


def _split_w_in(w_in, d_model):
    sizes = (ATT_Q_W, ATT_KV_W, ATT_KV_W, IDX_Q_W, IDX_HEAD_DIM, IDX_HEADS,
             CONV_WIDTH, CONV_WIDTH, CONV_WIDTH, HG_W, HG_W, HG_W, HG_W, N_BRANCHES * d_model)
    offs = np.cumsum((0,) + sizes)
    part = lambda n: w_in[:, offs[n]:offs[n + 1]]
    main = jnp.concatenate([part(0), part(1), part(2), part(3)] + [part(n) for n in range(6, 14)],
                           axis=1).astype(BF16)
    small = jnp.concatenate([part(4), part(5)], axis=1)
    small = jnp.pad(small, ((0, 0), (0, LANES - small.shape[1]))).astype(BF16)
    return main, small


def kernel(x, c, positions, ada_w, ada_b, norm_mix_g, w_in, conv_w, hg_lower_bounds, hg_norm_g,
           w_o_attn, w_o_conv, w_o_hgrn, w_out, norm_mlp_g, w_mlp1, w_mlp2, final_norm_g):
    batch, seq, d = x.shape
    depth = ada_w.shape[0]
    m = batch * seq
    topk = min(INDEX_TOPK, seq // 4)
    lb_all = _lower_bounds(hg_lower_bounds)
    mod = _modulation(c, ada_w, ada_b)
    x2 = x.reshape(m, d)
    for l in range(depth):
        sh1, sc1, g1, sh2, sc2, g2 = [mod[l, :, n * d:(n + 1) * d] for n in range(ADA_CHUNKS)]
        w_main, w_small = _split_w_in(w_in[l], d)
        h = _norm(x2, norm_mix_g[l], seq, sh1, sc1)
        proj = _matmul(h, w_main, out_dtype=F32, tn=768)
        small = _matmul(h, w_small, out_dtype=F32)
        q, k, v, qi, ki, wi = _rope_split(proj, small, positions)
        att = _dsa_attention(q, k, v, qi, ki, wi, batch, seq, topk)
        cv = _short_conv(proj, conv_w[l], seq, _COL_CONV)
        hg = _hgrn2(proj, lb_all[l], hg_norm_g[l], batch, seq, _COL_HG)
        mrg = _matmul(att, w_o_attn[l].astype(BF16), out_dtype=F32, epilogue="gated",
                      gate=proj, gate_off=_COL_GATE)
        mrg = _matmul(cv, w_o_conv[l].astype(BF16), out_dtype=F32, epilogue="gated",
                      gate=proj, gate_off=_COL_GATE + d, prev=mrg)
        mrg = _matmul(hg, w_o_hgrn[l].astype(BF16), out_dtype=BF16, epilogue="gated",
                      gate=proj, gate_off=_COL_GATE + 2 * d, prev=mrg)
        x2 = _matmul(mrg, w_out[l].astype(BF16), out_dtype=F32, epilogue="residual",
                     res=x2, gate_row=g1, seq=seq)
        h = _norm(x2, norm_mlp_g[l], seq, sh2, sc2)
        a = _matmul(h, w_mlp1[l].astype(BF16), out_dtype=BF16, epilogue="relu2")
        x2 = _matmul(a, w_mlp2[l].astype(BF16), out_dtype=F32, epilogue="residual", tk=2048,
                     res=x2, gate_row=g2, seq=seq)
    out = _norm(x2, final_norm_g, seq, out_dtype=x.dtype)
    return out.reshape(batch, seq, d)
```

```python
import functools

import numpy as np
import jax
import jax.numpy as jnp
from jax import lax
from jax.experimental import pallas as pl
from jax.experimental.pallas import tpu as pltpu

ATT_HEADS = 8
ATT_KV_HEADS = 2
ATT_HEAD_DIM = 128
IDX_HEADS = 8
IDX_HEAD_DIM = 64
INDEX_TOPK = 256
ROPE_THETA = 500000.0
ROPE_FRACTION = 4
MASK_VALUE = -1e30
CONV_WIDTH = 1024
CONV_K = 3
HG_HEADS = 8
HG_KEY_DIM = 128
HG_VAL_DIM = 128
F_MIN = 1e-30
N_BRANCHES = 3
NORM_EPS = 1e-6
ADA_CHUNKS = 6

ATT_Q_W = ATT_HEADS * ATT_HEAD_DIM
ATT_KV_W = ATT_KV_HEADS * ATT_HEAD_DIM
IDX_Q_W = IDX_HEADS * IDX_HEAD_DIM
HG_W = HG_HEADS * HG_KEY_DIM

LANES = 128
SUBLANES = 8
VMEM_LIMIT_BYTES = 56 * 1024 * 1024

INT_MIN = -2 ** 31
HALF_BITS = 16
HG_CHUNK = 64
HG_LEVELS = 6
HG_SPLIT = 3

BF16 = jnp.bfloat16
F32 = jnp.float32


def _cparams(sem):
    return pltpu.CompilerParams(dimension_semantics=sem, vmem_limit_bytes=VMEM_LIMIT_BYTES)


def _dot(a, b):
    return jnp.dot(a, b, preferred_element_type=F32)


def _dot_nt(a, b):
    return lax.dot_general(a, b, (((1,), (1,)), ((), ())), preferred_element_type=F32)


def _sigmoid(x):
    return 1.0 / (1.0 + jnp.exp(-x))


def _mod_kernel(c_ref, w_ref, b_ref, o_ref):
    c = c_ref[...]
    ca = (c * _sigmoid(c)).astype(BF16)
    o_ref[...] = _dot(ca, w_ref[...].astype(BF16)) + b_ref[...]


def _modulation(c, ada_w, ada_b):
    depth, d, n = ada_w.shape
    b = c.shape[0]
    bp = -(-b // SUBLANES) * SUBLANES
    cp = jnp.pad(c, ((0, bp - b), (0, 0)))
    tn = 1024
    out = pl.pallas_call(
        _mod_kernel,
        grid=(depth, n // tn),
        in_specs=[
            pl.BlockSpec((bp, d), lambda l, j: (0, 0)),
            pl.BlockSpec((None, d, tn), lambda l, j: (l, 0, j)),
            pl.BlockSpec((None, 1, tn), lambda l, j: (l, 0, j)),
        ],
        out_specs=pl.BlockSpec((None, bp, tn), lambda l, j: (l, 0, j)),
        out_shape=jax.ShapeDtypeStruct((depth, bp, n), F32),
        compiler_params=_cparams(("arbitrary", "arbitrary")),
        name="adaln_mod",
    )(cp, ada_w, ada_b.reshape(depth, 1, n))
    return out[:, :b]


def _norm_kernel(x_ref, g_ref, *rest, modulate):
    if modulate:
        sh_ref, sc_ref, o_ref = rest
    else:
        (o_ref,) = rest
    x = x_ref[...]
    y = x * lax.rsqrt(jnp.mean(x * x, axis=-1, keepdims=True) + NORM_EPS) * g_ref[...]
    if modulate:
        y = y * (1.0 + sc_ref[0]) + sh_ref[0]
    o_ref[...] = y.astype(o_ref.dtype)


def _norm(x2, g, seq, shift=None, scale=None, out_dtype=BF16, tm=512):
    m, d = x2.shape
    modulate = shift is not None
    tpb = seq // tm
    in_specs = [pl.BlockSpec((tm, d), lambda i: (i, 0)),
                pl.BlockSpec((1, d), lambda i: (0, 0))]
    args = [x2, g.reshape(1, d)]
    if modulate:
        row = pl.BlockSpec((1, 1, d), lambda i: (i // tpb, 0, 0))
        in_specs += [row, row]
        args += [shift[:, None, :], scale[:, None, :]]
    return pl.pallas_call(
        functools.partial(_norm_kernel, modulate=modulate),
        grid=(m // tm,),
        in_specs=in_specs,
        out_specs=pl.BlockSpec((tm, d), lambda i: (i, 0)),
        out_shape=jax.ShapeDtypeStruct((m, d), out_dtype),
        compiler_params=_cparams(("arbitrary",)),
        name="rmsnorm",
    )(*args)


def _mm_kernel(*refs, epilogue, nk, has_prev):
    a_ref, w_ref = refs[0], refs[1]
    pos = 2
    extra = []
    n_extra = {"none": 0, "relu2": 0, "residual": 2, "gated": 2 if has_prev else 1}[epilogue]
    for _ in range(n_extra):
        extra.append(refs[pos])
        pos += 1
    o_ref = refs[pos]
    acc_ref = refs[pos + 1] if nk > 1 else None

    def finish(y):
        if epilogue == "relu2":
            y = jnp.square(jnp.maximum(y, 0.0))
        elif epilogue == "residual":
            y = extra[0][...] + extra[1][0] * y
        elif epilogue == "gated":
            y = _sigmoid(extra[0][...]) * y
            if has_prev:
                y = extra[1][...] + y
        o_ref[...] = y.astype(o_ref.dtype)

    if nk == 1:
        finish(_dot(a_ref[...], w_ref[...]))
    else:
        k = pl.program_id(2)

        @pl.when(k == 0)
        def _():
            acc_ref[...] = jnp.zeros_like(acc_ref)

        acc_ref[...] += _dot(a_ref[...], w_ref[...])

        @pl.when(k == nk - 1)
        def _():
            finish(acc_ref[...])


def _matmul(a, w, *, out_dtype, epilogue="none", tm=1024, tn=512, tk=None, a_col=0,
            res=None, gate_row=None, seq=None, gate=None, gate_off=0, prev=None):
    m = a.shape[0]
    kdim, n = w.shape
    tk = kdim if tk is None else tk
    nk = kdim // tk
    tn = min(tn, n)
    a_off = a_col * nk
    in_specs = [pl.BlockSpec((tm, tk), lambda i, j, k: (i, a_off + k)),
                pl.BlockSpec((tk, tn), lambda i, j, k: (k, j))]
    args = [a, w]
    if epilogue == "residual":
        tpb = seq // tm
        in_specs += [pl.BlockSpec((tm, tn), lambda i, j, k: (i, j)),
                     pl.BlockSpec((1, 1, tn), lambda i, j, k: (i // tpb, 0, j))]
        args += [res, gate_row[:, None, :]]
    elif epilogue == "gated":
        assert gate_off % tn == 0
        g_off = gate_off // tn
        in_specs += [pl.BlockSpec((tm, tn), lambda i, j, k: (i, g_off + j))]
        args += [gate]
        if prev is not None:
            in_specs += [pl.BlockSpec((tm, tn), lambda i, j, k: (i, j))]
            args += [prev]
    scratch = [pltpu.VMEM((tm, tn), F32)] if nk > 1 else []
    return pl.pallas_call(
        functools.partial(_mm_kernel, epilogue=epilogue, nk=nk, has_prev=prev is not None),
        grid=(m // tm, n // tn, nk),
        in_specs=in_specs,
        out_specs=pl.BlockSpec((tm, tn), lambda i, j, k: (i, j)),
        out_shape=jax.ShapeDtypeStruct((m, n), out_dtype),
        scratch_shapes=scratch,
        compiler_params=_cparams(("arbitrary", "arbitrary", "arbitrary")),
        name="matmul_" + epilogue,
    )(*args)


def _rope_rows(period, half, rot):
    j = np.arange(LANES) % period
    inv = ROPE_THETA ** (-(np.arange(half, dtype=np.float32)) / np.float32(half))
    inv_row = np.where(j < rot, inv.astype(np.float32)[j % half], 0.0).astype(np.float32)
    sign_row = np.where(j < half, -1.0, np.where(j < rot, 1.0, 0.0)).astype(np.float32)
    first_row = (j < half).astype(np.float32)
    return np.stack([inv_row, sign_row, first_row])[:, None, :]


def _rope_apply(x, cos_t, sin_s, first, half):
    up = pltpu.roll(x, LANES - half, axis=1)
    dn = pltpu.roll(x, half, axis=1)
    sw = jnp.where(first > 0.5, up, dn)
    return x * cos_t + sw * sin_s


def _rope_kernel(pos_ref, rows_a_ref, rows_i_ref, q_ref, kv_ref, qi_ref, sm_ref,
                 qo_ref, ko_ref, vo_ref, qio_ref, kio_ref, wio_ref, *, w_scale):
    pos = pos_ref[...].astype(F32)
    ang_a = pos * rows_a_ref[0]
    cos_a, sin_a = jnp.cos(ang_a), jnp.sin(ang_a) * rows_a_ref[1]
    first_a = rows_a_ref[2]
    ang_i = pos * rows_i_ref[0]
    cos_i, sin_i = jnp.cos(ang_i), jnp.sin(ang_i) * rows_i_ref[1]
    first_i = rows_i_ref[2]
    half_a = ATT_HEAD_DIM // ROPE_FRACTION // 2
    half_i = IDX_HEAD_DIM // ROPE_FRACTION // 2
    for h in range(ATT_HEADS):
        sl = slice(h * LANES, (h + 1) * LANES)
        qo_ref[:, sl] = _rope_apply(q_ref[:, sl], cos_a, sin_a, first_a, half_a).astype(BF16)
    for h in range(ATT_KV_HEADS):
        sl = slice(h * LANES, (h + 1) * LANES)
        ko_ref[:, sl] = _rope_apply(kv_ref[:, sl], cos_a, sin_a, first_a, half_a).astype(BF16)
    vo_ref[...] = kv_ref[:, ATT_KV_W:].T.astype(BF16)
    for h in range(IDX_Q_W // LANES):
        sl = slice(h * LANES, (h + 1) * LANES)
        qio_ref[:, sl] = _rope_apply(qi_ref[:, sl], cos_i, sin_i, first_i, half_i).astype(BF16)
    sm = sm_ref[...]
    ki = _rope_apply(sm, cos_i, sin_i, first_i, half_i)
    kio_ref[...] = ki[:, :IDX_HEAD_DIM].astype(BF16)
    wio_ref[...] = sm.T[IDX_HEAD_DIM:IDX_HEAD_DIM + IDX_HEADS, :] * w_scale


def _rope_split(proj, small, positions, tm=512):
    m = proj.shape[0]
    rows_a = jnp.asarray(_rope_rows(ATT_HEAD_DIM, ATT_HEAD_DIM // ROPE_FRACTION // 2,
                                    ATT_HEAD_DIM // ROPE_FRACTION))
    rows_i = jnp.asarray(_rope_rows(IDX_HEAD_DIM, IDX_HEAD_DIM // ROPE_FRACTION // 2,
                                    IDX_HEAD_DIM // ROPE_FRACTION))
    w_scale = (IDX_HEADS ** -0.5) * (IDX_HEAD_DIM ** -0.5)
    rows_spec = pl.BlockSpec((3, 1, LANES), lambda i: (0, 0, 0))
    outs = pl.pallas_call(
        functools.partial(_rope_kernel, w_scale=w_scale),
        grid=(m // tm,),
        in_specs=[
            pl.BlockSpec((tm, 1), lambda i: (i, 0)),
            rows_spec, rows_spec,
            pl.BlockSpec((tm, ATT_Q_W), lambda i: (i, 0)),
            pl.BlockSpec((tm, 2 * ATT_KV_W), lambda i: (i, ATT_Q_W // (2 * ATT_KV_W))),
            pl.BlockSpec((tm, IDX_Q_W), lambda i: (i, (ATT_Q_W + 2 * ATT_KV_W) // IDX_Q_W)),
            pl.BlockSpec((tm, LANES), lambda i: (i, 0)),
        ],
        out_specs=[
            pl.BlockSpec((tm, ATT_Q_W), lambda i: (i, 0)),
            pl.BlockSpec((tm, ATT_KV_W), lambda i: (i, 0)),
            pl.BlockSpec((ATT_KV_W, tm), lambda i: (0, i)),
            pl.BlockSpec((tm, IDX_Q_W), lambda i: (i, 0)),
            pl.BlockSpec((tm, IDX_HEAD_DIM), lambda i: (i, 0)),
            pl.BlockSpec((IDX_HEADS, tm), lambda i: (0, i)),
        ],
        out_shape=[
            jax.ShapeDtypeStruct((m, ATT_Q_W), BF16),
            jax.ShapeDtypeStruct((m, ATT_KV_W), BF16),
            jax.ShapeDtypeStruct((ATT_KV_W, m), BF16),
            jax.ShapeDtypeStruct((m, IDX_Q_W), BF16),
            jax.ShapeDtypeStruct((m, IDX_HEAD_DIM), BF16),
            jax.ShapeDtypeStruct((IDX_HEADS, m), F32),
        ],
        compiler_params=_cparams(("arbitrary",)),
        name="rope_split",
    )(positions.reshape(m, 1), rows_a, rows_i, proj, proj, proj, small)
    return outs


def _attn_kernel(qi_ref, wi_ref, ki_ref, q_ref, k_ref, vt_ref, o_ref, skey_ref, hi_ref, lo_ref,
                 da_ref, db_ref, sa_ref, sb_ref, *, tq, kc, topk, seq):
    i = pl.program_id(1)
    nchunk = lax.shift_right_logical((i + 1) * tq + (kc - 1), int(np.log2(kc)))
    qpos = i * tq + lax.broadcasted_iota(jnp.int32, (1, tq), 1)
    i16 = jnp.int16
    half_bias = 1 << (HALF_BITS - 1)
    pk = 2 * SUBLANES
    rb = 4 * pk

    qi = qi_ref[0]
    wi = wi_ref[...]
    qi_s = jnp.concatenate([qi[:, h * IDX_HEAD_DIM:(h + 1) * IDX_HEAD_DIM] for h in range(IDX_HEADS)],
                           axis=0)

    kh = kc // 2
    last_half = seq // kh - 1

    def score_dots(half, dst_ref):
        k0 = pl.multiple_of(jnp.minimum(half, last_half) * kh, kh)
        dst_ref[...] = _dot_nt(ki_ref[0, pl.ds(k0, kh), :], qi_s)

    def score_keys(half, src_ref):
        k0 = pl.multiple_of(half * kh, kh)
        for j in range(kh // rb):
            rows = slice(j * rb, (j + 1) * rb)
            acc = jnp.zeros((rb, tq), F32)
            for h in range(IDX_HEADS):
                acc = acc + jnp.maximum(src_ref[rows, h * tq:(h + 1) * tq], 0.0) * wi[h:h + 1, :]
            kpos = k0 + j * rb + lax.broadcasted_iota(jnp.int32, (rb, tq), 0)
            bits = pltpu.bitcast(acc, jnp.int32)
            skey = jnp.where(bits < 0, bits ^ jnp.int32(0x7FFFFFFF), bits)
            skey = jnp.where(kpos <= qpos, skey, jnp.int32(INT_MIN))
            dst = pl.ds(k0 + j * rb, rb)
            skey_ref[dst, :] = skey
            hi_ref[dst, :] = lax.shift_right_arithmetic(skey, HALF_BITS).astype(i16)
            lo_ref[dst, :] = ((skey & jnp.int32(2 * half_bias - 1)) - half_bias).astype(i16)

    def score_chunk(c, carry):
        score_dots(2 * c + 1, db_ref)
        score_keys(2 * c, da_ref)
        score_dots(2 * c + 2, da_ref)
        score_keys(2 * c + 1, db_ref)
        return carry

    score_dots(0, da_ref)
    lax.fori_loop(0, nchunk, score_chunk, 0)

    nacc = 4

    def count_where(pred_fn):
        row_iota = lax.broadcasted_iota(jnp.int32, (SUBLANES, tq), 0)

        def chunk_body(c, parts):
            k0 = pl.multiple_of(c * kc, kc)
            parts = list(parts)
            xs = skey_ref[pl.ds(k0, kc), :]
            for j in range(kc // SUBLANES):
                x = xs[j * SUBLANES:(j + 1) * SUBLANES]
                parts[j % nacc] = parts[j % nacc] + jnp.where(pred_fn(x, k0 + j * SUBLANES + row_iota), 1, 0)
            return tuple(parts)
        zero = jnp.zeros((SUBLANES, tq), jnp.int32)
        parts = lax.fori_loop(0, nchunk, chunk_body, (zero,) * nacc)
        return jnp.sum(sum(parts[1:], parts[0]), axis=0, keepdims=True)

    nfull = seq // kc
    nshort = max(nfull // 2, 1)
    nstat = jnp.where(nchunk > nshort, nfull, nshort)

    def fill_chunk(c, carry):
        rows = pl.ds(pl.multiple_of(c * kc, kc), kc)
        hi_ref[rows, :] = jnp.full((kc, tq), -half_bias, i16)
        lo_ref[rows, :] = jnp.full((kc, tq), -half_bias, i16)
        return carry

    lax.fori_loop(nchunk, nstat, fill_chunk, 0)

    def search_all(nc):
        def count16(ref, pred_fn):
            parts = [jnp.zeros((pk, tq), i16)] * nacc
            for j in range(nc * kc // pk):
                parts[j % nacc] = parts[j % nacc] + jnp.where(pred_fn(ref[j * pk:(j + 1) * pk, :]), i16(1), i16(0))
            return jnp.sum(sum(parts[1:], parts[0]).astype(jnp.int32), axis=0, keepdims=True)

        def search16(ref, want):
            def bit_body(b, prefix):
                cand = prefix | lax.shift_left(jnp.int32(1), HALF_BITS - 1 - b)
                cand16 = jnp.broadcast_to(cand - half_bias, (pk, tq)).astype(i16)
                cnt = count16(ref, lambda x: x >= cand16)
                return jnp.where(cnt >= want, cand, prefix)
            return lax.fori_loop(0, HALF_BITS, bit_body, jnp.zeros((1, tq), jnp.int32))

        p_hi = search16(hi_ref, topk)
        t_hi = jnp.broadcast_to(p_hi - half_bias, (pk, tq)).astype(i16)
        n_hi_gt = count16(hi_ref, lambda x: x > t_hi)
        for j in range(nc * kc // pk):
            rows = slice(j * pk, (j + 1) * pk)
            lo_ref[rows, :] = jnp.where(hi_ref[rows, :] == t_hi, lo_ref[rows, :], i16(-half_bias))
        p_lo = search16(lo_ref, topk - n_hi_gt)
        return lax.shift_left(p_hi, HALF_BITS) | p_lo

    prefix = lax.cond(nchunk > nshort, lambda: search_all(nfull), lambda: search_all(nshort))
    thr = prefix ^ jnp.int32(INT_MIN)

    n_gt = count_where(lambda x, _: x > thr)
    n_eq = count_where(lambda x, _: x == thr)
    need = topk - n_gt
    excess = jnp.where((prefix != 0) & (n_eq > need), 1, 0)
    any_excess = jnp.max(excess) > 0
    idx_bits = int(np.log2(seq))

    def tie_search():
        def tie_bit(b, x):
            cand = x | lax.shift_left(jnp.int32(1), idx_bits - 1 - b)
            below = count_where(lambda xk, kidx: (xk == thr) & (kidx < cand))
            return jnp.where(below < need, cand, x)
        return lax.fori_loop(0, idx_bits, tie_bit, jnp.zeros((1, tq), jnp.int32))

    tie_last = lax.cond(any_excess, tie_search, lambda: jnp.full((1, tq), seq, jnp.int32))

    groups = ATT_KV_HEADS
    rep = ATT_HEADS // ATT_KV_HEADS
    scale2 = (ATT_HEAD_DIM ** -0.5) * float(np.log2(np.e))
    q = q_ref[0]
    q_g = [jnp.concatenate([q[:, (g * rep + r) * LANES:(g * rep + r + 1) * LANES] for r in range(rep)], axis=0)
           for g in range(groups)]

    def qk_dots(half, dst_ref):
        k0 = pl.multiple_of(jnp.minimum(half, last_half) * kh, kh)
        for g in range(groups):
            dst_ref[g] = _dot_nt(k_ref[0, pl.ds(k0, kh), g * LANES:(g + 1) * LANES], q_g[g])

    def softmax_pv(half, src_ref, carry):
        k0 = pl.multiple_of(half * kh, kh)
        x = skey_ref[pl.ds(k0, kh), :]
        kidx = k0 + lax.broadcasted_iota(jnp.int32, (kh, tq), 0)
        sel = (x > thr) | ((x == thr) & (kidx <= tie_last))
        sel = sel & (x != jnp.int32(INT_MIN))
        bias = jnp.where(sel, 0.0, MASK_VALUE)
        bias = jnp.concatenate([bias] * rep, axis=1)
        new = []
        for g in range(groups):
            m_old, l_old, acc_old = carry[g]
            vtg = vt_ref[g * LANES:(g + 1) * LANES, pl.ds(k0, kh)]
            sm = src_ref[g] * scale2 + bias
            m_new = jnp.maximum(m_old, jnp.max(sm, axis=0, keepdims=True))
            p = jnp.exp2(sm - m_new)
            alpha = jnp.exp2(m_old - m_new)
            l_new = alpha * l_old + jnp.sum(p, axis=0, keepdims=True)
            acc_new = alpha * acc_old + _dot(vtg, p.astype(BF16))
            new.append((m_new, l_new, acc_new))
        return tuple(new)

    def attn_chunk(c, carry):
        qk_dots(2 * c + 1, sb_ref)
        carry = softmax_pv(2 * c, sa_ref, carry)
        qk_dots(2 * c + 2, sa_ref)
        return softmax_pv(2 * c + 1, sb_ref, carry)

    init = tuple((jnp.full((1, rep * tq), MASK_VALUE, F32), jnp.zeros((1, rep * tq), F32),
                  jnp.zeros((LANES, rep * tq), F32)) for _ in range(groups))
    qk_dots(0, sa_ref)
    fin = lax.fori_loop(0, nchunk, attn_chunk, init)
    for g in range(groups):
        _, l_f, acc_f = fin[g]
        o_t = acc_f / l_f
        for r in range(rep):
            h = g * rep + r
            o_ref[0, :, h * LANES:(h + 1) * LANES] = o_t[:, r * tq:(r + 1) * tq].T.astype(o_ref.dtype)


def _dsa_attention(q, k, vt, qi, ki, wit, batch, seq, topk, tq=128, kc=512):
    nblk = seq // tq
    rep = ATT_HEADS // ATT_KV_HEADS
    r3 = lambda a: a.reshape(batch, seq, a.shape[-1])
    qblk = lambda w: pl.BlockSpec((1, tq, w), lambda b, i: (b, i, 0))
    full = lambda w: pl.BlockSpec((1, seq, w), lambda b, i: (b, 0, 0))
    out = pl.pallas_call(
        functools.partial(_attn_kernel, tq=tq, kc=kc, topk=topk, seq=seq),
        grid=(batch, nblk),
        in_specs=[qblk(IDX_Q_W), pl.BlockSpec((IDX_HEADS, tq), lambda b, i: (0, b * nblk + i)),
                  full(IDX_HEAD_DIM), qblk(ATT_Q_W), full(ATT_KV_W),
                  pl.BlockSpec((ATT_KV_W, seq), lambda b, i: (0, b))],
        out_specs=qblk(ATT_Q_W),
        out_shape=jax.ShapeDtypeStruct((batch, seq, ATT_Q_W), BF16),
        scratch_shapes=[pltpu.VMEM((seq, tq), jnp.int32), pltpu.VMEM((seq, tq), jnp.int16),
                        pltpu.VMEM((seq, tq), jnp.int16),
                        pltpu.VMEM((kc // 2, IDX_HEADS * tq), F32), pltpu.VMEM((kc // 2, IDX_HEADS * tq), F32),
                        pltpu.VMEM((ATT_KV_HEADS, kc // 2, rep * tq), F32),
                        pltpu.VMEM((ATT_KV_HEADS, kc // 2, rep * tq), F32)],
        compiler_params=_cparams(("arbitrary", "arbitrary")),
        name="dsa_attention",
    )(r3(qi), wit, r3(ki), r3(q), r3(k), vt)
    return out.reshape(batch * seq, ATT_Q_W)


def _conv_kernel(cb_ref, cc_ref, cx_ref, hc_ref, hx_ref, w_ref, o_ref, *, tm, seq):
    i = pl.program_id(0)
    u = cc_ref[...] * cx_ref[...]
    halo = hc_ref[...] * hx_ref[...]
    halo = jnp.where((i * tm) % seq == 0, 0.0, halo)
    row = lax.broadcasted_iota(jnp.int32, u.shape, 0)
    u1 = pltpu.roll(u, 1, axis=0)
    u2 = pltpu.roll(u, 2, axis=0)
    h1 = halo[SUBLANES - 1:SUBLANES, :]
    h2 = halo[SUBLANES - 2:SUBLANES - 1, :]
    u1 = jnp.where(row == 0, h1, u1)
    u2 = jnp.where(row == 0, h2, jnp.where(row == 1, h1, u2))
    w = w_ref[...]
    conv = u2 * w[0:1, :] + u1 * w[1:2, :] + u * w[2:3, :]
    o_ref[...] = (cb_ref[...] * conv).astype(o_ref.dtype)


def _short_conv(proj, conv_w, seq, col0, tm=512, tc=512):
    m = proj.shape[0]
    nb = CONV_WIDTH // tc
    c0 = col0 // tc
    rpb = tm // SUBLANES
    main = lambda off: pl.BlockSpec((tm, tc), lambda i, j: (i, c0 + off * nb + j))
    halo = lambda off: pl.BlockSpec((SUBLANES, tc),
                                    lambda i, j: (jnp.maximum(i * rpb - 1, 0), c0 + off * nb + j))
    return pl.pallas_call(
        functools.partial(_conv_kernel, tm=tm, seq=seq),
        grid=(m // tm, nb),
        in_specs=[main(0), main(1), main(2), halo(1), halo(2),
                  pl.BlockSpec((CONV_K, tc), lambda i, j: (0, j))],
        out_specs=pl.BlockSpec((tm, tc), lambda i, j: (i, j)),
        out_shape=jax.ShapeDtypeStruct((m, CONV_WIDTH), BF16),
        compiler_params=_cparams(("arbitrary", "arbitrary")),
        name="short_conv",
    )(proj, proj, proj, proj, proj, conv_w)


def _hg_tables():
    c = HG_CHUNK
    t = np.arange(c)
    mats, masks = [], []
    for l in range(HG_LEVELS):
        h = 1 << l
        start = (t // (2 * h)) * (2 * h)
        p = start + h - 1
        right = (t - start) >= h
        u = t[None, :]
        r_m = right[:, None] & (u > p[:, None]) & (u <= t[:, None])
        l_m = (~right)[:, None] & (u > t[:, None]) & (u <= p[:, None])
        mats.append(r_m | l_m)
        same = (t[:, None] // (2 * h)) == (t[None, :] // (2 * h))
        masks.append(same & right[:, None] & (~right)[None, :])
    mats.append(t[None, :] <= t[:, None])
    mats.append(t[None, :] > t[:, None])
    mat = np.concatenate(mats, 0).astype(np.float32)
    return np.concatenate([mat] * HG_SPLIT, 1), np.stack(masks).astype(np.float32)


def _split3(x):
    hi = x.astype(BF16)
    r1 = x - hi.astype(F32)
    mid = r1.astype(BF16)
    lo = (r1 - mid.astype(F32)).astype(BF16)
    return jnp.concatenate([hi, mid, lo], axis=0)


def _hg_kernel(hq_ref, hf_ref, hi_ref, hg_ref, lb_ref, g_ref, mat_ref, mask_ref, o_ref, state_ref, *, ts):
    c = HG_CHUNK

    @pl.when(pl.program_id(1) == 0)
    def _():
        state_ref[...] = jnp.zeros_like(state_ref)

    g = g_ref[...]

    def head_chunk(h, rows):
        cols = slice(h * LANES, (h + 1) * LANES)
        lb = lb_ref[:, cols]
        f = lb + (1.0 - lb) * _sigmoid(hf_ref[rows, cols])
        logf = jnp.log(jnp.maximum(f, F_MIN))
        kin = 1.0 - f
        q = hq_ref[rows, cols]
        v = hi_ref[rows, cols]
        v16 = v.astype(BF16)
        e = jnp.exp(_dot(mat_ref[...], _split3(logf)))
        attn = jnp.zeros((c, c), F32)
        for l in range(HG_LEVELS):
            el = e[l * c:(l + 1) * c]
            attn = attn + _dot_nt((q * el).astype(BF16), (kin * el).astype(BF16)) * mask_ref[l]
        e_b = e[HG_LEVELS * c:(HG_LEVELS + 1) * c]
        e_k = e[(HG_LEVELS + 1) * c:(HG_LEVELS + 2) * c]
        state_t = state_ref[h]
        diag = jnp.sum(q * kin, axis=1, keepdims=True)
        out = (_dot(attn.astype(BF16), v16) + diag * v
               + _dot_nt((q * e_b).astype(BF16), state_t.astype(BF16)))
        kv_t = lax.dot_general(v16, (kin * e_k).astype(BF16), (((0,), (0,)), ((), ())),
                               preferred_element_type=F32)
        state_ref[h] = state_t * e_b[c - 1:c, :] + kv_t
        y = out * lax.rsqrt(jnp.mean(out * out, axis=-1, keepdims=True) + NORM_EPS) * g
        gate = hg_ref[rows, cols]
        o_ref[rows, cols] = (y * (gate * _sigmoid(gate))).astype(o_ref.dtype)

    def chunk(ci, carry):
        rows = pl.ds(pl.multiple_of(ci * c, c), c)
        for h in range(HG_HEADS):
            head_chunk(h, rows)
        return carry

    lax.fori_loop(0, ts // c, chunk, 0)


def _hgrn2(proj, lb, norm_g, batch, seq, col0, ts=512):
    m = proj.shape[0]
    c0 = col0 // HG_W
    spb = seq // ts
    mat, masks = _hg_tables()
    col = lambda off: pl.BlockSpec((ts, HG_W), lambda b, s: (b * spb + s, c0 + off))
    return pl.pallas_call(
        functools.partial(_hg_kernel, ts=ts),
        grid=(batch, spb),
        in_specs=[col(0), col(1), col(2), col(3),
                  pl.BlockSpec((1, HG_W), lambda b, s: (0, 0)),
                  pl.BlockSpec((1, LANES), lambda b, s: (0, 0)),
                  pl.BlockSpec(mat.shape, lambda b, s: (0, 0)),
                  pl.BlockSpec(masks.shape, lambda b, s: (0, 0, 0))],
        out_specs=pl.BlockSpec((ts, HG_W), lambda b, s: (b * spb + s, 0)),
        out_shape=jax.ShapeDtypeStruct((m, HG_W), BF16),
        scratch_shapes=[pltpu.VMEM((HG_HEADS, HG_VAL_DIM, HG_KEY_DIM), F32)],
        compiler_params=_cparams(("arbitrary", "arbitrary")),
        name="hgrn2",
    )(proj, proj, proj, proj, lb.reshape(1, HG_W), norm_g.reshape(1, HG_VAL_DIM),
      jnp.asarray(mat, BF16), jnp.asarray(masks))


def _lb_kernel(x_ref, o_ref):
    x = x_ref[...]
    e = jnp.exp(x - jnp.max(x, axis=0, keepdims=True))
    p = e / jnp.sum(e, axis=0, keepdims=True)
    depth = x.shape[0]
    run = jnp.zeros_like(p[0:1])
    for l in range(depth):
        run = run + p[l:l + 1]
        o_ref[l:l + 1, :] = run - p[0:1]


def _lower_bounds(hg_lower_bounds):
    return pl.pallas_call(
        _lb_kernel,
        out_shape=jax.ShapeDtypeStruct(hg_lower_bounds.shape, F32),
        name="hg_lower_bounds",
    )(hg_lower_bounds.astype(F32))


_COL_Q = 0
_COL_CONV = ATT_Q_W + 2 * ATT_KV_W + IDX_Q_W
_COL_HG = _COL_CONV + 3 * CONV_WIDTH
_COL_GATE = _COL_HG + 4 * HG_W


def _split_w_in(w_in, d_model):
    sizes = (ATT_Q_W, ATT_KV_W, ATT_KV_W, IDX_Q_W, IDX_HEAD_DIM, IDX_HEADS,
             CONV_WIDTH, CONV_WIDTH, CONV_WIDTH, HG_W, HG_W, HG_W, HG_W, N_BRANCHES * d_model)
    offs = np.cumsum((0,) + sizes)
    part = lambda n: w_in[:, offs[n]:offs[n + 1]]
    main = jnp.concatenate([part(0), part(1), part(2), part(3)] + [part(n) for n in range(6, 14)],
                           axis=1).astype(BF16)
    small = jnp.concatenate([part(4), part(5)], axis=1)
    small = jnp.pad(small, ((0, 0), (0, LANES - small.shape[1]))).astype(BF16)
    return main, small


def kernel(x, c, positions, ada_w, ada_b, norm_mix_g, w_in, conv_w, hg_lower_bounds, hg_norm_g,
           w_o_attn, w_o_conv, w_o_hgrn, w_out, norm_mlp_g, w_mlp1, w_mlp2, final_norm_g):
    batch, seq, d = x.shape
    depth = ada_w.shape[0]
    m = batch * seq
    topk = min(INDEX_TOPK, seq // 4)
    lb_all = _lower_bounds(hg_lower_bounds)
    mod = _modulation(c, ada_w, ada_b)
    x2 = x.reshape(m, d)
    for l in range(depth):
        sh1, sc1, g1, sh2, sc2, g2 = [mod[l, :, n * d:(n + 1) * d] for n in range(ADA_CHUNKS)]
        w_main, w_small = _split_w_in(w_in[l], d)
        h = _norm(x2, norm_mix_g[l], seq, sh1, sc1)
        proj = _matmul(h, w_main, out_dtype=F32, tn=768)
        small = _matmul(h, w_small, out_dtype=F32)
        q, k, v, qi, ki, wi = _rope_split(proj, small, positions)
        att = _dsa_attention(q, k, v, qi, ki, wi, batch, seq, topk)
        cv = _short_conv(proj, conv_w[l], seq, _COL_CONV)
        hg = _hgrn2(proj, lb_all[l], hg_norm_g[l], batch, seq, _COL_HG)
        mrg = _matmul(att, w_o_attn[l].astype(BF16), out_dtype=F32, epilogue="gated",
                      gate=proj, gate_off=_COL_GATE)
        mrg = _matmul(cv, w_o_conv[l].astype(BF16), out_dtype=F32, epilogue="gated",
                      gate=proj, gate_off=_COL_GATE + d, prev=mrg)
        mrg = _matmul(hg, w_o_hgrn[l].astype(BF16), out_dtype=BF16, epilogue="gated",
                      gate=proj, gate_off=_COL_GATE + 2 * d, prev=mrg)
        x2 = _matmul(mrg, w_out[l].astype(BF16), out_dtype=F32, epilogue="residual",
                     res=x2, gate_row=g1, seq=seq)
        h = _norm(x2, norm_mlp_g[l], seq, sh2, sc2)
        a = _matmul(h, w_mlp1[l].astype(BF16), out_dtype=BF16, epilogue="relu2")
        x2 = _matmul(a, w_mlp2[l].astype(BF16), out_dtype=F32, epilogue="residual", tk=2048,
                     res=x2, gate_row=g2, seq=seq)
    out = _norm(x2, final_norm_g, seq, out_dtype=x.dtype)
    return out.reshape(batch, seq, d)
```

```python
import functools

import numpy as np
import jax
import jax.numpy as jnp
from jax import lax
from jax.experimental import pallas as pl
from jax.experimental.pallas import tpu as pltpu

ATT_HEADS = 8
ATT_KV_HEADS = 2
ATT_HEAD_DIM = 128
IDX_HEADS = 8
IDX_HEAD_DIM = 64
INDEX_TOPK = 256
ROPE_THETA = 500000.0
ROPE_FRACTION = 4
MASK_VALUE = -1e30
CONV_WIDTH = 1024
CONV_K = 3
HG_HEADS = 8
HG_KEY_DIM = 128
HG_VAL_DIM = 128
F_MIN = 1e-30
N_BRANCHES = 3
NORM_EPS = 1e-6
ADA_CHUNKS = 6

ATT_Q_W = ATT_HEADS * ATT_HEAD_DIM
ATT_KV_W = ATT_KV_HEADS * ATT_HEAD_DIM
IDX_Q_W = IDX_HEADS * IDX_HEAD_DIM
HG_W = HG_HEADS * HG_KEY_DIM

LANES = 128
SUBLANES = 8
VMEM_LIMIT_BYTES = 56 * 1024 * 1024

INT_MIN = -2 ** 31
SEARCH_VARIANTS = 4
HG_CHUNK = 64
HG_LEVELS = 6
HG_SPLIT = 3

BF16 = jnp.bfloat16
F32 = jnp.float32


def _cparams(sem):
    return pltpu.CompilerParams(dimension_semantics=sem, vmem_limit_bytes=VMEM_LIMIT_BYTES)


def _dot(a, b):
    return jnp.dot(a, b, preferred_element_type=F32)


def _dot_nt(a, b):
    return lax.dot_general(a, b, (((1,), (1,)), ((), ())), preferred_element_type=F32)


def _sigmoid(x):
    return 1.0 / (1.0 + jnp.exp(-x))


def _mod_kernel(c_ref, w_ref, b_ref, o_ref):
    c = c_ref[...]
    ca = (c * _sigmoid(c)).astype(BF16)
    o_ref[...] = _dot(ca, w_ref[...].astype(BF16)) + b_ref[...]


def _modulation(c, ada_w, ada_b):
    depth, d, n = ada_w.shape
    b = c.shape[0]
    bp = -(-b // SUBLANES) * SUBLANES
    cp = jnp.pad(c, ((0, bp - b), (0, 0)))
    tn = 1024
    out = pl.pallas_call(
        _mod_kernel,
        grid=(depth, n // tn),
        in_specs=[
            pl.BlockSpec((bp, d), lambda l, j: (0, 0)),
            pl.BlockSpec((None, d, tn), lambda l, j: (l, 0, j)),
            pl.BlockSpec((None, 1, tn), lambda l, j: (l, 0, j)),
        ],
        out_specs=pl.BlockSpec((None, bp, tn), lambda l, j: (l, 0, j)),
        out_shape=jax.ShapeDtypeStruct((depth, bp, n), F32),
        compiler_params=_cparams(("arbitrary", "arbitrary")),
        name="adaln_mod",
    )(cp, ada_w, ada_b.reshape(depth, 1, n))
    return out[:, :b]


def _norm_kernel(x_ref, g_ref, *rest, modulate):
    if modulate:
        sh_ref, sc_ref, o_ref = rest
    else:
        (o_ref,) = rest
    x = x_ref[...]
    y = x * lax.rsqrt(jnp.mean(x * x, axis=-1, keepdims=True) + NORM_EPS) * g_ref[...]
    if modulate:
        y = y * (1.0 + sc_ref[0]) + sh_ref[0]
    o_ref[...] = y.astype(o_ref.dtype)


def _norm(x2, g, seq, shift=None, scale=None, out_dtype=BF16, tm=512):
    m, d = x2.shape
    modulate = shift is not None
    tpb = seq // tm
    in_specs = [pl.BlockSpec((tm, d), lambda i: (i, 0)),
                pl.BlockSpec((1, d), lambda i: (0, 0))]
    args = [x2, g.reshape(1, d)]
    if modulate:
        row = pl.BlockSpec((1, 1, d), lambda i: (i // tpb, 0, 0))
        in_specs += [row, row]
        args += [shift[:, None, :], scale[:, None, :]]
    return pl.pallas_call(
        functools.partial(_norm_kernel, modulate=modulate),
        grid=(m // tm,),
        in_specs=in_specs,
        out_specs=pl.BlockSpec((tm, d), lambda i: (i, 0)),
        out_shape=jax.ShapeDtypeStruct((m, d), out_dtype),
        compiler_params=_cparams(("arbitrary",)),
        name="rmsnorm",
    )(*args)


def _norm_mm_kernel(x_ref, g_ref, sh_ref, sc_ref, w_ref, o_ref, *rest, relu2, emit_h):
    h_ref = rest[-1]

    @pl.when(pl.program_id(1) == 0)
    def _():
        x = x_ref[...]
        y = x * lax.rsqrt(jnp.mean(x * x, axis=-1, keepdims=True) + NORM_EPS) * g_ref[...]
        h_ref[...] = (y * (1.0 + sc_ref[0]) + sh_ref[0]).astype(h_ref.dtype)
        if emit_h:
            rest[0][...] = h_ref[...]

    y = _dot(h_ref[...], w_ref[...])
    if relu2:
        y = jnp.square(jnp.maximum(y, 0.0))
    o_ref[...] = y.astype(o_ref.dtype)


def _norm_matmul(x2, g, shift, scale, w, seq, *, out_dtype, relu2=False, emit_h=False, tm=1024, tn=512):
    m, d = x2.shape
    n = w.shape[1]
    tpb = seq // tm
    row = pl.BlockSpec((1, 1, d), lambda i, j: (i // tpb, 0, 0))
    out_specs = [pl.BlockSpec((tm, tn), lambda i, j: (i, j))]
    out_shape = [jax.ShapeDtypeStruct((m, n), out_dtype)]
    if emit_h:
        out_specs.append(pl.BlockSpec((tm, d), lambda i, j: (i, 0)))
        out_shape.append(jax.ShapeDtypeStruct((m, d), BF16))
    outs = pl.pallas_call(
        functools.partial(_norm_mm_kernel, relu2=relu2, emit_h=emit_h),
        grid=(m // tm, n // tn),
        in_specs=[pl.BlockSpec((tm, d), lambda i, j: (i, 0)),
                  pl.BlockSpec((1, d), lambda i, j: (0, 0)), row, row,
                  pl.BlockSpec((d, tn), lambda i, j: (0, j))],
        out_specs=out_specs,
        out_shape=out_shape,
        scratch_shapes=[pltpu.VMEM((tm, d), BF16)],
        compiler_params=_cparams(("arbitrary", "arbitrary")),
        name="norm_matmul",
    )(x2, g.reshape(1, d), shift[:, None, :], scale[:, None, :], w)
    return outs if emit_h else outs[0]


def _merge_kernel(att_ref, cv_ref, hg_ref, wa_ref, wc_ref, wh_ref, ga_ref, gc_ref, gh_ref,
                  wo_ref, x_ref, gr_ref, o_ref, acc_ref):
    j = pl.program_id(1)
    merged = (_sigmoid(ga_ref[...]) * _dot(att_ref[...], wa_ref[...])
              + _sigmoid(gc_ref[...]) * _dot(cv_ref[...], wc_ref[...])
              + _sigmoid(gh_ref[...]) * _dot(hg_ref[...], wh_ref[...]))
    part = _dot(merged.astype(BF16), wo_ref[...])

    @pl.when(j == 0)
    def _():
        acc_ref[...] = part

    @pl.when(j > 0)
    def _():
        acc_ref[...] += part

    @pl.when(j == pl.num_programs(1) - 1)
    def _():
        o_ref[...] = x_ref[...] + gr_ref[0] * acc_ref[...]


def _merge_out(att, cv, hg, wa, wc, wh, proj, gate_off, wout, x2, gate_row, seq, tm=512, tn=512):
    m, d = x2.shape
    tpb = seq // tm
    g0 = gate_off // tn
    nd = d // tn
    act = lambda a: pl.BlockSpec((tm, a.shape[1]), lambda i, j: (i, 0))
    wcol = lambda w: pl.BlockSpec((w.shape[0], tn), lambda i, j: (0, j))
    gate = lambda b: pl.BlockSpec((tm, tn), lambda i, j: (i, g0 + b * nd + j))
    return pl.pallas_call(
        _merge_kernel,
        grid=(m // tm, nd),
        in_specs=[act(att), act(cv), act(hg), wcol(wa), wcol(wc), wcol(wh), gate(0), gate(1), gate(2),
                  pl.BlockSpec((tn, d), lambda i, j: (j, 0)),
                  pl.BlockSpec((tm, d), lambda i, j: (i, 0)),
                  pl.BlockSpec((1, 1, d), lambda i, j: (i // tpb, 0, 0))],
        out_specs=pl.BlockSpec((tm, d), lambda i, j: (i, 0)),
        out_shape=jax.ShapeDtypeStruct((m, d), F32),
        scratch_shapes=[pltpu.VMEM((tm, d), F32)],
        compiler_params=_cparams(("arbitrary", "arbitrary")),
        name="merge_out",
    )(att, cv, hg, wa, wc, wh, proj, proj, proj, wout, x2, gate_row[:, None, :])


def _mm_kernel(*refs, epilogue, nk):
    a_ref, w_ref = refs[0], refs[1]
    pos = 2
    extra = []
    n_extra = {"none": 0, "residual": 2}[epilogue]
    for _ in range(n_extra):
        extra.append(refs[pos])
        pos += 1
    o_ref = refs[pos]
    acc_ref = refs[pos + 1] if nk > 1 else None

    def finish(y):
        if epilogue == "residual":
            y = extra[0][...] + extra[1][0] * y
        o_ref[...] = y.astype(o_ref.dtype)

    if nk == 1:
        finish(_dot(a_ref[...], w_ref[...]))
    else:
        k = pl.program_id(2)

        @pl.when(k == 0)
        def _():
            acc_ref[...] = jnp.zeros_like(acc_ref)

        acc_ref[...] += _dot(a_ref[...], w_ref[...])

        @pl.when(k == nk - 1)
        def _():
            finish(acc_ref[...])


def _matmul(a, w, *, out_dtype, epilogue="none", tm=1024, tn=512, tk=None,
            res=None, gate_row=None, seq=None):
    m = a.shape[0]
    kdim, n = w.shape
    tk = kdim if tk is None else tk
    nk = kdim // tk
    tn = min(tn, n)
    in_specs = [pl.BlockSpec((tm, tk), lambda i, j, k: (i, k)),
                pl.BlockSpec((tk, tn), lambda i, j, k: (k, j))]
    args = [a, w]
    if epilogue == "residual":
        tpb = seq // tm
        in_specs += [pl.BlockSpec((tm, tn), lambda i, j, k: (i, j)),
                     pl.BlockSpec((1, 1, tn), lambda i, j, k: (i // tpb, 0, j))]
        args += [res, gate_row[:, None, :]]
    scratch = [pltpu.VMEM((tm, tn), F32)] if nk > 1 else []
    return pl.pallas_call(
        functools.partial(_mm_kernel, epilogue=epilogue, nk=nk),
        grid=(m // tm, n // tn, nk),
        in_specs=in_specs,
        out_specs=pl.BlockSpec((tm, tn), lambda i, j, k: (i, j)),
        out_shape=jax.ShapeDtypeStruct((m, n), out_dtype),
        scratch_shapes=scratch,
        compiler_params=_cparams(("arbitrary", "arbitrary", "arbitrary")),
        name="matmul_" + epilogue,
    )(*args)


def _rope_rows(period, half, rot):
    j = np.arange(LANES) % period
    inv = ROPE_THETA ** (-(np.arange(half, dtype=np.float32)) / np.float32(half))
    inv_row = np.where(j < rot, inv.astype(np.float32)[j % half], 0.0).astype(np.float32)
    sign_row = np.where(j < half, -1.0, np.where(j < rot, 1.0, 0.0)).astype(np.float32)
    first_row = (j < half).astype(np.float32)
    return np.stack([inv_row, sign_row, first_row])[:, None, :]


def _rope_apply(x, cos_t, sin_s, first, half):
    up = pltpu.roll(x, LANES - half, axis=1)
    dn = pltpu.roll(x, half, axis=1)
    sw = jnp.where(first > 0.5, up, dn)
    return x * cos_t + sw * sin_s


def _rope_kernel(pos_ref, rows_a_ref, rows_i_ref, q_ref, kv_ref, qi_ref, sm_ref,
                 qo_ref, ko_ref, vo_ref, qio_ref, kio_ref, wio_ref, *, w_scale):
    pos = pos_ref[...].astype(F32)
    ang_a = pos * rows_a_ref[0]
    cos_a, sin_a = jnp.cos(ang_a), jnp.sin(ang_a) * rows_a_ref[1]
    first_a = rows_a_ref[2]
    ang_i = pos * rows_i_ref[0]
    cos_i, sin_i = jnp.cos(ang_i), jnp.sin(ang_i) * rows_i_ref[1]
    first_i = rows_i_ref[2]
    half_a = ATT_HEAD_DIM // ROPE_FRACTION // 2
    half_i = IDX_HEAD_DIM // ROPE_FRACTION // 2
    for h in range(ATT_HEADS):
        sl = slice(h * LANES, (h + 1) * LANES)
        qo_ref[:, sl] = _rope_apply(q_ref[:, sl], cos_a, sin_a, first_a, half_a).astype(BF16)
    for h in range(ATT_KV_HEADS):
        sl = slice(h * LANES, (h + 1) * LANES)
        ko_ref[:, sl] = _rope_apply(kv_ref[:, sl], cos_a, sin_a, first_a, half_a).astype(BF16)
    vo_ref[...] = kv_ref[:, ATT_KV_W:].T.astype(BF16)
    for h in range(IDX_Q_W // LANES):
        sl = slice(h * LANES, (h + 1) * LANES)
        qio_ref[:, sl] = _rope_apply(qi_ref[:, sl], cos_i, sin_i, first_i, half_i).astype(BF16)
    sm = sm_ref[...]
    ki = _rope_apply(sm, cos_i, sin_i, first_i, half_i)
    kio_ref[...] = ki[:, :IDX_HEAD_DIM].astype(BF16)
    wio_ref[...] = sm.T[IDX_HEAD_DIM:IDX_HEAD_DIM + IDX_HEADS, :] * w_scale


def _rope_split(proj, small, positions, tm=512):
    m = proj.shape[0]
    rows_a = jnp.asarray(_rope_rows(ATT_HEAD_DIM, ATT_HEAD_DIM // ROPE_FRACTION // 2,
                                    ATT_HEAD_DIM // ROPE_FRACTION))
    rows_i = jnp.asarray(_rope_rows(IDX_HEAD_DIM, IDX_HEAD_DIM // ROPE_FRACTION // 2,
                                    IDX_HEAD_DIM // ROPE_FRACTION))
    w_scale = (IDX_HEADS ** -0.5) * (IDX_HEAD_DIM ** -0.5)
    rows_spec = pl.BlockSpec((3, 1, LANES), lambda i: (0, 0, 0))
    outs = pl.pallas_call(
        functools.partial(_rope_kernel, w_scale=w_scale),
        grid=(m // tm,),
        in_specs=[
            pl.BlockSpec((tm, 1), lambda i: (i, 0)),
            rows_spec, rows_spec,
            pl.BlockSpec((tm, ATT_Q_W), lambda i: (i, 0)),
            pl.BlockSpec((tm, 2 * ATT_KV_W), lambda i: (i, ATT_Q_W // (2 * ATT_KV_W))),
            pl.BlockSpec((tm, IDX_Q_W), lambda i: (i, (ATT_Q_W + 2 * ATT_KV_W) // IDX_Q_W)),
            pl.BlockSpec((tm, LANES), lambda i: (i, 0)),
        ],
        out_specs=[
            pl.BlockSpec((tm, ATT_Q_W), lambda i: (i, 0)),
            pl.BlockSpec((tm, ATT_KV_W), lambda i: (i, 0)),
            pl.BlockSpec((ATT_KV_W, tm), lambda i: (0, i)),
            pl.BlockSpec((tm, IDX_Q_W), lambda i: (i, 0)),
            pl.BlockSpec((tm, IDX_HEAD_DIM), lambda i: (i, 0)),
            pl.BlockSpec((IDX_HEADS, tm), lambda i: (0, i)),
        ],
        out_shape=[
            jax.ShapeDtypeStruct((m, ATT_Q_W), BF16),
            jax.ShapeDtypeStruct((m, ATT_KV_W), BF16),
            jax.ShapeDtypeStruct((ATT_KV_W, m), BF16),
            jax.ShapeDtypeStruct((m, IDX_Q_W), BF16),
            jax.ShapeDtypeStruct((m, IDX_HEAD_DIM), BF16),
            jax.ShapeDtypeStruct((IDX_HEADS, m), F32),
        ],
        compiler_params=_cparams(("arbitrary",)),
        name="rope_split",
    )(positions.reshape(m, 1), rows_a, rows_i, proj, proj, proj, small)
    return outs


def _attn_kernel(qi_ref, wi_ref, ki_ref, q_ref, k_ref, vt_ref, o_ref, skey_ref,
                 da_ref, db_ref, sa_ref, sb_ref, *, tq, kc, topk, seq):
    i = pl.program_id(1)
    nchunk = lax.shift_right_logical((i + 1) * tq + (kc - 1), int(np.log2(kc)))
    qpos = i * tq + lax.broadcasted_iota(jnp.int32, (1, tq), 1)
    rb = 8 * SUBLANES

    qi = qi_ref[0]
    wi = wi_ref[...]
    qi_s = jnp.concatenate([qi[:, h * IDX_HEAD_DIM:(h + 1) * IDX_HEAD_DIM] for h in range(IDX_HEADS)],
                           axis=0)

    kh = kc // 2
    last_half = seq // kh - 1

    def score_dots(half, dst_ref):
        k0 = pl.multiple_of(jnp.minimum(half, last_half) * kh, kh)
        dst_ref[...] = _dot_nt(ki_ref[0, pl.ds(k0, kh), :], qi_s)

    def score_keys(half, src_ref):
        k0 = pl.multiple_of(half * kh, kh)
        for j in range(kh // rb):
            rows = slice(j * rb, (j + 1) * rb)
            acc = jnp.zeros((rb, tq), F32)
            for h in range(IDX_HEADS):
                acc = acc + jnp.maximum(src_ref[rows, h * tq:(h + 1) * tq], 0.0) * wi[h:h + 1, :]
            kpos = k0 + j * rb + lax.broadcasted_iota(jnp.int32, (rb, tq), 0)
            bits = pltpu.bitcast(acc, jnp.int32)
            skey = jnp.where(bits < 0, bits ^ jnp.int32(0x7FFFFFFF), bits)
            skey = jnp.where(kpos <= qpos, skey, jnp.int32(INT_MIN))
            skey_ref[pl.ds(k0 + j * rb, rb), :] = skey

    def score_chunk(c, carry):
        score_dots(2 * c + 1, db_ref)
        score_keys(2 * c, da_ref)
        score_dots(2 * c + 2, da_ref)
        score_keys(2 * c + 1, db_ref)
        return carry

    score_dots(0, da_ref)
    lax.fori_loop(0, nchunk, score_chunk, 0)

    nacc = 4

    def count_where(pred_fn):
        row_iota = lax.broadcasted_iota(jnp.int32, (SUBLANES, tq), 0)

        def chunk_body(c, parts):
            k0 = pl.multiple_of(c * kc, kc)
            parts = list(parts)
            xs = skey_ref[pl.ds(k0, kc), :]
            for j in range(kc // SUBLANES):
                x = xs[j * SUBLANES:(j + 1) * SUBLANES]
                parts[j % nacc] = parts[j % nacc] + jnp.where(pred_fn(x, k0 + j * SUBLANES + row_iota), 1, 0)
            return tuple(parts)
        zero = jnp.zeros((SUBLANES, tq), jnp.int32)
        parts = lax.fori_loop(0, nchunk, chunk_body, (zero,) * nacc)
        return jnp.sum(sum(parts[1:], parts[0]), axis=0, keepdims=True)

    nfull = seq // kc
    step = max(nfull // SEARCH_VARIANTS, 1)
    variant = lax.div(nchunk + (step - 1), step) - 1

    def fill_chunk(c, carry):
        rows = pl.ds(pl.multiple_of(c * kc, kc), kc)
        skey_ref[rows, :] = jnp.full((kc, tq), INT_MIN, jnp.int32)
        return carry

    lax.fori_loop(nchunk, (variant + 1) * step, fill_chunk, 0)

    def search_all(nc):
        def bit_body(b, prefix):
            cand = prefix | lax.shift_left(jnp.int32(1), 31 - b)
            cand_s = jnp.broadcast_to(cand ^ jnp.int32(INT_MIN), (SUBLANES, tq))
            parts = [jnp.zeros((SUBLANES, tq), jnp.int32)] * nacc
            for j in range(nc * kc // SUBLANES):
                x = skey_ref[j * SUBLANES:(j + 1) * SUBLANES, :]
                parts[j % nacc] = parts[j % nacc] + jnp.where(x >= cand_s, 1, 0)
            cnt = jnp.sum(sum(parts[1:], parts[0]), axis=0, keepdims=True)
            return jnp.where(cnt >= topk, cand, prefix)
        return lax.fori_loop(0, 32, bit_body, jnp.zeros((1, tq), jnp.int32))

    prefix = lax.switch(variant, [functools.partial(search_all, nc) for nc in range(step, nfull + 1, step)])
    thr = prefix ^ jnp.int32(INT_MIN)

    n_gt = count_where(lambda x, _: x > thr)
    n_eq = count_where(lambda x, _: x == thr)
    need = topk - n_gt
    excess = jnp.where((prefix != 0) & (n_eq > need), 1, 0)
    any_excess = jnp.max(excess) > 0
    idx_bits = int(np.log2(seq))

    def tie_search():
        def tie_bit(b, x):
            cand = x | lax.shift_left(jnp.int32(1), idx_bits - 1 - b)
            below = count_where(lambda xk, kidx: (xk == thr) & (kidx < cand))
            return jnp.where(below < need, cand, x)
        return lax.fori_loop(0, idx_bits, tie_bit, jnp.zeros((1, tq), jnp.int32))

    tie_last = lax.cond(any_excess, tie_search, lambda: jnp.full((1, tq), seq, jnp.int32))

    groups = ATT_KV_HEADS
    rep = ATT_HEADS // ATT_KV_HEADS
    scale2 = (ATT_HEAD_DIM ** -0.5) * float(np.log2(np.e))
    q = q_ref[0]
    q_g = [jnp.concatenate([q[:, (g * rep + r) * LANES:(g * rep + r + 1) * LANES] for r in range(rep)], axis=0)
           for g in range(groups)]

    def qk_dots(half, dst_ref):
        k0 = pl.multiple_of(jnp.minimum(half, last_half) * kh, kh)
        for g in range(groups):
            dst_ref[g] = _dot_nt(k_ref[0, pl.ds(k0, kh), g * LANES:(g + 1) * LANES], q_g[g])

    def softmax_pv(half, src_ref, carry):
        k0 = pl.multiple_of(half * kh, kh)
        x = skey_ref[pl.ds(k0, kh), :]
        kidx = k0 + lax.broadcasted_iota(jnp.int32, (kh, tq), 0)
        sel = (x > thr) | ((x == thr) & (kidx <= tie_last))
        sel = sel & (x != jnp.int32(INT_MIN))
        bias = jnp.where(sel, 0.0, MASK_VALUE)
        bias = jnp.concatenate([bias] * rep, axis=1)
        new = []
        for g in range(groups):
            m_old, l_old, acc_old = carry[g]
            vtg = vt_ref[g * LANES:(g + 1) * LANES, pl.ds(k0, kh)]
            sm = src_ref[g] * scale2 + bias
            m_new = jnp.maximum(m_old, jnp.max(sm, axis=0, keepdims=True))
            p = jnp.exp2(sm - m_new)
            alpha = jnp.exp2(m_old - m_new)
            l_new = alpha * l_old + jnp.sum(p, axis=0, keepdims=True)
            acc_new = alpha * acc_old + _dot(vtg, p.astype(BF16))
            new.append((m_new, l_new, acc_new))
        return tuple(new)

    def attn_chunk(c, carry):
        qk_dots(2 * c + 1, sb_ref)
        carry = softmax_pv(2 * c, sa_ref, carry)
        qk_dots(2 * c + 2, sa_ref)
        return softmax_pv(2 * c + 1, sb_ref, carry)

    init = tuple((jnp.full((1, rep * tq), MASK_VALUE, F32), jnp.zeros((1, rep * tq), F32),
                  jnp.zeros((LANES, rep * tq), F32)) for _ in range(groups))
    qk_dots(0, sa_ref)
    fin = lax.fori_loop(0, nchunk, attn_chunk, init)
    for g in range(groups):
        _, l_f, acc_f = fin[g]
        o_t = acc_f / l_f
        for r in range(rep):
            h = g * rep + r
            o_ref[0, :, h * LANES:(h + 1) * LANES] = o_t[:, r * tq:(r + 1) * tq].T.astype(o_ref.dtype)


def _dsa_attention(q, k, vt, qi, ki, wit, batch, seq, topk, tq=128, kc=512):
    nblk = seq // tq
    rep = ATT_HEADS // ATT_KV_HEADS
    r3 = lambda a: a.reshape(batch, seq, a.shape[-1])
    qblk = lambda w: pl.BlockSpec((1, tq, w), lambda b, i: (b, i, 0))
    full = lambda w: pl.BlockSpec((1, seq, w), lambda b, i: (b, 0, 0))
    out = pl.pallas_call(
        functools.partial(_attn_kernel, tq=tq, kc=kc, topk=topk, seq=seq),
        grid=(batch, nblk),
        in_specs=[qblk(IDX_Q_W), pl.BlockSpec((IDX_HEADS, tq), lambda b, i: (0, b * nblk + i)),
                  full(IDX_HEAD_DIM), qblk(ATT_Q_W), full(ATT_KV_W),
                  pl.BlockSpec((ATT_KV_W, seq), lambda b, i: (0, b))],
        out_specs=qblk(ATT_Q_W),
        out_shape=jax.ShapeDtypeStruct((batch, seq, ATT_Q_W), BF16),
        scratch_shapes=[pltpu.VMEM((seq, tq), jnp.int32),
                        pltpu.VMEM((kc // 2, IDX_HEADS * tq), F32), pltpu.VMEM((kc // 2, IDX_HEADS * tq), F32),
                        pltpu.VMEM((ATT_KV_HEADS, kc // 2, rep * tq), F32),
                        pltpu.VMEM((ATT_KV_HEADS, kc // 2, rep * tq), F32)],
        compiler_params=_cparams(("arbitrary", "arbitrary")),
        name="dsa_attention",
    )(r3(qi), wit, r3(ki), r3(q), r3(k), vt)
    return out.reshape(batch * seq, ATT_Q_W)


def _conv_kernel(cb_ref, cc_ref, cx_ref, hc_ref, hx_ref, w_ref, o_ref, *, tm, seq):
    i = pl.program_id(0)
    u = cc_ref[...] * cx_ref[...]
    halo = hc_ref[...] * hx_ref[...]
    halo = jnp.where((i * tm) % seq == 0, 0.0, halo)
    row = lax.broadcasted_iota(jnp.int32, u.shape, 0)
    u1 = pltpu.roll(u, 1, axis=0)
    u2 = pltpu.roll(u, 2, axis=0)
    h1 = halo[SUBLANES - 1:SUBLANES, :]
    h2 = halo[SUBLANES - 2:SUBLANES - 1, :]
    u1 = jnp.where(row == 0, h1, u1)
    u2 = jnp.where(row == 0, h2, jnp.where(row == 1, h1, u2))
    w = w_ref[...]
    conv = u2 * w[0:1, :] + u1 * w[1:2, :] + u * w[2:3, :]
    o_ref[...] = (cb_ref[...] * conv).astype(o_ref.dtype)


def _short_conv(proj, conv_w, seq, col0, tm=512, tc=512):
    m = proj.shape[0]
    nb = CONV_WIDTH // tc
    c0 = col0 // tc
    rpb = tm // SUBLANES
    main = lambda off: pl.BlockSpec((tm, tc), lambda i, j: (i, c0 + off * nb + j))
    halo = lambda off: pl.BlockSpec((SUBLANES, tc),
                                    lambda i, j: (jnp.maximum(i * rpb - 1, 0), c0 + off * nb + j))
    return pl.pallas_call(
        functools.partial(_conv_kernel, tm=tm, seq=seq),
        grid=(m // tm, nb),
        in_specs=[main(0), main(1), main(2), halo(1), halo(2),
                  pl.BlockSpec((CONV_K, tc), lambda i, j: (0, j))],
        out_specs=pl.BlockSpec((tm, tc), lambda i, j: (i, j)),
        out_shape=jax.ShapeDtypeStruct((m, CONV_WIDTH), BF16),
        compiler_params=_cparams(("arbitrary", "arbitrary")),
        name="short_conv",
    )(proj, proj, proj, proj, proj, conv_w)


def _hg_tables():
    c = HG_CHUNK
    t = np.arange(c)
    mats, masks = [], []
    for l in range(HG_LEVELS):
        h = 1 << l
        start = (t // (2 * h)) * (2 * h)
        p = start + h - 1
        right = (t - start) >= h
        u = t[None, :]
        r_m = right[:, None] & (u > p[:, None]) & (u <= t[:, None])
        l_m = (~right)[:, None] & (u > t[:, None]) & (u <= p[:, None])
        mats.append(r_m | l_m)
        same = (t[:, None] // (2 * h)) == (t[None, :] // (2 * h))
        masks.append(same & right[:, None] & (~right)[None, :])
    mats.append(t[None, :] <= t[:, None])
    mats.append(t[None, :] > t[:, None])
    mat = np.concatenate(mats, 0).astype(np.float32)
    return np.concatenate([mat] * HG_SPLIT, 1), np.stack(masks).astype(np.float32)


def _split3(x):
    hi = x.astype(BF16)
    r1 = x - hi.astype(F32)
    mid = r1.astype(BF16)
    lo = (r1 - mid.astype(F32)).astype(BF16)
    return jnp.concatenate([hi, mid, lo], axis=0)


def _hg_kernel(hq_ref, hf_ref, hi_ref, hg_ref, lb_ref, g_ref, mat_ref, mask_ref, o_ref, state_ref, *, ts):
    c = HG_CHUNK

    @pl.when(pl.program_id(1) == 0)
    def _():
        state_ref[...] = jnp.zeros_like(state_ref)

    g = g_ref[...]

    def head_chunk(h, rows):
        cols = slice(h * LANES, (h + 1) * LANES)
        lb = lb_ref[:, cols]
        f = lb + (1.0 - lb) * _sigmoid(hf_ref[rows, cols])
        logf = jnp.log(jnp.maximum(f, F_MIN))
        kin = 1.0 - f
        q = hq_ref[rows, cols]
        v = hi_ref[rows, cols]
        v16 = v.astype(BF16)
        e = jnp.exp(_dot(mat_ref[...], _split3(logf)))
        attn = jnp.zeros((c, c), F32)
        for l in range(HG_LEVELS):
            el = e[l * c:(l + 1) * c]
            attn = attn + _dot_nt((q * el).astype(BF16), (kin * el).astype(BF16)) * mask_ref[l]
        e_b = e[HG_LEVELS * c:(HG_LEVELS + 1) * c]
        e_k = e[(HG_LEVELS + 1) * c:(HG_LEVELS + 2) * c]
        state_t = state_ref[h]
        diag = jnp.sum(q * kin, axis=1, keepdims=True)
        out = (_dot(attn.astype(BF16), v16) + diag * v
               + _dot_nt((q * e_b).astype(BF16), state_t.astype(BF16)))
        kv_t = lax.dot_general(v16, (kin * e_k).astype(BF16), (((0,), (0,)), ((), ())),
                               preferred_element_type=F32)
        state_ref[h] = state_t * e_b[c - 1:c, :] + kv_t
        y = out * lax.rsqrt(jnp.mean(out * out, axis=-1, keepdims=True) + NORM_EPS) * g
        gate = hg_ref[rows, cols]
        o_ref[rows, cols] = (y * (gate * _sigmoid(gate))).astype(o_ref.dtype)

    def chunk(ci, carry):
        rows = pl.ds(pl.multiple_of(ci * c, c), c)
        for h in range(HG_HEADS):
            head_chunk(h, rows)
        return carry

    lax.fori_loop(0, ts // c, chunk, 0)


def _hgrn2(proj, lb, norm_g, batch, seq, col0, ts=512):
    m = proj.shape[0]
    c0 = col0 // HG_W
    spb = seq // ts
    mat, masks = _hg_tables()
    col = lambda off: pl.BlockSpec((ts, HG_W), lambda b, s: (b * spb + s, c0 + off))
    return pl.pallas_call(
        functools.partial(_hg_kernel, ts=ts),
        grid=(batch, spb),
        in_specs=[col(0), col(1), col(2), col(3),
                  pl.BlockSpec((1, HG_W), lambda b, s: (0, 0)),
                  pl.BlockSpec((1, LANES), lambda b, s: (0, 0)),
                  pl.BlockSpec(mat.shape, lambda b, s: (0, 0)),
                  pl.BlockSpec(masks.shape, lambda b, s: (0, 0, 0))],
        out_specs=pl.BlockSpec((ts, HG_W), lambda b, s: (b * spb + s, 0)),
        out_shape=jax.ShapeDtypeStruct((m, HG_W), BF16),
        scratch_shapes=[pltpu.VMEM((HG_HEADS, HG_VAL_DIM, HG_KEY_DIM), F32)],
        compiler_params=_cparams(("arbitrary", "arbitrary")),
        name="hgrn2",
    )(proj, proj, proj, proj, lb.reshape(1, HG_W), norm_g.reshape(1, HG_VAL_DIM),
      jnp.asarray(mat, BF16), jnp.asarray(masks))


def _lb_kernel(x_ref, o_ref):
    x = x_ref[...]
    e = jnp.exp(x - jnp.max(x, axis=0, keepdims=True))
    p = e / jnp.sum(e, axis=0, keepdims=True)
    depth = x.shape[0]
    run = jnp.zeros_like(p[0:1])
    for l in range(depth):
        run = run + p[l:l + 1]
        o_ref[l:l + 1, :] = run - p[0:1]


def _lower_bounds(hg_lower_bounds):
    return pl.pallas_call(
        _lb_kernel,
        out_shape=jax.ShapeDtypeStruct(hg_lower_bounds.shape, F32),
        name="hg_lower_bounds",
    )(hg_lower_bounds.astype(F32))


_COL_Q = 0
_COL_CONV = ATT_Q_W + 2 * ATT_KV_W + IDX_Q_W
_COL_HG = _COL_CONV + 3 * CONV_WIDTH
_COL_GATE = _COL_HG + 4 * HG_W


def _split_w_in(w_in, d_model):
    sizes = (ATT_Q_W, ATT_KV_W, ATT_KV_W, IDX_Q_W, IDX_HEAD_DIM, IDX_HEADS,
             CONV_WIDTH, CONV_WIDTH, CONV_WIDTH, HG_W, HG_W, HG_W, HG_W, N_BRANCHES * d_model)
    offs = np.cumsum((0,) + sizes)
    part = lambda n: w_in[:, offs[n]:offs[n + 1]]
    main = jnp.concatenate([part(0), part(1), part(2), part(3)] + [part(n) for n in range(6, 14)],
                           axis=1).astype(BF16)
    small = jnp.concatenate([part(4), part(5)], axis=1)
    small = jnp.pad(small, ((0, 0), (0, LANES - small.shape[1]))).astype(BF16)
    return main, small


def kernel(x, c, positions, ada_w, ada_b, norm_mix_g, w_in, conv_w, hg_lower_bounds, hg_norm_g,
           w_o_attn, w_o_conv, w_o_hgrn, w_out, norm_mlp_g, w_mlp1, w_mlp2, final_norm_g):
    batch, seq, d = x.shape
    depth = ada_w.shape[0]
    m = batch * seq
    topk = min(INDEX_TOPK, seq // 4)
    lb_all = _lower_bounds(hg_lower_bounds)
    mod = _modulation(c, ada_w, ada_b)
    x2 = x.reshape(m, d)
    for l in range(depth):
        sh1, sc1, g1, sh2, sc2, g2 = [mod[l, :, n * d:(n + 1) * d] for n in range(ADA_CHUNKS)]
        w_main, w_small = _split_w_in(w_in[l], d)
        proj, h = _norm_matmul(x2, norm_mix_g[l], sh1, sc1, w_main, seq, out_dtype=F32, emit_h=True, tn=768)
        small = _matmul(h, w_small, out_dtype=F32)
        q, k, vt, qi, ki, wit = _rope_split(proj, small, positions)
        att = _dsa_attention(q, k, vt, qi, ki, wit, batch, seq, topk)
        cv = _short_conv(proj, conv_w[l], seq, _COL_CONV)
        hg = _hgrn2(proj, lb_all[l], hg_norm_g[l], batch, seq, _COL_HG)
        x2 = _merge_out(att, cv, hg, w_o_attn[l].astype(BF16), w_o_conv[l].astype(BF16),
                        w_o_hgrn[l].astype(BF16), proj, _COL_GATE, w_out[l].astype(BF16), x2, g1, seq)
        a = _norm_matmul(x2, norm_mlp_g[l], sh2, sc2, w_mlp1[l].astype(BF16), seq, out_dtype=BF16, relu2=True)
        x2 = _matmul(a, w_mlp2[l].astype(BF16), out_dtype=F32, epilogue="residual", tk=2048,
                     res=x2, gate_row=g2, seq=seq)
    out = _norm(x2, final_norm_g, seq, out_dtype=x.dtype)
    return out.reshape(batch, seq, d)
```

```python
import functools

import numpy as np
import jax
import jax.numpy as jnp
from jax import lax
from jax.experimental import pallas as pl
from jax.experimental.pallas import tpu as pltpu

ATT_HEADS = 8
ATT_KV_HEADS = 2
ATT_HEAD_DIM = 128
IDX_HEADS = 8
IDX_HEAD_DIM = 64
INDEX_TOPK = 256
ROPE_THETA = 500000.0
ROPE_FRACTION = 4
MASK_VALUE = -1e30
CONV_WIDTH = 1024
CONV_K = 3
HG_HEADS = 8
HG_KEY_DIM = 128
HG_VAL_DIM = 128
F_MIN = 1e-30
N_BRANCHES = 3
NORM_EPS = 1e-6
ADA_CHUNKS = 6

ATT_Q_W = ATT_HEADS * ATT_HEAD_DIM
ATT_KV_W = ATT_KV_HEADS * ATT_HEAD_DIM
IDX_Q_W = IDX_HEADS * IDX_HEAD_DIM
HG_W = HG_HEADS * HG_KEY_DIM
ATT_LOGIT_SCALE = (ATT_HEAD_DIM ** -0.5) * float(np.log2(np.e))

LANES = 128
SUBLANES = 8
VMEM_LIMIT_BYTES = 56 * 1024 * 1024

INT_MIN = -2 ** 31
SEARCH_VARIANTS = 4
HG_CHUNK = 64
HG_LEVELS = 6
HG_SPLIT = 3

BF16 = jnp.bfloat16
F32 = jnp.float32


def _cparams(sem):
    return pltpu.CompilerParams(dimension_semantics=sem, vmem_limit_bytes=VMEM_LIMIT_BYTES)


def _dot(a, b):
    return jnp.dot(a, b, preferred_element_type=F32)


def _dot_nt(a, b):
    return lax.dot_general(a, b, (((1,), (1,)), ((), ())), preferred_element_type=F32)


def _sigmoid(x):
    return 1.0 / (1.0 + jnp.exp(-x))


def _mod_kernel(c_ref, w_ref, b_ref, o_ref):
    c = c_ref[...]
    ca = (c * _sigmoid(c)).astype(BF16)
    o_ref[...] = _dot(ca, w_ref[...].astype(BF16)) + b_ref[...]


def _modulation(c, ada_w, ada_b):
    depth, d, n = ada_w.shape
    b = c.shape[0]
    bp = -(-b // SUBLANES) * SUBLANES
    cp = jnp.pad(c, ((0, bp - b), (0, 0)))
    tn = 1024
    out = pl.pallas_call(
        _mod_kernel,
        grid=(depth, n // tn),
        in_specs=[
            pl.BlockSpec((bp, d), lambda l, j: (0, 0)),
            pl.BlockSpec((None, d, tn), lambda l, j: (l, 0, j)),
            pl.BlockSpec((None, 1, tn), lambda l, j: (l, 0, j)),
        ],
        out_specs=pl.BlockSpec((None, bp, tn), lambda l, j: (l, 0, j)),
        out_shape=jax.ShapeDtypeStruct((depth, bp, n), F32),
        compiler_params=_cparams(("arbitrary", "arbitrary")),
        name="adaln_mod",
    )(cp, ada_w, ada_b.reshape(depth, 1, n))
    return out[:, :b]


def _norm_kernel(x_ref, g_ref, *rest, modulate):
    if modulate:
        sh_ref, sc_ref, o_ref = rest
    else:
        (o_ref,) = rest
    x = x_ref[...]
    y = x * lax.rsqrt(jnp.mean(x * x, axis=-1, keepdims=True) + NORM_EPS) * g_ref[...]
    if modulate:
        y = y * (1.0 + sc_ref[0]) + sh_ref[0]
    o_ref[...] = y.astype(o_ref.dtype)


def _norm(x2, g, seq, shift=None, scale=None, out_dtype=BF16, tm=512):
    m, d = x2.shape
    modulate = shift is not None
    tpb = seq // tm
    in_specs = [pl.BlockSpec((tm, d), lambda i: (i, 0)),
                pl.BlockSpec((1, d), lambda i: (0, 0))]
    args = [x2, g.reshape(1, d)]
    if modulate:
        row = pl.BlockSpec((1, 1, d), lambda i: (i // tpb, 0, 0))
        in_specs += [row, row]
        args += [shift[:, None, :], scale[:, None, :]]
    return pl.pallas_call(
        functools.partial(_norm_kernel, modulate=modulate),
        grid=(m // tm,),
        in_specs=in_specs,
        out_specs=pl.BlockSpec((tm, d), lambda i: (i, 0)),
        out_shape=jax.ShapeDtypeStruct((m, d), out_dtype),
        compiler_params=_cparams(("arbitrary",)),
        name="rmsnorm",
    )(*args)


def _merge_kernel(att_ref, cv_ref, hg_ref, wa_ref, wc_ref, wh_ref, ga_ref, gc_ref, gh_ref,
                  wo_ref, x_ref, gr_ref, o_ref, acc_ref):
    j = pl.program_id(1)
    merged = (_sigmoid(ga_ref[...]) * _dot(att_ref[...], wa_ref[...])
              + _sigmoid(gc_ref[...]) * _dot(cv_ref[...], wc_ref[...])
              + _sigmoid(gh_ref[...]) * _dot(hg_ref[...], wh_ref[...]))
    part = _dot(merged.astype(BF16), wo_ref[...])

    @pl.when(j == 0)
    def _():
        acc_ref[...] = part

    @pl.when(j > 0)
    def _():
        acc_ref[...] += part

    @pl.when(j == pl.num_programs(1) - 1)
    def _():
        o_ref[...] = x_ref[...] + gr_ref[0] * acc_ref[...]


def _merge_out(att, cv, hg, wa, wc, wh, proj, gate_off, wout, x2, gate_row, seq, tm=512, tn=512):
    m, d = x2.shape
    tpb = seq // tm
    g0 = gate_off // tn
    nd = d // tn
    act = lambda a: pl.BlockSpec((tm, a.shape[1]), lambda i, j: (i, 0))
    wcol = lambda w: pl.BlockSpec((w.shape[0], tn), lambda i, j: (0, j))
    gate = lambda b: pl.BlockSpec((tm, tn), lambda i, j: (i, g0 + b * nd + j))
    return pl.pallas_call(
        _merge_kernel,
        grid=(m // tm, nd),
        in_specs=[act(att), act(cv), act(hg), wcol(wa), wcol(wc), wcol(wh), gate(0), gate(1), gate(2),
                  pl.BlockSpec((tn, d), lambda i, j: (j, 0)),
                  pl.BlockSpec((tm, d), lambda i, j: (i, 0)),
                  pl.BlockSpec((1, 1, d), lambda i, j: (i // tpb, 0, 0))],
        out_specs=pl.BlockSpec((tm, d), lambda i, j: (i, 0)),
        out_shape=jax.ShapeDtypeStruct((m, d), F32),
        scratch_shapes=[pltpu.VMEM((tm, d), F32)],
        compiler_params=_cparams(("arbitrary", "arbitrary")),
        name="merge_out",
    )(att, cv, hg, wa, wc, wh, proj, proj, proj, wout, x2, gate_row[:, None, :])


def _mm_kernel(*refs, epilogue, nk):
    a_ref, w_ref = refs[0], refs[1]
    pos = 2
    extra = []
    n_extra = {"none": 0, "relu2": 0, "residual": 2}[epilogue]
    for _ in range(n_extra):
        extra.append(refs[pos])
        pos += 1
    o_ref = refs[pos]
    acc_ref = refs[pos + 1] if nk > 1 else None

    def finish(y):
        if epilogue == "relu2":
            y = jnp.square(jnp.maximum(y, 0.0))
        elif epilogue == "residual":
            y = extra[0][...] + extra[1][0] * y
        o_ref[...] = y.astype(o_ref.dtype)

    if nk == 1:
        finish(_dot(a_ref[...], w_ref[...]))
    else:
        k = pl.program_id(2)

        @pl.when(k == 0)
        def _():
            acc_ref[...] = jnp.zeros_like(acc_ref)

        acc_ref[...] += _dot(a_ref[...], w_ref[...])

        @pl.when(k == nk - 1)
        def _():
            finish(acc_ref[...])


def _matmul(a, w, *, out_dtype, epilogue="none", tm=1024, tn=512, tk=None,
            res=None, gate_row=None, seq=None):
    m = a.shape[0]
    kdim, n = w.shape
    tk = kdim if tk is None else tk
    nk = kdim // tk
    tn = min(tn, n)
    in_specs = [pl.BlockSpec((tm, tk), lambda i, j, k: (i, k)),
                pl.BlockSpec((tk, tn), lambda i, j, k: (k, j))]
    args = [a, w]
    if epilogue == "residual":
        tpb = seq // tm
        in_specs += [pl.BlockSpec((tm, tn), lambda i, j, k: (i, j)),
                     pl.BlockSpec((1, 1, tn), lambda i, j, k: (i // tpb, 0, j))]
        args += [res, gate_row[:, None, :]]
    scratch = [pltpu.VMEM((tm, tn), F32)] if nk > 1 else []
    return pl.pallas_call(
        functools.partial(_mm_kernel, epilogue=epilogue, nk=nk),
        grid=(m // tm, n // tn, nk),
        in_specs=in_specs,
        out_specs=pl.BlockSpec((tm, tn), lambda i, j, k: (i, j)),
        out_shape=jax.ShapeDtypeStruct((m, n), out_dtype),
        scratch_shapes=scratch,
        compiler_params=_cparams(("arbitrary", "arbitrary", "arbitrary")),
        name="matmul_" + epilogue,
    )(*args)


def _rope_rows(period, half, rot):
    j = np.arange(LANES) % period
    inv = ROPE_THETA ** (-(np.arange(half, dtype=np.float32)) / np.float32(half))
    inv_row = np.where(j < rot, inv.astype(np.float32)[j % half], 0.0).astype(np.float32)
    sign_row = np.where(j < half, -1.0, np.where(j < rot, 1.0, 0.0)).astype(np.float32)
    first_row = (j < half).astype(np.float32)
    return np.stack([inv_row, sign_row, first_row])[:, None, :]


def _rope_apply(x, cos_t, sin_s, first, half):
    up = pltpu.roll(x, LANES - half, axis=1)
    dn = pltpu.roll(x, half, axis=1)
    sw = jnp.where(first > 0.5, up, dn)
    return x * cos_t + sw * sin_s


def _rope_kernel(pos_ref, rows_a_ref, rows_i_ref, q_ref, kv_ref, qi_ref, sm_ref,
                 qo_ref, ko_ref, vo_ref, qio_ref, kio_ref, wio_ref, *, w_scale):
    pos = pos_ref[...].astype(F32)
    ang_a = pos * rows_a_ref[0]
    cos_a, sin_a = jnp.cos(ang_a), jnp.sin(ang_a) * rows_a_ref[1]
    first_a = rows_a_ref[2]
    ang_i = pos * rows_i_ref[0]
    cos_i, sin_i = jnp.cos(ang_i), jnp.sin(ang_i) * rows_i_ref[1]
    first_i = rows_i_ref[2]
    half_a = ATT_HEAD_DIM // ROPE_FRACTION // 2
    half_i = IDX_HEAD_DIM // ROPE_FRACTION // 2
    for h in range(ATT_HEADS):
        sl = slice(h * LANES, (h + 1) * LANES)
        qo_ref[:, sl] = (_rope_apply(q_ref[:, sl], cos_a, sin_a, first_a, half_a) * ATT_LOGIT_SCALE).astype(BF16)
    for h in range(ATT_KV_HEADS):
        sl = slice(h * LANES, (h + 1) * LANES)
        ko_ref[:, sl] = _rope_apply(kv_ref[:, sl], cos_a, sin_a, first_a, half_a).astype(BF16)
    vo_ref[...] = kv_ref[:, ATT_KV_W:].T.astype(BF16)
    for h in range(IDX_Q_W // LANES):
        sl = slice(h * LANES, (h + 1) * LANES)
        qio_ref[:, sl] = _rope_apply(qi_ref[:, sl], cos_i, sin_i, first_i, half_i).astype(BF16)
    sm = sm_ref[...]
    ki = _rope_apply(sm, cos_i, sin_i, first_i, half_i)
    kio_ref[...] = ki[:, :IDX_HEAD_DIM].astype(BF16)
    wio_ref[...] = sm.T[IDX_HEAD_DIM:IDX_HEAD_DIM + IDX_HEADS, :] * w_scale


def _rope_split(proj, small, positions, tm=512):
    m = proj.shape[0]
    rows_a = jnp.asarray(_rope_rows(ATT_HEAD_DIM, ATT_HEAD_DIM // ROPE_FRACTION // 2,
                                    ATT_HEAD_DIM // ROPE_FRACTION))
    rows_i = jnp.asarray(_rope_rows(IDX_HEAD_DIM, IDX_HEAD_DIM // ROPE_FRACTION // 2,
                                    IDX_HEAD_DIM // ROPE_FRACTION))
    w_scale = (IDX_HEADS ** -0.5) * (IDX_HEAD_DIM ** -0.5)
    rows_spec = pl.BlockSpec((3, 1, LANES), lambda i: (0, 0, 0))
    outs = pl.pallas_call(
        functools.partial(_rope_kernel, w_scale=w_scale),
        grid=(m // tm,),
        in_specs=[
            pl.BlockSpec((tm, 1), lambda i: (i, 0)),
            rows_spec, rows_spec,
            pl.BlockSpec((tm, ATT_Q_W), lambda i: (i, 0)),
            pl.BlockSpec((tm, 2 * ATT_KV_W), lambda i: (i, ATT_Q_W // (2 * ATT_KV_W))),
            pl.BlockSpec((tm, IDX_Q_W), lambda i: (i, (ATT_Q_W + 2 * ATT_KV_W) // IDX_Q_W)),
            pl.BlockSpec((tm, LANES), lambda i: (i, 0)),
        ],
        out_specs=[
            pl.BlockSpec((tm, ATT_Q_W), lambda i: (i, 0)),
            pl.BlockSpec((tm, ATT_KV_W), lambda i: (i, 0)),
            pl.BlockSpec((ATT_KV_W, tm), lambda i: (0, i)),
            pl.BlockSpec((tm, IDX_Q_W), lambda i: (i, 0)),
            pl.BlockSpec((tm, IDX_HEAD_DIM), lambda i: (i, 0)),
            pl.BlockSpec((IDX_HEADS, tm), lambda i: (0, i)),
        ],
        out_shape=[
            jax.ShapeDtypeStruct((m, ATT_Q_W), BF16),
            jax.ShapeDtypeStruct((m, ATT_KV_W), BF16),
            jax.ShapeDtypeStruct((ATT_KV_W, m), BF16),
            jax.ShapeDtypeStruct((m, IDX_Q_W), BF16),
            jax.ShapeDtypeStruct((m, IDX_HEAD_DIM), BF16),
            jax.ShapeDtypeStruct((IDX_HEADS, m), F32),
        ],
        compiler_params=_cparams(("arbitrary",)),
        name="rope_split",
    )(positions.reshape(m, 1), rows_a, rows_i, proj, proj, proj, small)
    return outs


def _attn_kernel(qi_ref, wi_ref, ki_ref, q_ref, k_ref, vt_ref, o_ref, skey_ref,
                 da_ref, db_ref, sa_ref, sb_ref, *, tq, kc, topk, seq):
    i = pl.program_id(1)
    nchunk = lax.shift_right_logical((i + 1) * tq + (kc - 1), int(np.log2(kc)))
    qpos = i * tq + lax.broadcasted_iota(jnp.int32, (1, tq), 1)
    rb = 8 * SUBLANES

    qi = qi_ref[0]
    wi = wi_ref[...]
    qi_s = jnp.concatenate([qi[:, h * IDX_HEAD_DIM:(h + 1) * IDX_HEAD_DIM] for h in range(IDX_HEADS)],
                           axis=0)

    kh = kc // 2
    last_half = seq // kh - 1

    def score_dots(half, dst_ref):
        k0 = pl.multiple_of(jnp.minimum(half, last_half) * kh, kh)
        dst_ref[...] = _dot_nt(ki_ref[0, pl.ds(k0, kh), :], qi_s)

    def score_keys(half, src_ref):
        k0 = pl.multiple_of(half * kh, kh)
        for j in range(kh // rb):
            rows = slice(j * rb, (j + 1) * rb)
            acc = jnp.zeros((rb, tq), F32)
            for h in range(IDX_HEADS):
                acc = acc + jnp.maximum(src_ref[rows, h * tq:(h + 1) * tq], 0.0) * wi[h:h + 1, :]
            kpos = k0 + j * rb + lax.broadcasted_iota(jnp.int32, (rb, tq), 0)
            bits = pltpu.bitcast(acc, jnp.int32)
            skey = jnp.where(bits < 0, bits ^ jnp.int32(0x7FFFFFFF), bits)
            skey = jnp.where(kpos <= qpos, skey, jnp.int32(INT_MIN))
            skey_ref[pl.ds(k0 + j * rb, rb), :] = skey

    def score_chunk(c, carry):
        score_dots(2 * c + 1, db_ref)
        score_keys(2 * c, da_ref)
        score_dots(2 * c + 2, da_ref)
        score_keys(2 * c + 1, db_ref)
        return carry

    score_dots(0, da_ref)
    lax.fori_loop(0, nchunk, score_chunk, 0)

    nacc = 4

    def count_where(pred_fn):
        row_iota = lax.broadcasted_iota(jnp.int32, (SUBLANES, tq), 0)

        def chunk_body(c, parts):
            k0 = pl.multiple_of(c * kc, kc)
            parts = list(parts)
            xs = skey_ref[pl.ds(k0, kc), :]
            for j in range(kc // SUBLANES):
                x = xs[j * SUBLANES:(j + 1) * SUBLANES]
                parts[j % nacc] = parts[j % nacc] + jnp.where(pred_fn(x, k0 + j * SUBLANES + row_iota), 1, 0)
            return tuple(parts)
        zero = jnp.zeros((SUBLANES, tq), jnp.int32)
        parts = lax.fori_loop(0, nchunk, chunk_body, (zero,) * nacc)
        return jnp.sum(sum(parts[1:], parts[0]), axis=0, keepdims=True)

    nfull = seq // kc
    step = max(nfull // SEARCH_VARIANTS, 1)
    variant = lax.div(nchunk + (step - 1), step) - 1

    def fill_chunk(c, carry):
        rows = pl.ds(pl.multiple_of(c * kc, kc), kc)
        skey_ref[rows, :] = jnp.full((kc, tq), INT_MIN, jnp.int32)
        return carry

    lax.fori_loop(nchunk, (variant + 1) * step, fill_chunk, 0)

    def search_all(nc):
        def bit_body(b, prefix):
            cand = prefix | lax.shift_left(jnp.int32(1), 31 - b)
            cand_s = jnp.broadcast_to(cand ^ jnp.int32(INT_MIN), (SUBLANES, tq))
            parts = [jnp.zeros((SUBLANES, tq), jnp.int32)] * nacc
            for j in range(nc * kc // SUBLANES):
                x = skey_ref[j * SUBLANES:(j + 1) * SUBLANES, :]
                parts[j % nacc] = parts[j % nacc] + jnp.where(x >= cand_s, 1, 0)
            cnt = jnp.sum(sum(parts[1:], parts[0]), axis=0, keepdims=True)
            return jnp.where(cnt >= topk, cand, prefix)
        return lax.fori_loop(0, 32, bit_body, jnp.zeros((1, tq), jnp.int32))

    prefix = lax.switch(variant, [functools.partial(search_all, nc) for nc in range(step, nfull + 1, step)])
    thr = prefix ^ jnp.int32(INT_MIN)

    n_gt = count_where(lambda x, _: x > thr)
    n_eq = count_where(lambda x, _: x == thr)
    need = topk - n_gt
    excess = jnp.where((prefix != 0) & (n_eq > need), 1, 0)
    any_excess = jnp.max(excess) > 0
    idx_bits = int(np.log2(seq))

    def tie_search():
        def tie_bit(b, x):
            cand = x | lax.shift_left(jnp.int32(1), idx_bits - 1 - b)
            below = count_where(lambda xk, kidx: (xk == thr) & (kidx < cand))
            return jnp.where(below < need, cand, x)
        return lax.fori_loop(0, idx_bits, tie_bit, jnp.zeros((1, tq), jnp.int32))

    tie_last = lax.cond(any_excess, tie_search, lambda: jnp.full((1, tq), seq, jnp.int32))

    groups = ATT_KV_HEADS
    rep = ATT_HEADS // ATT_KV_HEADS
    q = q_ref[0]
    q_g = [jnp.concatenate([q[:, (g * rep + r) * LANES:(g * rep + r + 1) * LANES] for r in range(rep)], axis=0)
           for g in range(groups)]

    def qk_dots(half, dst_ref):
        k0 = pl.multiple_of(jnp.minimum(half, last_half) * kh, kh)
        for g in range(groups):
            dst_ref[g] = _dot_nt(k_ref[0, pl.ds(k0, kh), g * LANES:(g + 1) * LANES], q_g[g])

    def softmax_pv(half, src_ref, carry):
        k0 = pl.multiple_of(half * kh, kh)
        x = skey_ref[pl.ds(k0, kh), :]
        kidx = k0 + lax.broadcasted_iota(jnp.int32, (kh, tq), 0)
        sel = (x > thr) | ((x == thr) & (kidx <= tie_last))
        sel = sel & (x != jnp.int32(INT_MIN))
        bias = jnp.where(sel, 0.0, MASK_VALUE)
        bias = jnp.concatenate([bias] * rep, axis=1)
        new = []
        for g in range(groups):
            m_old, l_old, acc_old = carry[g]
            vtg = vt_ref[g * LANES:(g + 1) * LANES, pl.ds(k0, kh)]
            sm = src_ref[g] + bias
            m_new = jnp.maximum(m_old, jnp.max(sm, axis=0, keepdims=True))
            p = jnp.exp2(sm - m_new)
            alpha = jnp.exp2(m_old - m_new)
            l_new = alpha * l_old + jnp.sum(p, axis=0, keepdims=True)
            acc_new = alpha * acc_old + _dot(vtg, p.astype(BF16))
            new.append((m_new, l_new, acc_new))
        return tuple(new)

    def attn_chunk(c, carry):
        qk_dots(2 * c + 1, sb_ref)
        carry = softmax_pv(2 * c, sa_ref, carry)
        qk_dots(2 * c + 2, sa_ref)
        return softmax_pv(2 * c + 1, sb_ref, carry)

    init = tuple((jnp.full((1, rep * tq), MASK_VALUE, F32), jnp.zeros((1, rep * tq), F32),
                  jnp.zeros((LANES, rep * tq), F32)) for _ in range(groups))
    qk_dots(0, sa_ref)
    fin = lax.fori_loop(0, nchunk, attn_chunk, init)
    for g in range(groups):
        _, l_f, acc_f = fin[g]
        o_t = acc_f / l_f
        for r in range(rep):
            h = g * rep + r
            o_ref[0, :, h * LANES:(h + 1) * LANES] = o_t[:, r * tq:(r + 1) * tq].T.astype(o_ref.dtype)


def _dsa_attention(q, k, vt, qi, ki, wit, batch, seq, topk, tq=128, kc=512):
    nblk = seq // tq
    rep = ATT_HEADS // ATT_KV_HEADS
    r3 = lambda a: a.reshape(batch, seq, a.shape[-1])
    qblk = lambda w: pl.BlockSpec((1, tq, w), lambda b, i: (b, i, 0))
    full = lambda w: pl.BlockSpec((1, seq, w), lambda b, i: (b, 0, 0))
    out = pl.pallas_call(
        functools.partial(_attn_kernel, tq=tq, kc=kc, topk=topk, seq=seq),
        grid=(batch, nblk),
        in_specs=[qblk(IDX_Q_W), pl.BlockSpec((IDX_HEADS, tq), lambda b, i: (0, b * nblk + i)),
                  full(IDX_HEAD_DIM), qblk(ATT_Q_W), full(ATT_KV_W),
                  pl.BlockSpec((ATT_KV_W, seq), lambda b, i: (0, b))],
        out_specs=qblk(ATT_Q_W),
        out_shape=jax.ShapeDtypeStruct((batch, seq, ATT_Q_W), BF16),
        scratch_shapes=[pltpu.VMEM((seq, tq), jnp.int32),
                        pltpu.VMEM((kc // 2, IDX_HEADS * tq), F32), pltpu.VMEM((kc // 2, IDX_HEADS * tq), F32),
                        pltpu.VMEM((ATT_KV_HEADS, kc // 2, rep * tq), F32),
                        pltpu.VMEM((ATT_KV_HEADS, kc // 2, rep * tq), F32)],
        compiler_params=_cparams(("arbitrary", "arbitrary")),
        name="dsa_attention",
    )(r3(qi), wit, r3(ki), r3(q), r3(k), vt)
    return out.reshape(batch * seq, ATT_Q_W)


def _conv_kernel(cb_ref, cc_ref, cx_ref, hc_ref, hx_ref, w_ref, o_ref, *, tm, seq):
    i = pl.program_id(0)
    u = cc_ref[...] * cx_ref[...]
    halo = hc_ref[...] * hx_ref[...]
    halo = jnp.where((i * tm) % seq == 0, 0.0, halo)
    row = lax.broadcasted_iota(jnp.int32, u.shape, 0)
    u1 = pltpu.roll(u, 1, axis=0)
    u2 = pltpu.roll(u, 2, axis=0)
    h1 = halo[SUBLANES - 1:SUBLANES, :]
    h2 = halo[SUBLANES - 2:SUBLANES - 1, :]
    u1 = jnp.where(row == 0, h1, u1)
    u2 = jnp.where(row == 0, h2, jnp.where(row == 1, h1, u2))
    w = w_ref[...]
    conv = u2 * w[0:1, :] + u1 * w[1:2, :] + u * w[2:3, :]
    o_ref[...] = (cb_ref[...] * conv).astype(o_ref.dtype)


def _short_conv(proj, conv_w, seq, col0, tm=512, tc=512):
    m = proj.shape[0]
    nb = CONV_WIDTH // tc
    c0 = col0 // tc
    rpb = tm // SUBLANES
    main = lambda off: pl.BlockSpec((tm, tc), lambda i, j: (i, c0 + off * nb + j))
    halo = lambda off: pl.BlockSpec((SUBLANES, tc),
                                    lambda i, j: (jnp.maximum(i * rpb - 1, 0), c0 + off * nb + j))
    return pl.pallas_call(
        functools.partial(_conv_kernel, tm=tm, seq=seq),
        grid=(m // tm, nb),
        in_specs=[main(0), main(1), main(2), halo(1), halo(2),
                  pl.BlockSpec((CONV_K, tc), lambda i, j: (0, j))],
        out_specs=pl.BlockSpec((tm, tc), lambda i, j: (i, j)),
        out_shape=jax.ShapeDtypeStruct((m, CONV_WIDTH), BF16),
        compiler_params=_cparams(("arbitrary", "arbitrary")),
        name="short_conv",
    )(proj, proj, proj, proj, proj, conv_w)


def _hg_tables():
    c = HG_CHUNK
    t = np.arange(c)
    mats, masks = [], []
    for l in range(HG_LEVELS):
        h = 1 << l
        start = (t // (2 * h)) * (2 * h)
        p = start + h - 1
        right = (t - start) >= h
        u = t[None, :]
        r_m = right[:, None] & (u > p[:, None]) & (u <= t[:, None])
        l_m = (~right)[:, None] & (u > t[:, None]) & (u <= p[:, None])
        mats.append(r_m | l_m)
        same = (t[:, None] // (2 * h)) == (t[None, :] // (2 * h))
        masks.append(same & right[:, None] & (~right)[None, :])
    mats.append(t[None, :] <= t[:, None])
    mats.append(t[None, :] > t[:, None])
    mat = np.concatenate(mats, 0).astype(np.float32)
    return np.concatenate([mat] * HG_SPLIT, 1), np.stack(masks).astype(np.float32)


def _split3(x):
    hi = x.astype(BF16)
    r1 = x - hi.astype(F32)
    mid = r1.astype(BF16)
    lo = (r1 - mid.astype(F32)).astype(BF16)
    return jnp.concatenate([hi, mid, lo], axis=0)


def _hg_kernel(hq_ref, hf_ref, hi_ref, hg_ref, lb_ref, g_ref, mat_ref, mask_ref, o_ref, state_ref, *, ts):
    c = HG_CHUNK

    @pl.when(pl.program_id(1) == 0)
    def _():
        state_ref[...] = jnp.zeros_like(state_ref)

    g = g_ref[...]

    def head_chunk(h, rows, e_all):
        cols = slice(h * LANES, (h + 1) * LANES)
        lb = lb_ref[:, cols]
        kin = 1.0 - (lb + (1.0 - lb) * _sigmoid(hf_ref[rows, cols]))
        q = hq_ref[rows, cols]
        v = hi_ref[rows, cols]
        v16 = v.astype(BF16)
        e = e_all[:, cols]
        attn = jnp.zeros((c, c), F32)
        for l in range(HG_LEVELS):
            el = e[l * c:(l + 1) * c]
            attn = attn + _dot_nt((q * el).astype(BF16), (kin * el).astype(BF16)) * mask_ref[l]
        e_b = e[HG_LEVELS * c:(HG_LEVELS + 1) * c]
        e_k = e[(HG_LEVELS + 1) * c:(HG_LEVELS + 2) * c]
        state_t = state_ref[h]
        diag = jnp.sum(q * kin, axis=1, keepdims=True)
        out = (_dot(attn.astype(BF16), v16) + diag * v
               + _dot_nt((q * e_b).astype(BF16), state_t.astype(BF16)))
        kv_t = lax.dot_general(v16, (kin * e_k).astype(BF16), (((0,), (0,)), ((), ())),
                               preferred_element_type=F32)
        state_ref[h] = state_t * e_b[c - 1:c, :] + kv_t
        y = out * lax.rsqrt(jnp.mean(out * out, axis=-1, keepdims=True) + NORM_EPS) * g
        gate = hg_ref[rows, cols]
        o_ref[rows, cols] = (y * (gate * _sigmoid(gate))).astype(o_ref.dtype)

    def chunk(ci, carry):
        rows = pl.ds(pl.multiple_of(ci * c, c), c)
        lb = lb_ref[...]
        f = lb + (1.0 - lb) * _sigmoid(hf_ref[rows, :])
        logf = jnp.log(jnp.maximum(f, F_MIN))
        e_all = jnp.exp(_dot(mat_ref[...], _split3(logf)))
        for h in range(HG_HEADS):
            head_chunk(h, rows, e_all)
        return carry

    lax.fori_loop(0, ts // c, chunk, 0)


def _hgrn2(proj, lb, norm_g, batch, seq, col0, ts=512):
    m = proj.shape[0]
    c0 = col0 // HG_W
    spb = seq // ts
    mat, masks = _hg_tables()
    col = lambda off: pl.BlockSpec((ts, HG_W), lambda b, s: (b * spb + s, c0 + off))
    return pl.pallas_call(
        functools.partial(_hg_kernel, ts=ts),
        grid=(batch, spb),
        in_specs=[col(0), col(1), col(2), col(3),
                  pl.BlockSpec((1, HG_W), lambda b, s: (0, 0)),
                  pl.BlockSpec((1, LANES), lambda b, s: (0, 0)),
                  pl.BlockSpec(mat.shape, lambda b, s: (0, 0)),
                  pl.BlockSpec(masks.shape, lambda b, s: (0, 0, 0))],
        out_specs=pl.BlockSpec((ts, HG_W), lambda b, s: (b * spb + s, 0)),
        out_shape=jax.ShapeDtypeStruct((m, HG_W), BF16),
        scratch_shapes=[pltpu.VMEM((HG_HEADS, HG_VAL_DIM, HG_KEY_DIM), F32)],
        compiler_params=_cparams(("arbitrary", "arbitrary")),
        name="hgrn2",
    )(proj, proj, proj, proj, lb.reshape(1, HG_W), norm_g.reshape(1, HG_VAL_DIM),
      jnp.asarray(mat, BF16), jnp.asarray(masks))


def _lb_kernel(x_ref, o_ref):
    x = x_ref[...]
    e = jnp.exp(x - jnp.max(x, axis=0, keepdims=True))
    p = e / jnp.sum(e, axis=0, keepdims=True)
    depth = x.shape[0]
    run = jnp.zeros_like(p[0:1])
    for l in range(depth):
        run = run + p[l:l + 1]
        o_ref[l:l + 1, :] = run - p[0:1]


def _lower_bounds(hg_lower_bounds):
    return pl.pallas_call(
        _lb_kernel,
        out_shape=jax.ShapeDtypeStruct(hg_lower_bounds.shape, F32),
        name="hg_lower_bounds",
    )(hg_lower_bounds.astype(F32))


_COL_Q = 0
_COL_CONV = ATT_Q_W + 2 * ATT_KV_W + IDX_Q_W
_COL_HG = _COL_CONV + 3 * CONV_WIDTH
_COL_GATE = _COL_HG + 4 * HG_W


def _split_w_in(w_in, d_model):
    sizes = (ATT_Q_W, ATT_KV_W, ATT_KV_W, IDX_Q_W, IDX_HEAD_DIM, IDX_HEADS,
             CONV_WIDTH, CONV_WIDTH, CONV_WIDTH, HG_W, HG_W, HG_W, HG_W, N_BRANCHES * d_model)
    offs = np.cumsum((0,) + sizes)
    part = lambda n: w_in[:, offs[n]:offs[n + 1]]
    main = jnp.concatenate([part(0), part(1), part(2), part(3)] + [part(n) for n in range(6, 14)],
                           axis=1).astype(BF16)
    small = jnp.concatenate([part(4), part(5)], axis=1)
    small = jnp.pad(small, ((0, 0), (0, LANES - small.shape[1]))).astype(BF16)
    return main, small


def kernel(x, c, positions, ada_w, ada_b, norm_mix_g, w_in, conv_w, hg_lower_bounds, hg_norm_g,
           w_o_attn, w_o_conv, w_o_hgrn, w_out, norm_mlp_g, w_mlp1, w_mlp2, final_norm_g):
    batch, seq, d = x.shape
    depth = ada_w.shape[0]
    m = batch * seq
    topk = min(INDEX_TOPK, seq // 4)
    lb_all = _lower_bounds(hg_lower_bounds)
    mod = _modulation(c, ada_w, ada_b)
    x2 = x.reshape(m, d)
    for l in range(depth):
        sh1, sc1, g1, sh2, sc2, g2 = [mod[l, :, n * d:(n + 1) * d] for n in range(ADA_CHUNKS)]
        w_main, w_small = _split_w_in(w_in[l], d)
        h = _norm(x2, norm_mix_g[l], seq, sh1, sc1)
        proj = _matmul(h, w_main, out_dtype=F32, tn=768)
        small = _matmul(h, w_small, out_dtype=F32)
        q, k, vt, qi, ki, wit = _rope_split(proj, small, positions)
        att = _dsa_attention(q, k, vt, qi, ki, wit, batch, seq, topk)
        cv = _short_conv(proj, conv_w[l], seq, _COL_CONV)
        hg = _hgrn2(proj, lb_all[l], hg_norm_g[l], batch, seq, _COL_HG)
        x2 = _merge_out(att, cv, hg, w_o_attn[l].astype(BF16), w_o_conv[l].astype(BF16),
                        w_o_hgrn[l].astype(BF16), proj, _COL_GATE, w_out[l].astype(BF16), x2, g1, seq)
        h = _norm(x2, norm_mlp_g[l], seq, sh2, sc2)
        a = _matmul(h, w_mlp1[l].astype(BF16), out_dtype=BF16, epilogue="relu2")
        x2 = _matmul(a, w_mlp2[l].astype(BF16), out_dtype=F32, epilogue="residual", tm=512,
                     res=x2, gate_row=g2, seq=seq)
    out = _norm(x2, final_norm_g, seq, out_dtype=x.dtype)
    return out.reshape(batch, seq, d)
```

```python
import functools

import numpy as np
import jax
import jax.numpy as jnp
from jax import lax
from jax.experimental import pallas as pl
from jax.experimental.pallas import tpu as pltpu

ATT_HEADS = 8
ATT_KV_HEADS = 2
ATT_HEAD_DIM = 128
IDX_HEADS = 8
IDX_HEAD_DIM = 64
INDEX_TOPK = 256
ROPE_THETA = 500000.0
ROPE_FRACTION = 4
MASK_VALUE = -1e30
CONV_WIDTH = 1024
CONV_K = 3
HG_HEADS = 8
HG_KEY_DIM = 128
HG_VAL_DIM = 128
F_MIN = 1e-30
N_BRANCHES = 3
NORM_EPS = 1e-6
ADA_CHUNKS = 6

ATT_Q_W = ATT_HEADS * ATT_HEAD_DIM
ATT_KV_W = ATT_KV_HEADS * ATT_HEAD_DIM
IDX_Q_W = IDX_HEADS * IDX_HEAD_DIM
HG_W = HG_HEADS * HG_KEY_DIM
ATT_LOGIT_SCALE = (ATT_HEAD_DIM ** -0.5) * float(np.log2(np.e))

LANES = 128
SUBLANES = 8
HALO_ROWS = 16
VMEM_LIMIT_BYTES = 56 * 1024 * 1024

INT_MIN = -2 ** 31
SEARCH_VARIANTS = 8
HG_CHUNK = 64
HG_LEVELS = 6
HG_SPLIT = 3

BF16 = jnp.bfloat16
F32 = jnp.float32


def _cparams(sem):
    return pltpu.CompilerParams(dimension_semantics=sem, vmem_limit_bytes=VMEM_LIMIT_BYTES)


def _dot(a, b):
    return jnp.dot(a, b, preferred_element_type=F32)


def _dot_nt(a, b):
    return lax.dot_general(a, b, (((1,), (1,)), ((), ())), preferred_element_type=F32)


def _sigmoid(x):
    return 1.0 / (1.0 + jnp.exp(-x))


def _mod_kernel(c_ref, w_ref, b_ref, o_ref):
    c = c_ref[...]
    ca = (c * _sigmoid(c)).astype(BF16)
    o_ref[...] = _dot(ca, w_ref[...].astype(BF16)) + b_ref[...]


def _modulation(c, ada_w, ada_b):
    depth, d, n = ada_w.shape
    b = c.shape[0]
    bp = -(-b // SUBLANES) * SUBLANES
    cp = jnp.pad(c, ((0, bp - b), (0, 0)))
    tn = 1024
    out = pl.pallas_call(
        _mod_kernel,
        grid=(depth, n // tn),
        in_specs=[
            pl.BlockSpec((bp, d), lambda l, j: (0, 0)),
            pl.BlockSpec((None, d, tn), lambda l, j: (l, 0, j)),
            pl.BlockSpec((None, 1, tn), lambda l, j: (l, 0, j)),
        ],
        out_specs=pl.BlockSpec((None, bp, tn), lambda l, j: (l, 0, j)),
        out_shape=jax.ShapeDtypeStruct((depth, bp, n), F32),
        compiler_params=_cparams(("arbitrary", "arbitrary")),
        name="adaln_mod",
    )(cp, ada_w, ada_b.reshape(depth, 1, n))
    return out[:, :b]


def _norm_kernel(x_ref, g_ref, *rest, modulate):
    if modulate:
        sh_ref, sc_ref, o_ref = rest
    else:
        (o_ref,) = rest
    x = x_ref[...]
    y = x * lax.rsqrt(jnp.mean(x * x, axis=-1, keepdims=True) + NORM_EPS) * g_ref[...]
    if modulate:
        y = y * (1.0 + sc_ref[0]) + sh_ref[0]
    o_ref[...] = y.astype(o_ref.dtype)


def _norm(x2, g, seq, shift=None, scale=None, out_dtype=BF16, tm=512):
    m, d = x2.shape
    modulate = shift is not None
    tpb = seq // tm
    in_specs = [pl.BlockSpec((tm, d), lambda i: (i, 0)),
                pl.BlockSpec((1, d), lambda i: (0, 0))]
    args = [x2, g.reshape(1, d)]
    if modulate:
        row = pl.BlockSpec((1, 1, d), lambda i: (i // tpb, 0, 0))
        in_specs += [row, row]
        args += [shift[:, None, :], scale[:, None, :]]
    return pl.pallas_call(
        functools.partial(_norm_kernel, modulate=modulate),
        grid=(m // tm,),
        in_specs=in_specs,
        out_specs=pl.BlockSpec((tm, d), lambda i: (i, 0)),
        out_shape=jax.ShapeDtypeStruct((m, d), out_dtype),
        compiler_params=_cparams(("arbitrary",)),
        name="rmsnorm",
    )(*args)


def _merge_kernel(att_ref, cv_ref, hg_ref, wa_ref, wc_ref, wh_ref, ga_ref, gc_ref, gh_ref,
                  wo_ref, x_ref, gr_ref, o_ref, acc_ref):
    j = pl.program_id(1)
    merged = (_sigmoid(ga_ref[...].astype(F32)) * _dot(att_ref[...], wa_ref[...])
              + _sigmoid(gc_ref[...].astype(F32)) * _dot(cv_ref[...], wc_ref[...])
              + _sigmoid(gh_ref[...].astype(F32)) * _dot(hg_ref[...], wh_ref[...]))
    part = _dot(merged.astype(BF16), wo_ref[...])

    @pl.when(j == 0)
    def _():
        acc_ref[...] = part

    @pl.when(j > 0)
    def _():
        acc_ref[...] += part

    @pl.when(j == pl.num_programs(1) - 1)
    def _():
        o_ref[...] = x_ref[...] + gr_ref[0] * acc_ref[...]


def _merge_out(att, cv, hg, wa, wc, wh, proj, gate_off, wout, x2, gate_row, seq, tm=512, tn=512):
    m, d = x2.shape
    tpb = seq // tm
    g0 = gate_off // tn
    nd = d // tn
    act = lambda a: pl.BlockSpec((tm, a.shape[1]), lambda i, j: (i, 0))
    wcol = lambda w: pl.BlockSpec((w.shape[0], tn), lambda i, j: (0, j))
    gate = lambda b: pl.BlockSpec((tm, tn), lambda i, j: (i, g0 + b * nd + j))
    return pl.pallas_call(
        _merge_kernel,
        grid=(m // tm, nd),
        in_specs=[act(att), act(cv), act(hg), wcol(wa), wcol(wc), wcol(wh), gate(0), gate(1), gate(2),
                  pl.BlockSpec((tn, d), lambda i, j: (j, 0)),
                  pl.BlockSpec((tm, d), lambda i, j: (i, 0)),
                  pl.BlockSpec((1, 1, d), lambda i, j: (i // tpb, 0, 0))],
        out_specs=pl.BlockSpec((tm, d), lambda i, j: (i, 0)),
        out_shape=jax.ShapeDtypeStruct((m, d), F32),
        scratch_shapes=[pltpu.VMEM((tm, d), F32)],
        compiler_params=_cparams(("arbitrary", "arbitrary")),
        name="merge_out",
    )(att, cv, hg, wa, wc, wh, proj, proj, proj, wout, x2, gate_row[:, None, :])


def _mm_kernel(*refs, epilogue, nk):
    a_ref, w_ref = refs[0], refs[1]
    pos = 2
    extra = []
    n_extra = {"none": 0, "relu2": 0, "residual": 2}[epilogue]
    for _ in range(n_extra):
        extra.append(refs[pos])
        pos += 1
    o_ref = refs[pos]
    acc_ref = refs[pos + 1] if nk > 1 else None

    def finish(y):
        if epilogue == "relu2":
            y = jnp.square(jnp.maximum(y, 0.0))
        elif epilogue == "residual":
            y = extra[0][...] + extra[1][0] * y
        o_ref[...] = y.astype(o_ref.dtype)

    if nk == 1:
        finish(_dot(a_ref[...], w_ref[...]))
    else:
        k = pl.program_id(2)

        @pl.when(k == 0)
        def _():
            acc_ref[...] = jnp.zeros_like(acc_ref)

        acc_ref[...] += _dot(a_ref[...], w_ref[...])

        @pl.when(k == nk - 1)
        def _():
            finish(acc_ref[...])


def _matmul(a, w, *, out_dtype, epilogue="none", tm=1024, tn=512, tk=None,
            res=None, gate_row=None, seq=None):
    m = a.shape[0]
    kdim, n = w.shape
    tk = kdim if tk is None else tk
    nk = kdim // tk
    tn = min(tn, n)
    in_specs = [pl.BlockSpec((tm, tk), lambda i, j, k: (i, k)),
                pl.BlockSpec((tk, tn), lambda i, j, k: (k, j))]
    args = [a, w]
    if epilogue == "residual":
        tpb = seq // tm
        in_specs += [pl.BlockSpec((tm, tn), lambda i, j, k: (i, j)),
                     pl.BlockSpec((1, 1, tn), lambda i, j, k: (i // tpb, 0, j))]
        args += [res, gate_row[:, None, :]]
    scratch = [pltpu.VMEM((tm, tn), F32)] if nk > 1 else []
    return pl.pallas_call(
        functools.partial(_mm_kernel, epilogue=epilogue, nk=nk),
        grid=(m // tm, n // tn, nk),
        in_specs=in_specs,
        out_specs=pl.BlockSpec((tm, tn), lambda i, j, k: (i, j)),
        out_shape=jax.ShapeDtypeStruct((m, n), out_dtype),
        scratch_shapes=scratch,
        compiler_params=_cparams(("arbitrary", "arbitrary", "arbitrary")),
        name="matmul_" + epilogue,
    )(*args)


def _rope_rows(period, half, rot):
    j = np.arange(LANES) % period
    inv = ROPE_THETA ** (-(np.arange(half, dtype=np.float32)) / np.float32(half))
    inv_row = np.where(j < rot, inv.astype(np.float32)[j % half], 0.0).astype(np.float32)
    sign_row = np.where(j < half, -1.0, np.where(j < rot, 1.0, 0.0)).astype(np.float32)
    first_row = (j < half).astype(np.float32)
    return np.stack([inv_row, sign_row, first_row])[:, None, :]


def _rope_apply(x, cos_t, sin_s, first, half):
    up = pltpu.roll(x, LANES - half, axis=1)
    dn = pltpu.roll(x, half, axis=1)
    sw = jnp.where(first > 0.5, up, dn)
    return x * cos_t + sw * sin_s


def _rope_kernel(pos_ref, rows_a_ref, rows_i_ref, q_ref, kv_ref, qi_ref, sm_ref,
                 qo_ref, ko_ref, vo_ref, qio_ref, kio_ref, wio_ref, *, w_scale):
    pos = pos_ref[...].astype(F32)
    ang_a = pos * rows_a_ref[0]
    cos_a, sin_a = jnp.cos(ang_a), jnp.sin(ang_a) * rows_a_ref[1]
    first_a = rows_a_ref[2]
    ang_i = pos * rows_i_ref[0]
    cos_i, sin_i = jnp.cos(ang_i), jnp.sin(ang_i) * rows_i_ref[1]
    first_i = rows_i_ref[2]
    half_a = ATT_HEAD_DIM // ROPE_FRACTION // 2
    half_i = IDX_HEAD_DIM // ROPE_FRACTION // 2
    for h in range(ATT_HEADS):
        sl = slice(h * LANES, (h + 1) * LANES)
        qo_ref[:, sl] = (_rope_apply(q_ref[:, sl].astype(F32), cos_a, sin_a, first_a, half_a)
                         * ATT_LOGIT_SCALE).astype(BF16)
    for h in range(ATT_KV_HEADS):
        sl = slice(h * LANES, (h + 1) * LANES)
        ko_ref[:, sl] = _rope_apply(kv_ref[:, sl].astype(F32), cos_a, sin_a, first_a, half_a).astype(BF16)
    vo_ref[...] = kv_ref[:, ATT_KV_W:].astype(F32).T.astype(BF16)
    for h in range(IDX_Q_W // LANES):
        sl = slice(h * LANES, (h + 1) * LANES)
        qio_ref[:, sl] = _rope_apply(qi_ref[:, sl].astype(F32), cos_i, sin_i, first_i, half_i).astype(BF16)
    sm = sm_ref[...]
    ki = _rope_apply(sm, cos_i, sin_i, first_i, half_i)
    kio_ref[...] = ki[:, :IDX_HEAD_DIM].astype(BF16)
    wio_ref[...] = sm.T[IDX_HEAD_DIM:IDX_HEAD_DIM + IDX_HEADS, :] * w_scale


def _rope_split(proj, side, positions, tm=512):
    m = proj.shape[0]
    rows_a = jnp.asarray(_rope_rows(ATT_HEAD_DIM, ATT_HEAD_DIM // ROPE_FRACTION // 2,
                                    ATT_HEAD_DIM // ROPE_FRACTION))
    rows_i = jnp.asarray(_rope_rows(IDX_HEAD_DIM, IDX_HEAD_DIM // ROPE_FRACTION // 2,
                                    IDX_HEAD_DIM // ROPE_FRACTION))
    w_scale = (IDX_HEADS ** -0.5) * (IDX_HEAD_DIM ** -0.5)
    rows_spec = pl.BlockSpec((3, 1, LANES), lambda i: (0, 0, 0))
    outs = pl.pallas_call(
        functools.partial(_rope_kernel, w_scale=w_scale),
        grid=(m // tm,),
        in_specs=[
            pl.BlockSpec((tm, 1), lambda i: (i, 0)),
            rows_spec, rows_spec,
            pl.BlockSpec((tm, ATT_Q_W), lambda i: (i, 0)),
            pl.BlockSpec((tm, 2 * ATT_KV_W), lambda i: (i, ATT_Q_W // (2 * ATT_KV_W))),
            pl.BlockSpec((tm, IDX_Q_W), lambda i: (i, (ATT_Q_W + 2 * ATT_KV_W) // IDX_Q_W)),
            pl.BlockSpec((tm, LANES), lambda i: (i, _SIDE_SMALL // LANES)),
        ],
        out_specs=[
            pl.BlockSpec((tm, ATT_Q_W), lambda i: (i, 0)),
            pl.BlockSpec((tm, ATT_KV_W), lambda i: (i, 0)),
            pl.BlockSpec((ATT_KV_W, tm), lambda i: (0, i)),
            pl.BlockSpec((tm, IDX_Q_W), lambda i: (i, 0)),
            pl.BlockSpec((tm, IDX_HEAD_DIM), lambda i: (i, 0)),
            pl.BlockSpec((IDX_HEADS, tm), lambda i: (0, i)),
        ],
        out_shape=[
            jax.ShapeDtypeStruct((m, ATT_Q_W), BF16),
            jax.ShapeDtypeStruct((m, ATT_KV_W), BF16),
            jax.ShapeDtypeStruct((ATT_KV_W, m), BF16),
            jax.ShapeDtypeStruct((m, IDX_Q_W), BF16),
            jax.ShapeDtypeStruct((m, IDX_HEAD_DIM), BF16),
            jax.ShapeDtypeStruct((IDX_HEADS, m), F32),
        ],
        compiler_params=_cparams(("arbitrary",)),
        name="rope_split",
    )(positions.reshape(m, 1), rows_a, rows_i, proj, proj, proj, side)
    return outs


def _attn_kernel(qi_ref, wi_ref, ki_ref, q_ref, k_ref, vt_ref, o_ref, skey_ref,
                 da_ref, db_ref, sa_ref, sb_ref, *, tq, kc, topk, seq):
    i = pl.program_id(1)
    nchunk = lax.shift_right_logical((i + 1) * tq + (kc - 1), int(np.log2(kc)))
    qpos = i * tq + lax.broadcasted_iota(jnp.int32, (1, tq), 1)
    rb = 8 * SUBLANES

    qi = qi_ref[0]
    wi = wi_ref[...]
    qi_s = jnp.concatenate([qi[:, h * IDX_HEAD_DIM:(h + 1) * IDX_HEAD_DIM] for h in range(IDX_HEADS)],
                           axis=0)

    kh = kc // 2
    last_half = seq // kh - 1

    def score_dots(half, dst_ref):
        k0 = pl.multiple_of(jnp.minimum(half, last_half) * kh, kh)
        dst_ref[...] = _dot_nt(ki_ref[0, pl.ds(k0, kh), :], qi_s)

    def score_keys(half, src_ref):
        k0 = pl.multiple_of(half * kh, kh)
        for j in range(kh // rb):
            rows = slice(j * rb, (j + 1) * rb)
            acc = jnp.zeros((rb, tq), F32)
            for h in range(IDX_HEADS):
                acc = acc + jnp.maximum(src_ref[rows, h * tq:(h + 1) * tq], 0.0) * wi[h:h + 1, :]
            kpos = k0 + j * rb + lax.broadcasted_iota(jnp.int32, (rb, tq), 0)
            bits = pltpu.bitcast(acc, jnp.int32)
            skey = jnp.where(bits < 0, bits ^ jnp.int32(0x7FFFFFFF), bits)
            skey = jnp.where(kpos <= qpos, skey, jnp.int32(INT_MIN))
            skey_ref[pl.ds(k0 + j * rb, rb), :] = skey

    def score_chunk(c, carry):
        score_dots(2 * c + 1, db_ref)
        score_keys(2 * c, da_ref)
        score_dots(2 * c + 2, da_ref)
        score_keys(2 * c + 1, db_ref)
        return carry

    score_dots(0, da_ref)
    lax.fori_loop(0, nchunk, score_chunk, 0)

    nacc = 4

    def count_where(pred_fn):
        row_iota = lax.broadcasted_iota(jnp.int32, (SUBLANES, tq), 0)

        def chunk_body(c, parts):
            k0 = pl.multiple_of(c * kc, kc)
            parts = list(parts)
            xs = skey_ref[pl.ds(k0, kc), :]
            for j in range(kc // SUBLANES):
                x = xs[j * SUBLANES:(j + 1) * SUBLANES]
                parts[j % nacc] = parts[j % nacc] + jnp.where(pred_fn(x, k0 + j * SUBLANES + row_iota), 1, 0)
            return tuple(parts)
        zero = jnp.zeros((SUBLANES, tq), jnp.int32)
        parts = lax.fori_loop(0, nchunk, chunk_body, (zero,) * nacc)
        return jnp.sum(sum(parts[1:], parts[0]), axis=0, keepdims=True)

    nfull = seq // kc
    step = max(nfull // SEARCH_VARIANTS, 1)
    variant = lax.div(nchunk + (step - 1), step) - 1

    def fill_chunk(c, carry):
        rows = pl.ds(pl.multiple_of(c * kc, kc), kc)
        skey_ref[rows, :] = jnp.full((kc, tq), INT_MIN, jnp.int32)
        return carry

    lax.fori_loop(nchunk, (variant + 1) * step, fill_chunk, 0)

    def search_all(nc):
        def bit_body(b, prefix):
            cand = prefix | lax.shift_left(jnp.int32(1), 31 - b)
            cand_s = jnp.broadcast_to(cand ^ jnp.int32(INT_MIN), (SUBLANES, tq))
            parts = [jnp.zeros((SUBLANES, tq), jnp.int32)] * nacc
            for j in range(nc * kc // SUBLANES):
                x = skey_ref[j * SUBLANES:(j + 1) * SUBLANES, :]
                parts[j % nacc] = parts[j % nacc] + jnp.where(x >= cand_s, 1, 0)
            cnt = jnp.sum(sum(parts[1:], parts[0]), axis=0, keepdims=True)
            return jnp.where(cnt >= topk, cand, prefix)
        return lax.fori_loop(0, 32, bit_body, jnp.zeros((1, tq), jnp.int32))

    prefix = lax.switch(variant, [functools.partial(search_all, nc) for nc in range(step, nfull + 1, step)])
    thr = prefix ^ jnp.int32(INT_MIN)

    n_gt = count_where(lambda x, _: x > thr)
    n_eq = count_where(lambda x, _: x == thr)
    need = topk - n_gt
    excess = jnp.where((prefix != 0) & (n_eq > need), 1, 0)
    any_excess = jnp.max(excess) > 0
    idx_bits = int(np.log2(seq))

    def tie_search():
        def tie_bit(b, x):
            cand = x | lax.shift_left(jnp.int32(1), idx_bits - 1 - b)
            below = count_where(lambda xk, kidx: (xk == thr) & (kidx < cand))
            return jnp.where(below < need, cand, x)
        return lax.fori_loop(0, idx_bits, tie_bit, jnp.zeros((1, tq), jnp.int32))

    tie_last = lax.cond(any_excess, tie_search, lambda: jnp.full((1, tq), seq, jnp.int32))

    groups = ATT_KV_HEADS
    rep = ATT_HEADS // ATT_KV_HEADS
    q = q_ref[0]
    q_g = [jnp.concatenate([q[:, (g * rep + r) * LANES:(g * rep + r + 1) * LANES] for r in range(rep)], axis=0)
           for g in range(groups)]

    def qk_dots(half, dst_ref):
        k0 = pl.multiple_of(jnp.minimum(half, last_half) * kh, kh)
        for g in range(groups):
            dst_ref[g] = _dot_nt(k_ref[0, pl.ds(k0, kh), g * LANES:(g + 1) * LANES], q_g[g])

    def softmax_pv(half, src_ref, carry):
        k0 = pl.multiple_of(half * kh, kh)
        x = skey_ref[pl.ds(k0, kh), :]
        kidx = k0 + lax.broadcasted_iota(jnp.int32, (kh, tq), 0)
        sel = (x > thr) | ((x == thr) & (kidx <= tie_last))
        sel = sel & (x != jnp.int32(INT_MIN))
        bias = jnp.where(sel, 0.0, MASK_VALUE)
        bias = jnp.concatenate([bias] * rep, axis=1)
        new = []
        for g in range(groups):
            m_old, l_old, acc_old = carry[g]
            vtg = vt_ref[g * LANES:(g + 1) * LANES, pl.ds(k0, kh)]
            sm = src_ref[g] + bias
            m_new = jnp.maximum(m_old, jnp.max(sm, axis=0, keepdims=True))
            p = jnp.exp2(sm - m_new)
            alpha = jnp.exp2(m_old - m_new)
            l_new = alpha * l_old + jnp.sum(p, axis=0, keepdims=True)
            acc_new = alpha * acc_old + _dot(vtg, p.astype(BF16))
            new.append((m_new, l_new, acc_new))
        return tuple(new)

    def attn_chunk(c, carry):
        qk_dots(2 * c + 1, sb_ref)
        carry = softmax_pv(2 * c, sa_ref, carry)
        qk_dots(2 * c + 2, sa_ref)
        return softmax_pv(2 * c + 1, sb_ref, carry)

    init = tuple((jnp.full((1, rep * tq), MASK_VALUE, F32), jnp.zeros((1, rep * tq), F32),
                  jnp.zeros((LANES, rep * tq), F32)) for _ in range(groups))
    qk_dots(0, sa_ref)
    fin = lax.fori_loop(0, nchunk, attn_chunk, init)
    for g in range(groups):
        _, l_f, acc_f = fin[g]
        o_t = acc_f / l_f
        for r in range(rep):
            h = g * rep + r
            o_ref[0, :, h * LANES:(h + 1) * LANES] = o_t[:, r * tq:(r + 1) * tq].T.astype(o_ref.dtype)


def _dsa_attention(q, k, vt, qi, ki, wit, batch, seq, topk, tq=128, kc=512):
    nblk = seq // tq
    rep = ATT_HEADS // ATT_KV_HEADS
    r3 = lambda a: a.reshape(batch, seq, a.shape[-1])
    qblk = lambda w: pl.BlockSpec((1, tq, w), lambda b, i: (b, i, 0))
    full = lambda w: pl.BlockSpec((1, seq, w), lambda b, i: (b, 0, 0))
    out = pl.pallas_call(
        functools.partial(_attn_kernel, tq=tq, kc=kc, topk=topk, seq=seq),
        grid=(batch, nblk),
        in_specs=[qblk(IDX_Q_W), pl.BlockSpec((IDX_HEADS, tq), lambda b, i: (0, b * nblk + i)),
                  full(IDX_HEAD_DIM), qblk(ATT_Q_W), full(ATT_KV_W),
                  pl.BlockSpec((ATT_KV_W, seq), lambda b, i: (0, b))],
        out_specs=qblk(ATT_Q_W),
        out_shape=jax.ShapeDtypeStruct((batch, seq, ATT_Q_W), BF16),
        scratch_shapes=[pltpu.VMEM((seq, tq), jnp.int32),
                        pltpu.VMEM((kc // 2, IDX_HEADS * tq), F32), pltpu.VMEM((kc // 2, IDX_HEADS * tq), F32),
                        pltpu.VMEM((ATT_KV_HEADS, kc // 2, rep * tq), F32),
                        pltpu.VMEM((ATT_KV_HEADS, kc // 2, rep * tq), F32)],
        compiler_params=_cparams(("arbitrary", "arbitrary")),
        name="dsa_attention",
    )(r3(qi), wit, r3(ki), r3(q), r3(k), vt)
    return out.reshape(batch * seq, ATT_Q_W)


def _conv_kernel(cb_ref, cc_ref, cx_ref, hc_ref, hx_ref, w_ref, o_ref, *, tm, seq):
    i = pl.program_id(0)
    u = cc_ref[...].astype(F32) * cx_ref[...].astype(F32)
    halo = hc_ref[...].astype(F32) * hx_ref[...].astype(F32)
    halo = jnp.where((i * tm) % seq == 0, 0.0, halo)
    row = lax.broadcasted_iota(jnp.int32, u.shape, 0)
    u1 = pltpu.roll(u, 1, axis=0)
    u2 = pltpu.roll(u, 2, axis=0)
    h1 = halo[HALO_ROWS - 1:HALO_ROWS, :]
    h2 = halo[HALO_ROWS - 2:HALO_ROWS - 1, :]
    u1 = jnp.where(row == 0, h1, u1)
    u2 = jnp.where(row == 0, h2, jnp.where(row == 1, h1, u2))
    w = w_ref[...]
    conv = u2 * w[0:1, :] + u1 * w[1:2, :] + u * w[2:3, :]
    o_ref[...] = (cb_ref[...].astype(F32) * conv).astype(o_ref.dtype)


def _short_conv(proj, conv_w, seq, col0, tm=512, tc=512):
    m = proj.shape[0]
    nb = CONV_WIDTH // tc
    c0 = col0 // tc
    rpb = tm // HALO_ROWS
    main = lambda off: pl.BlockSpec((tm, tc), lambda i, j: (i, c0 + off * nb + j))
    halo = lambda off: pl.BlockSpec((HALO_ROWS, tc),
                                    lambda i, j: (jnp.maximum(i * rpb - 1, 0), c0 + off * nb + j))
    return pl.pallas_call(
        functools.partial(_conv_kernel, tm=tm, seq=seq),
        grid=(m // tm, nb),
        in_specs=[main(0), main(1), main(2), halo(1), halo(2),
                  pl.BlockSpec((CONV_K, tc), lambda i, j: (0, j))],
        out_specs=pl.BlockSpec((tm, tc), lambda i, j: (i, j)),
        out_shape=jax.ShapeDtypeStruct((m, CONV_WIDTH), BF16),
        compiler_params=_cparams(("arbitrary", "arbitrary")),
        name="short_conv",
    )(proj, proj, proj, proj, proj, conv_w)


def _hg_tables():
    c = HG_CHUNK
    t = np.arange(c)
    mats, masks = [], []
    for l in range(HG_LEVELS):
        h = 1 << l
        start = (t // (2 * h)) * (2 * h)
        p = start + h - 1
        right = (t - start) >= h
        u = t[None, :]
        r_m = right[:, None] & (u > p[:, None]) & (u <= t[:, None])
        l_m = (~right)[:, None] & (u > t[:, None]) & (u <= p[:, None])
        mats.append(r_m | l_m)
        same = (t[:, None] // (2 * h)) == (t[None, :] // (2 * h))
        masks.append(same & right[:, None] & (~right)[None, :])
    mats.append(t[None, :] <= t[:, None])
    mats.append(t[None, :] > t[:, None])
    mat = np.concatenate(mats, 0).astype(np.float32)
    return np.concatenate([mat] * HG_SPLIT, 1), np.stack(masks).astype(np.float32)


def _split3(x):
    hi = x.astype(BF16)
    r1 = x - hi.astype(F32)
    mid = r1.astype(BF16)
    lo = (r1 - mid.astype(F32)).astype(BF16)
    return jnp.concatenate([hi, mid, lo], axis=0)


def _hg_kernel(hq_ref, hf_ref, hi_ref, hg_ref, lb_ref, g_ref, mat_ref, mask_ref, o_ref, state_ref, *, ts):
    c = HG_CHUNK

    @pl.when(pl.program_id(1) == 0)
    def _():
        state_ref[...] = jnp.zeros_like(state_ref)

    g = g_ref[...]

    def head_chunk(h, rows, e_all):
        cols = slice(h * LANES, (h + 1) * LANES)
        lb = lb_ref[:, cols]
        kin = 1.0 - (lb + (1.0 - lb) * _sigmoid(hf_ref[rows, cols]))
        q = hq_ref[rows, cols].astype(F32)
        v16 = hi_ref[rows, cols]
        v = v16.astype(F32)
        e = e_all[:, cols]
        attn = jnp.zeros((c, c), F32)
        for l in range(HG_LEVELS):
            el = e[l * c:(l + 1) * c]
            attn = attn + _dot_nt((q * el).astype(BF16), (kin * el).astype(BF16)) * mask_ref[l]
        e_b = e[HG_LEVELS * c:(HG_LEVELS + 1) * c]
        e_k = e[(HG_LEVELS + 1) * c:(HG_LEVELS + 2) * c]
        state_t = state_ref[h]
        diag = jnp.sum(q * kin, axis=1, keepdims=True)
        out = (_dot(attn.astype(BF16), v16) + diag * v
               + _dot_nt((q * e_b).astype(BF16), state_t.astype(BF16)))
        kv_t = lax.dot_general(v16, (kin * e_k).astype(BF16), (((0,), (0,)), ((), ())),
                               preferred_element_type=F32)
        state_ref[h] = state_t * e_b[c - 1:c, :] + kv_t
        y = out * lax.rsqrt(jnp.mean(out * out, axis=-1, keepdims=True) + NORM_EPS) * g
        gate = hg_ref[rows, cols].astype(F32)
        o_ref[rows, cols] = (y * (gate * _sigmoid(gate))).astype(o_ref.dtype)

    def chunk(ci, carry):
        rows = pl.ds(pl.multiple_of(ci * c, c), c)
        lb = lb_ref[...]
        f = lb + (1.0 - lb) * _sigmoid(hf_ref[rows, :])
        logf = jnp.log(jnp.maximum(f, F_MIN))
        e_all = jnp.exp(_dot(mat_ref[...], _split3(logf)))
        for h in range(HG_HEADS):
            head_chunk(h, rows, e_all)
        return carry

    lax.fori_loop(0, ts // c, chunk, 0)


def _hgrn2(proj, side, lb, norm_g, batch, seq, col0, ts=512):
    m = proj.shape[0]
    c0 = col0 // HG_W
    spb = seq // ts
    mat, masks = _hg_tables()
    col = lambda off: pl.BlockSpec((ts, HG_W), lambda b, s: (b * spb + s, c0 + off))
    return pl.pallas_call(
        functools.partial(_hg_kernel, ts=ts),
        grid=(batch, spb),
        in_specs=[col(0), pl.BlockSpec((ts, HG_W), lambda b, s: (b * spb + s, 0)), col(1), col(2),
                  pl.BlockSpec((1, HG_W), lambda b, s: (0, 0)),
                  pl.BlockSpec((1, LANES), lambda b, s: (0, 0)),
                  pl.BlockSpec(mat.shape, lambda b, s: (0, 0)),
                  pl.BlockSpec(masks.shape, lambda b, s: (0, 0, 0))],
        out_specs=pl.BlockSpec((ts, HG_W), lambda b, s: (b * spb + s, 0)),
        out_shape=jax.ShapeDtypeStruct((m, HG_W), BF16),
        scratch_shapes=[pltpu.VMEM((HG_HEADS, HG_VAL_DIM, HG_KEY_DIM), F32)],
        compiler_params=_cparams(("arbitrary", "arbitrary")),
        name="hgrn2",
    )(proj, side, proj, proj, lb.reshape(1, HG_W), norm_g.reshape(1, HG_VAL_DIM),
      jnp.asarray(mat, BF16), jnp.asarray(masks))


def _lb_kernel(x_ref, o_ref):
    x = x_ref[...]
    e = jnp.exp(x - jnp.max(x, axis=0, keepdims=True))
    p = e / jnp.sum(e, axis=0, keepdims=True)
    depth = x.shape[0]
    run = jnp.zeros_like(p[0:1])
    for l in range(depth):
        run = run + p[l:l + 1]
        o_ref[l:l + 1, :] = run - p[0:1]


def _lower_bounds(hg_lower_bounds):
    return pl.pallas_call(
        _lb_kernel,
        out_shape=jax.ShapeDtypeStruct(hg_lower_bounds.shape, F32),
        name="hg_lower_bounds",
    )(hg_lower_bounds.astype(F32))


_COL_Q = 0
_COL_CONV = ATT_Q_W + 2 * ATT_KV_W + IDX_Q_W
_COL_HG = _COL_CONV + 3 * CONV_WIDTH
_COL_GATE = _COL_HG + 3 * HG_W
_SIDE_SMALL = HG_W
_SIDE_W = HG_W + LANES


def _split_w_in(w_in, d_model):
    sizes = (ATT_Q_W, ATT_KV_W, ATT_KV_W, IDX_Q_W, IDX_HEAD_DIM, IDX_HEADS,
             CONV_WIDTH, CONV_WIDTH, CONV_WIDTH, HG_W, HG_W, HG_W, HG_W, N_BRANCHES * d_model)
    offs = np.cumsum((0,) + sizes)
    part = lambda n: w_in[:, offs[n]:offs[n + 1]]
    main = jnp.concatenate([part(n) for n in (0, 1, 2, 3, 6, 7, 8, 9, 11, 12, 13)], axis=1).astype(BF16)
    side = jnp.concatenate([part(10), part(4), part(5)], axis=1)
    side = jnp.pad(side, ((0, 0), (0, _SIDE_W - side.shape[1]))).astype(BF16)
    return main, side


def kernel(x, c, positions, ada_w, ada_b, norm_mix_g, w_in, conv_w, hg_lower_bounds, hg_norm_g,
           w_o_attn, w_o_conv, w_o_hgrn, w_out, norm_mlp_g, w_mlp1, w_mlp2, final_norm_g):
    batch, seq, d = x.shape
    depth = ada_w.shape[0]
    m = batch * seq
    topk = min(INDEX_TOPK, seq // 4)
    lb_all = _lower_bounds(hg_lower_bounds)
    mod = _modulation(c, ada_w, ada_b)
    x2 = x.reshape(m, d)
    for l in range(depth):
        sh1, sc1, g1, sh2, sc2, g2 = [mod[l, :, n * d:(n + 1) * d] for n in range(ADA_CHUNKS)]
        w_main, w_side = _split_w_in(w_in[l], d)
        h = _norm(x2, norm_mix_g[l], seq, sh1, sc1)
        proj = _matmul(h, w_main, out_dtype=BF16, tn=1024)
        side = _matmul(h, w_side, out_dtype=F32, tn=_SIDE_W)
        q, k, vt, qi, ki, wit = _rope_split(proj, side, positions)
        att = _dsa_attention(q, k, vt, qi, ki, wit, batch, seq, topk)
        cv = _short_conv(proj, conv_w[l], seq, _COL_CONV)
        hg = _hgrn2(proj, side, lb_all[l], hg_norm_g[l], batch, seq, _COL_HG)
        x2 = _merge_out(att, cv, hg, w_o_attn[l].astype(BF16), w_o_conv[l].astype(BF16),
                        w_o_hgrn[l].astype(BF16), proj, _COL_GATE, w_out[l].astype(BF16), x2, g1, seq)
        h = _norm(x2, norm_mlp_g[l], seq, sh2, sc2)
        a = _matmul(h, w_mlp1[l].astype(BF16), out_dtype=BF16, epilogue="relu2")
        x2 = _matmul(a, w_mlp2[l].astype(BF16), out_dtype=F32, epilogue="residual", tm=512,
                     res=x2, gate_row=g2, seq=seq)
    out = _norm(x2, final_norm_g, seq, out_dtype=x.dtype)
    return out.reshape(batch, seq, d)
```

```python
import functools

import numpy as np
import jax
import jax.numpy as jnp
from jax import lax
from jax.experimental import pallas as pl
from jax.experimental.pallas import tpu as pltpu

ATT_HEADS = 8
ATT_KV_HEADS = 2
ATT_HEAD_DIM = 128
IDX_HEADS = 8
IDX_HEAD_DIM = 64
INDEX_TOPK = 256
ROPE_THETA = 500000.0
ROPE_FRACTION = 4
MASK_VALUE = -1e30
CONV_WIDTH = 1024
CONV_K = 3
HG_HEADS = 8
HG_KEY_DIM = 128
HG_VAL_DIM = 128
F_MIN = 1e-30
N_BRANCHES = 3
NORM_EPS = 1e-6
ADA_CHUNKS = 6

ATT_Q_W = ATT_HEADS * ATT_HEAD_DIM
ATT_KV_W = ATT_KV_HEADS * ATT_HEAD_DIM
IDX_Q_W = IDX_HEADS * IDX_HEAD_DIM
HG_W = HG_HEADS * HG_KEY_DIM
ATT_LOGIT_SCALE = (ATT_HEAD_DIM ** -0.5) * float(np.log2(np.e))

LANES = 128
SUBLANES = 8
HALO_ROWS = 16
VMEM_LIMIT_BYTES = 56 * 1024 * 1024

INT_MIN = -2 ** 31
SEARCH_VARIANTS = 8
HG_CHUNK = 64
HG_LEVELS = 6
HG_SPLIT = 3

BF16 = jnp.bfloat16
F32 = jnp.float32


def _cparams(sem):
    return pltpu.CompilerParams(dimension_semantics=sem, vmem_limit_bytes=VMEM_LIMIT_BYTES)


def _dot(a, b):
    return jnp.dot(a, b, preferred_element_type=F32)


def _dot_nt(a, b):
    return lax.dot_general(a, b, (((1,), (1,)), ((), ())), preferred_element_type=F32)


def _sigmoid(x):
    return 1.0 / (1.0 + jnp.exp(-x))


def _mod_kernel(c_ref, w_ref, b_ref, o_ref):
    c = c_ref[...]
    ca = (c * _sigmoid(c)).astype(BF16)
    o_ref[...] = _dot(ca, w_ref[...].astype(BF16)) + b_ref[...]


def _modulation(c, ada_w, ada_b):
    depth, d, n = ada_w.shape
    b = c.shape[0]
    bp = -(-b // SUBLANES) * SUBLANES
    cp = jnp.pad(c, ((0, bp - b), (0, 0)))
    tn = 1024
    out = pl.pallas_call(
        _mod_kernel,
        grid=(depth, n // tn),
        in_specs=[
            pl.BlockSpec((bp, d), lambda l, j: (0, 0)),
            pl.BlockSpec((None, d, tn), lambda l, j: (l, 0, j)),
            pl.BlockSpec((None, 1, tn), lambda l, j: (l, 0, j)),
        ],
        out_specs=pl.BlockSpec((None, bp, tn), lambda l, j: (l, 0, j)),
        out_shape=jax.ShapeDtypeStruct((depth, bp, n), F32),
        compiler_params=_cparams(("arbitrary", "arbitrary")),
        name="adaln_mod",
    )(cp, ada_w, ada_b.reshape(depth, 1, n))
    return out[:, :b]


def _norm_kernel(x_ref, g_ref, *rest, modulate):
    if modulate:
        sh_ref, sc_ref, o_ref = rest
    else:
        (o_ref,) = rest
    x = x_ref[...]
    y = x * lax.rsqrt(jnp.mean(x * x, axis=-1, keepdims=True) + NORM_EPS) * g_ref[...]
    if modulate:
        y = y * (1.0 + sc_ref[0]) + sh_ref[0]
    o_ref[...] = y.astype(o_ref.dtype)


def _norm(x2, g, seq, shift=None, scale=None, out_dtype=BF16, tm=512):
    m, d = x2.shape
    modulate = shift is not None
    tpb = seq // tm
    in_specs = [pl.BlockSpec((tm, d), lambda i: (i, 0)),
                pl.BlockSpec((1, d), lambda i: (0, 0))]
    args = [x2, g.reshape(1, d)]
    if modulate:
        row = pl.BlockSpec((1, 1, d), lambda i: (i // tpb, 0, 0))
        in_specs += [row, row]
        args += [shift[:, None, :], scale[:, None, :]]
    return pl.pallas_call(
        functools.partial(_norm_kernel, modulate=modulate),
        grid=(m // tm,),
        in_specs=in_specs,
        out_specs=pl.BlockSpec((tm, d), lambda i: (i, 0)),
        out_shape=jax.ShapeDtypeStruct((m, d), out_dtype),
        compiler_params=_cparams(("arbitrary",)),
        name="rmsnorm",
    )(*args)


def _merge_kernel(att_ref, cv_ref, hg_ref, wa_ref, wc_ref, wh_ref, ga_ref, gc_ref, gh_ref,
                  wo_ref, x_ref, gr_ref, ng_ref, sh_ref, sc_ref, o_ref, h_ref, acc_ref):
    j = pl.program_id(1)
    merged = (_sigmoid(ga_ref[...].astype(F32)) * _dot(att_ref[...], wa_ref[...])
              + _sigmoid(gc_ref[...].astype(F32)) * _dot(cv_ref[...], wc_ref[...])
              + _sigmoid(gh_ref[...].astype(F32)) * _dot(hg_ref[...], wh_ref[...]))
    part = _dot(merged.astype(BF16), wo_ref[...])

    @pl.when(j == 0)
    def _():
        acc_ref[...] = part

    @pl.when(j > 0)
    def _():
        acc_ref[...] += part

    @pl.when(j == pl.num_programs(1) - 1)
    def _():
        y = x_ref[...] + gr_ref[0] * acc_ref[...]
        o_ref[...] = y
        yn = y * lax.rsqrt(jnp.mean(y * y, axis=-1, keepdims=True) + NORM_EPS) * ng_ref[...]
        h_ref[...] = (yn * (1.0 + sc_ref[0]) + sh_ref[0]).astype(h_ref.dtype)


def _merge_out(att, cv, hg, wa, wc, wh, proj, gate_off, wout, x2, gate_row, norm_g, shift, scale, seq,
               tm=512, tn=512):
    m, d = x2.shape
    tpb = seq // tm
    g0 = gate_off // tn
    nd = d // tn
    act = lambda a: pl.BlockSpec((tm, a.shape[1]), lambda i, j: (i, 0))
    wcol = lambda w: pl.BlockSpec((w.shape[0], tn), lambda i, j: (0, j))
    gate = lambda b: pl.BlockSpec((tm, tn), lambda i, j: (i, g0 + b * nd + j))
    row = pl.BlockSpec((1, 1, d), lambda i, j: (i // tpb, 0, 0))
    full = pl.BlockSpec((tm, d), lambda i, j: (i, 0))
    return pl.pallas_call(
        _merge_kernel,
        grid=(m // tm, nd),
        in_specs=[act(att), act(cv), act(hg), wcol(wa), wcol(wc), wcol(wh), gate(0), gate(1), gate(2),
                  pl.BlockSpec((tn, d), lambda i, j: (j, 0)), full, row,
                  pl.BlockSpec((1, d), lambda i, j: (0, 0)), row, row],
        out_specs=[full, full],
        out_shape=[jax.ShapeDtypeStruct((m, d), F32), jax.ShapeDtypeStruct((m, d), BF16)],
        scratch_shapes=[pltpu.VMEM((tm, d), F32)],
        compiler_params=_cparams(("arbitrary", "arbitrary")),
        name="merge_out",
    )(att, cv, hg, wa, wc, wh, proj, proj, proj, wout, x2, gate_row[:, None, :],
      norm_g.reshape(1, d), shift[:, None, :], scale[:, None, :])


def _mm_kernel(*refs, epilogue, nk):
    a_ref, w_ref = refs[0], refs[1]
    pos = 2
    extra = []
    n_extra = {"none": 0, "relu2": 0, "residual": 2}[epilogue]
    for _ in range(n_extra):
        extra.append(refs[pos])
        pos += 1
    o_ref = refs[pos]
    acc_ref = refs[pos + 1] if nk > 1 else None

    def finish(y):
        if epilogue == "relu2":
            y = jnp.square(jnp.maximum(y, 0.0))
        elif epilogue == "residual":
            y = extra[0][...] + extra[1][0] * y
        o_ref[...] = y.astype(o_ref.dtype)

    if nk == 1:
        finish(_dot(a_ref[...], w_ref[...]))
    else:
        k = pl.program_id(2)

        @pl.when(k == 0)
        def _():
            acc_ref[...] = jnp.zeros_like(acc_ref)

        acc_ref[...] += _dot(a_ref[...], w_ref[...])

        @pl.when(k == nk - 1)
        def _():
            finish(acc_ref[...])


def _matmul(a, w, *, out_dtype, epilogue="none", tm=1024, tn=512, tk=None,
            res=None, gate_row=None, seq=None, layer=None):
    m = a.shape[0]
    kdim, n = w.shape[-2:]
    tk = kdim if tk is None else tk
    nk = kdim // tk
    tn = min(tn, n)
    if layer is None:
        w_spec = pl.BlockSpec((tk, tn), lambda i, j, k: (k, j))
    else:
        w_spec = pl.BlockSpec((None, tk, tn), lambda i, j, k: (layer, k, j))
    in_specs = [pl.BlockSpec((tm, tk), lambda i, j, k: (i, k)), w_spec]
    args = [a, w]
    if epilogue == "residual":
        tpb = seq // tm
        in_specs += [pl.BlockSpec((tm, tn), lambda i, j, k: (i, j)),
                     pl.BlockSpec((1, 1, tn), lambda i, j, k: (i // tpb, 0, j))]
        args += [res, gate_row[:, None, :]]
    scratch = [pltpu.VMEM((tm, tn), F32)] if nk > 1 else []
    return pl.pallas_call(
        functools.partial(_mm_kernel, epilogue=epilogue, nk=nk),
        grid=(m // tm, n // tn, nk),
        in_specs=in_specs,
        out_specs=pl.BlockSpec((tm, tn), lambda i, j, k: (i, j)),
        out_shape=jax.ShapeDtypeStruct((m, n), out_dtype),
        scratch_shapes=scratch,
        compiler_params=_cparams(("arbitrary", "arbitrary", "arbitrary")),
        name="matmul_" + epilogue,
    )(*args)


def _rope_rows(period, half, rot):
    j = np.arange(LANES) % period
    inv = ROPE_THETA ** (-(np.arange(half, dtype=np.float32)) / np.float32(half))
    inv_row = np.where(j < rot, inv.astype(np.float32)[j % half], 0.0).astype(np.float32)
    sign_row = np.where(j < half, -1.0, np.where(j < rot, 1.0, 0.0)).astype(np.float32)
    first_row = (j < half).astype(np.float32)
    return np.stack([inv_row, sign_row, first_row])[:, None, :]


def _rope_apply(x, cos_t, sin_s, first, half):
    up = pltpu.roll(x, LANES - half, axis=1)
    dn = pltpu.roll(x, half, axis=1)
    sw = jnp.where(first > 0.5, up, dn)
    return x * cos_t + sw * sin_s


def _rope_table_kernel(pos_ref, rows_a_ref, rows_i_ref, ca_ref, sa_ref, ci_ref, si_ref):
    pos = pos_ref[...].astype(F32)
    ang_a = pos * rows_a_ref[0]
    ca_ref[...] = jnp.cos(ang_a)
    sa_ref[...] = jnp.sin(ang_a) * rows_a_ref[1]
    ang_i = pos * rows_i_ref[0]
    ci_ref[...] = jnp.cos(ang_i)
    si_ref[...] = jnp.sin(ang_i) * rows_i_ref[1]


def _rope_consts():
    rows_a = jnp.asarray(_rope_rows(ATT_HEAD_DIM, ATT_HEAD_DIM // ROPE_FRACTION // 2,
                                    ATT_HEAD_DIM // ROPE_FRACTION))
    rows_i = jnp.asarray(_rope_rows(IDX_HEAD_DIM, IDX_HEAD_DIM // ROPE_FRACTION // 2,
                                    IDX_HEAD_DIM // ROPE_FRACTION))
    return rows_a, rows_i


def _rope_tables(positions, tm=512):
    m = positions.size
    rows_a, rows_i = _rope_consts()
    rows_spec = pl.BlockSpec((3, 1, LANES), lambda i: (0, 0, 0))
    tab = pl.BlockSpec((tm, LANES), lambda i: (i, 0))
    return pl.pallas_call(
        _rope_table_kernel,
        grid=(m // tm,),
        in_specs=[pl.BlockSpec((tm, 1), lambda i: (i, 0)), rows_spec, rows_spec],
        out_specs=[tab] * 4,
        out_shape=[jax.ShapeDtypeStruct((m, LANES), F32)] * 4,
        compiler_params=_cparams(("arbitrary",)),
        name="rope_tables",
    )(positions.reshape(m, 1), rows_a, rows_i)


def _rope_kernel(ca_ref, sa_ref, ci_ref, si_ref, rows_a_ref, rows_i_ref, q_ref, kv_ref, qi_ref, sm_ref,
                 qo_ref, ko_ref, vo_ref, qio_ref, kio_ref, wio_ref, *, w_scale):
    cos_a, sin_a, first_a = ca_ref[...], sa_ref[...], rows_a_ref[2]
    cos_i, sin_i, first_i = ci_ref[...], si_ref[...], rows_i_ref[2]
    half_a = ATT_HEAD_DIM // ROPE_FRACTION // 2
    half_i = IDX_HEAD_DIM // ROPE_FRACTION // 2
    for h in range(ATT_HEADS):
        sl = slice(h * LANES, (h + 1) * LANES)
        qo_ref[:, sl] = (_rope_apply(q_ref[:, sl].astype(F32), cos_a, sin_a, first_a, half_a)
                         * ATT_LOGIT_SCALE).astype(BF16)
    for h in range(ATT_KV_HEADS):
        sl = slice(h * LANES, (h + 1) * LANES)
        ko_ref[:, sl] = _rope_apply(kv_ref[:, sl].astype(F32), cos_a, sin_a, first_a, half_a).astype(BF16)
    vo_ref[...] = kv_ref[:, ATT_KV_W:].astype(F32).T.astype(BF16)
    for h in range(IDX_Q_W // LANES):
        sl = slice(h * LANES, (h + 1) * LANES)
        qio_ref[:, sl] = _rope_apply(qi_ref[:, sl].astype(F32), cos_i, sin_i, first_i, half_i).astype(BF16)
    sm = sm_ref[...]
    ki = _rope_apply(sm, cos_i, sin_i, first_i, half_i)
    kio_ref[...] = ki[:, :IDX_HEAD_DIM].astype(BF16)
    wio_ref[...] = sm.T[IDX_HEAD_DIM:IDX_HEAD_DIM + IDX_HEADS, :] * w_scale


def _rope_split(proj, side, tables, tm=512):
    m = proj.shape[0]
    rows_a, rows_i = _rope_consts()
    w_scale = (IDX_HEADS ** -0.5) * (IDX_HEAD_DIM ** -0.5)
    rows_spec = pl.BlockSpec((3, 1, LANES), lambda i: (0, 0, 0))
    tab = pl.BlockSpec((tm, LANES), lambda i: (i, 0))
    outs = pl.pallas_call(
        functools.partial(_rope_kernel, w_scale=w_scale),
        grid=(m // tm,),
        in_specs=[
            tab, tab, tab, tab,
            rows_spec, rows_spec,
            pl.BlockSpec((tm, ATT_Q_W), lambda i: (i, 0)),
            pl.BlockSpec((tm, 2 * ATT_KV_W), lambda i: (i, ATT_Q_W // (2 * ATT_KV_W))),
            pl.BlockSpec((tm, IDX_Q_W), lambda i: (i, (ATT_Q_W + 2 * ATT_KV_W) // IDX_Q_W)),
            pl.BlockSpec((tm, LANES), lambda i: (i, _SIDE_SMALL // LANES)),
        ],
        out_specs=[
            pl.BlockSpec((tm, ATT_Q_W), lambda i: (i, 0)),
            pl.BlockSpec((tm, ATT_KV_W), lambda i: (i, 0)),
            pl.BlockSpec((ATT_KV_W, tm), lambda i: (0, i)),
            pl.BlockSpec((tm, IDX_Q_W), lambda i: (i, 0)),
            pl.BlockSpec((tm, IDX_HEAD_DIM), lambda i: (i, 0)),
            pl.BlockSpec((IDX_HEADS, tm), lambda i: (0, i)),
        ],
        out_shape=[
            jax.ShapeDtypeStruct((m, ATT_Q_W), BF16),
            jax.ShapeDtypeStruct((m, ATT_KV_W), BF16),
            jax.ShapeDtypeStruct((ATT_KV_W, m), BF16),
            jax.ShapeDtypeStruct((m, IDX_Q_W), BF16),
            jax.ShapeDtypeStruct((m, IDX_HEAD_DIM), BF16),
            jax.ShapeDtypeStruct((IDX_HEADS, m), F32),
        ],
        compiler_params=_cparams(("arbitrary",)),
        name="rope_split",
    )(*tables, rows_a, rows_i, proj, proj, proj, side)
    return outs


def _attn_kernel(qi_ref, wi_ref, ki_ref, q_ref, k_ref, vt_ref, o_ref, skey_ref,
                 da_ref, db_ref, sa_ref, sb_ref, *, tq, kc, topk, seq):
    i = pl.program_id(1)
    nchunk = lax.shift_right_logical((i + 1) * tq + (kc - 1), int(np.log2(kc)))
    qpos = i * tq + lax.broadcasted_iota(jnp.int32, (1, tq), 1)
    rb = 8 * SUBLANES

    qi = qi_ref[0]
    wi = wi_ref[...]
    qi_s = jnp.concatenate([qi[:, h * IDX_HEAD_DIM:(h + 1) * IDX_HEAD_DIM] for h in range(IDX_HEADS)],
                           axis=0)

    kh = kc // 2
    last_half = seq // kh - 1

    def score_dots(half, dst_ref):
        k0 = pl.multiple_of(jnp.minimum(half, last_half) * kh, kh)
        dst_ref[...] = _dot_nt(ki_ref[0, pl.ds(k0, kh), :], qi_s)

    def score_keys(half, src_ref):
        k0 = pl.multiple_of(half * kh, kh)
        for j in range(kh // rb):
            rows = slice(j * rb, (j + 1) * rb)
            acc = jnp.zeros((rb, tq), F32)
            for h in range(IDX_HEADS):
                acc = acc + jnp.maximum(src_ref[rows, h * tq:(h + 1) * tq], 0.0) * wi[h:h + 1, :]
            kpos = k0 + j * rb + lax.broadcasted_iota(jnp.int32, (rb, tq), 0)
            bits = pltpu.bitcast(acc, jnp.int32)
            skey = jnp.where(bits < 0, bits ^ jnp.int32(0x7FFFFFFF), bits)
            skey = jnp.where(kpos <= qpos, skey, jnp.int32(INT_MIN))
            skey_ref[pl.ds(k0 + j * rb, rb), :] = skey

    def score_chunk(c, carry):
        score_dots(2 * c + 1, db_ref)
        score_keys(2 * c, da_ref)
        score_dots(2 * c + 2, da_ref)
        score_keys(2 * c + 1, db_ref)
        return carry

    score_dots(0, da_ref)
    lax.fori_loop(0, nchunk, score_chunk, 0)

    nacc = 4

    def count_where(pred_fn):
        row_iota = lax.broadcasted_iota(jnp.int32, (SUBLANES, tq), 0)

        def chunk_body(c, parts):
            k0 = pl.multiple_of(c * kc, kc)
            parts = list(parts)
            xs = skey_ref[pl.ds(k0, kc), :]
            for j in range(kc // SUBLANES):
                x = xs[j * SUBLANES:(j + 1) * SUBLANES]
                parts[j % nacc] = parts[j % nacc] + jnp.where(pred_fn(x, k0 + j * SUBLANES + row_iota), 1, 0)
            return tuple(parts)
        zero = jnp.zeros((SUBLANES, tq), jnp.int32)
        parts = lax.fori_loop(0, nchunk, chunk_body, (zero,) * nacc)
        return jnp.sum(sum(parts[1:], parts[0]), axis=0, keepdims=True)

    nfull = seq // kc
    step = max(nfull // SEARCH_VARIANTS, 1)
    variant = lax.div(nchunk + (step - 1), step) - 1

    def fill_chunk(c, carry):
        rows = pl.ds(pl.multiple_of(c * kc, kc), kc)
        skey_ref[rows, :] = jnp.full((kc, tq), INT_MIN, jnp.int32)
        return carry

    lax.fori_loop(nchunk, (variant + 1) * step, fill_chunk, 0)

    def search_all(nc):
        def bit_body(b, prefix):
            cand = prefix | lax.shift_left(jnp.int32(1), 31 - b)
            cand_s = jnp.broadcast_to(cand ^ jnp.int32(INT_MIN), (SUBLANES, tq))
            parts = [jnp.zeros((SUBLANES, tq), jnp.int32)] * nacc
            for j in range(nc * kc // SUBLANES):
                x = skey_ref[j * SUBLANES:(j + 1) * SUBLANES, :]
                parts[j % nacc] = parts[j % nacc] + jnp.where(x >= cand_s, 1, 0)
            cnt = jnp.sum(sum(parts[1:], parts[0]), axis=0, keepdims=True)
            return jnp.where(cnt >= topk, cand, prefix)
        return lax.fori_loop(0, 32, bit_body, jnp.zeros((1, tq), jnp.int32))

    prefix = lax.switch(variant, [functools.partial(search_all, nc) for nc in range(step, nfull + 1, step)])
    thr = prefix ^ jnp.int32(INT_MIN)

    n_gt = count_where(lambda x, _: x > thr)
    n_eq = count_where(lambda x, _: x == thr)
    need = topk - n_gt
    excess = jnp.where((prefix != 0) & (n_eq > need), 1, 0)
    any_excess = jnp.max(excess) > 0
    idx_bits = int(np.log2(seq))

    def tie_search():
        def tie_bit(b, x):
            cand = x | lax.shift_left(jnp.int32(1), idx_bits - 1 - b)
            below = count_where(lambda xk, kidx: (xk == thr) & (kidx < cand))
            return jnp.where(below < need, cand, x)
        return lax.fori_loop(0, idx_bits, tie_bit, jnp.zeros((1, tq), jnp.int32))

    tie_last = lax.cond(any_excess, tie_search, lambda: jnp.full((1, tq), seq, jnp.int32))

    groups = ATT_KV_HEADS
    rep = ATT_HEADS // ATT_KV_HEADS
    q = q_ref[0]
    q_g = [jnp.concatenate([q[:, (g * rep + r) * LANES:(g * rep + r + 1) * LANES] for r in range(rep)], axis=0)
           for g in range(groups)]

    def qk_dots(half, dst_ref):
        k0 = pl.multiple_of(jnp.minimum(half, last_half) * kh, kh)
        for g in range(groups):
            dst_ref[g] = _dot_nt(k_ref[0, pl.ds(k0, kh), g * LANES:(g + 1) * LANES], q_g[g])

    def softmax_pv(half, src_ref, carry):
        k0 = pl.multiple_of(half * kh, kh)
        x = skey_ref[pl.ds(k0, kh), :]
        kidx = k0 + lax.broadcasted_iota(jnp.int32, (kh, tq), 0)
        sel = (x > thr) | ((x == thr) & (kidx <= tie_last))
        sel = sel & (x != jnp.int32(INT_MIN))
        bias = jnp.where(sel, 0.0, MASK_VALUE)
        bias = jnp.concatenate([bias] * rep, axis=1)
        new = []
        for g in range(groups):
            m_old, l_old, acc_old = carry[g]
            vtg = vt_ref[g * LANES:(g + 1) * LANES, pl.ds(k0, kh)]
            sm = src_ref[g] + bias
            m_new = jnp.maximum(m_old, jnp.max(sm, axis=0, keepdims=True))
            p = jnp.exp2(sm - m_new)
            alpha = jnp.exp2(m_old - m_new)
            l_new = alpha * l_old + jnp.sum(p, axis=0, keepdims=True)
            acc_new = alpha * acc_old + _dot(vtg, p.astype(BF16))
            new.append((m_new, l_new, acc_new))
        return tuple(new)

    def attn_chunk(c, carry):
        qk_dots(2 * c + 1, sb_ref)
        carry = softmax_pv(2 * c, sa_ref, carry)
        qk_dots(2 * c + 2, sa_ref)
        return softmax_pv(2 * c + 1, sb_ref, carry)

    init = tuple((jnp.full((1, rep * tq), MASK_VALUE, F32), jnp.zeros((1, rep * tq), F32),
                  jnp.zeros((LANES, rep * tq), F32)) for _ in range(groups))
    qk_dots(0, sa_ref)
    fin = lax.fori_loop(0, nchunk, attn_chunk, init)
    for g in range(groups):
        _, l_f, acc_f = fin[g]
        o_t = acc_f / l_f
        for r in range(rep):
            h = g * rep + r
            o_ref[0, :, h * LANES:(h + 1) * LANES] = o_t[:, r * tq:(r + 1) * tq].T.astype(o_ref.dtype)


def _dsa_attention(q, k, vt, qi, ki, wit, batch, seq, topk, tq=128, kc=512):
    nblk = seq // tq
    rep = ATT_HEADS // ATT_KV_HEADS
    r3 = lambda a: a.reshape(batch, seq, a.shape[-1])
    qblk = lambda w: pl.BlockSpec((1, tq, w), lambda b, i: (b, i, 0))
    full = lambda w: pl.BlockSpec((1, seq, w), lambda b, i: (b, 0, 0))
    out = pl.pallas_call(
        functools.partial(_attn_kernel, tq=tq, kc=kc, topk=topk, seq=seq),
        grid=(batch, nblk),
        in_specs=[qblk(IDX_Q_W), pl.BlockSpec((IDX_HEADS, tq), lambda b, i: (0, b * nblk + i)),
                  full(IDX_HEAD_DIM), qblk(ATT_Q_W), full(ATT_KV_W),
                  pl.BlockSpec((ATT_KV_W, seq), lambda b, i: (0, b))],
        out_specs=qblk(ATT_Q_W),
        out_shape=jax.ShapeDtypeStruct((batch, seq, ATT_Q_W), BF16),
        scratch_shapes=[pltpu.VMEM((seq, tq), jnp.int32),
                        pltpu.VMEM((kc // 2, IDX_HEADS * tq), F32), pltpu.VMEM((kc // 2, IDX_HEADS * tq), F32),
                        pltpu.VMEM((ATT_KV_HEADS, kc // 2, rep * tq), F32),
                        pltpu.VMEM((ATT_KV_HEADS, kc // 2, rep * tq), F32)],
        compiler_params=_cparams(("arbitrary", "arbitrary")),
        name="dsa_attention",
    )(r3(qi), wit, r3(ki), r3(q), r3(k), vt)
    return out.reshape(batch * seq, ATT_Q_W)


def _conv_kernel(cb_ref, cc_ref, cx_ref, hc_ref, hx_ref, w_ref, o_ref, *, tm, seq):
    i = pl.program_id(0)
    u = cc_ref[...].astype(F32) * cx_ref[...].astype(F32)
    halo = hc_ref[...].astype(F32) * hx_ref[...].astype(F32)
    halo = jnp.where((i * tm) % seq == 0, 0.0, halo)
    row = lax.broadcasted_iota(jnp.int32, u.shape, 0)
    u1 = pltpu.roll(u, 1, axis=0)
    u2 = pltpu.roll(u, 2, axis=0)
    h1 = halo[HALO_ROWS - 1:HALO_ROWS, :]
    h2 = halo[HALO_ROWS - 2:HALO_ROWS - 1, :]
    u1 = jnp.where(row == 0, h1, u1)
    u2 = jnp.where(row == 0, h2, jnp.where(row == 1, h1, u2))
    w = w_ref[...]
    conv = u2 * w[0:1, :] + u1 * w[1:2, :] + u * w[2:3, :]
    o_ref[...] = (cb_ref[...].astype(F32) * conv).astype(o_ref.dtype)


def _short_conv(proj, conv_w, seq, col0, tm=512, tc=512):
    m = proj.shape[0]
    nb = CONV_WIDTH // tc
    c0 = col0 // tc
    rpb = tm // HALO_ROWS
    main = lambda off: pl.BlockSpec((tm, tc), lambda i, j: (i, c0 + off * nb + j))
    halo = lambda off: pl.BlockSpec((HALO_ROWS, tc),
                                    lambda i, j: (jnp.maximum(i * rpb - 1, 0), c0 + off * nb + j))
    return pl.pallas_call(
        functools.partial(_conv_kernel, tm=tm, seq=seq),
        grid=(m // tm, nb),
        in_specs=[main(0), main(1), main(2), halo(1), halo(2),
                  pl.BlockSpec((CONV_K, tc), lambda i, j: (0, j))],
        out_specs=pl.BlockSpec((tm, tc), lambda i, j: (i, j)),
        out_shape=jax.ShapeDtypeStruct((m, CONV_WIDTH), BF16),
        compiler_params=_cparams(("arbitrary", "arbitrary")),
        name="short_conv",
    )(proj, proj, proj, proj, proj, conv_w)


def _hg_tables():
    c = HG_CHUNK
    t = np.arange(c)
    mats, masks = [], []
    for l in range(HG_LEVELS):
        h = 1 << l
        start = (t // (2 * h)) * (2 * h)
        p = start + h - 1
        right = (t - start) >= h
        u = t[None, :]
        r_m = right[:, None] & (u > p[:, None]) & (u <= t[:, None])
        l_m = (~right)[:, None] & (u > t[:, None]) & (u <= p[:, None])
        mats.append(r_m | l_m)
        same = (t[:, None] // (2 * h)) == (t[None, :] // (2 * h))
        masks.append(same & right[:, None] & (~right)[None, :])
    mats.append(t[None, :] <= t[:, None])
    mats.append(t[None, :] > t[:, None])
    mat = np.concatenate(mats, 0).astype(np.float32)
    return np.concatenate([mat] * HG_SPLIT, 1), np.stack(masks).astype(np.float32)


def _split3(x):
    hi = x.astype(BF16)
    r1 = x - hi.astype(F32)
    mid = r1.astype(BF16)
    lo = (r1 - mid.astype(F32)).astype(BF16)
    return jnp.concatenate([hi, mid, lo], axis=0)


def _hg_kernel(hq_ref, hf_ref, hi_ref, hg_ref, lb_ref, g_ref, mat_ref, mask_ref, o_ref, state_ref, *, ts):
    c = HG_CHUNK

    @pl.when(pl.program_id(1) == 0)
    def _():
        state_ref[...] = jnp.zeros_like(state_ref)

    g = g_ref[...]

    def head_chunk(h, rows, e_all):
        cols = slice(h * LANES, (h + 1) * LANES)
        lb = lb_ref[:, cols]
        kin = 1.0 - (lb + (1.0 - lb) * _sigmoid(hf_ref[rows, cols]))
        q = hq_ref[rows, cols].astype(F32)
        v16 = hi_ref[rows, cols]
        v = v16.astype(F32)
        e = e_all[:, cols]
        attn = jnp.zeros((c, c), F32)
        for l in range(HG_LEVELS):
            el = e[l * c:(l + 1) * c]
            attn = attn + _dot_nt((q * el).astype(BF16), (kin * el).astype(BF16)) * mask_ref[l]
        e_b = e[HG_LEVELS * c:(HG_LEVELS + 1) * c]
        e_k = e[(HG_LEVELS + 1) * c:(HG_LEVELS + 2) * c]
        state_t = state_ref[h]
        diag = jnp.sum(q * kin, axis=1, keepdims=True)
        out = (_dot(attn.astype(BF16), v16) + diag * v
               + _dot_nt((q * e_b).astype(BF16), state_t.astype(BF16)))
        kv_t = lax.dot_general(v16, (kin * e_k).astype(BF16), (((0,), (0,)), ((), ())),
                               preferred_element_type=F32)
        state_ref[h] = state_t * e_b[c - 1:c, :] + kv_t
        y = out * lax.rsqrt(jnp.mean(out * out, axis=-1, keepdims=True) + NORM_EPS) * g
        gate = hg_ref[rows, cols].astype(F32)
        o_ref[rows, cols] = (y * (gate * _sigmoid(gate))).astype(o_ref.dtype)

    def chunk(ci, carry):
        rows = pl.ds(pl.multiple_of(ci * c, c), c)
        lb = lb_ref[...]
        f = lb + (1.0 - lb) * _sigmoid(hf_ref[rows, :])
        logf = jnp.log(jnp.maximum(f, F_MIN))
        e_all = jnp.exp(_dot(mat_ref[...], _split3(logf)))
        for h in range(HG_HEADS):
            head_chunk(h, rows, e_all)
        return carry

    lax.fori_loop(0, ts // c, chunk, 0)


def _hgrn2(proj, side, lb, norm_g, batch, seq, col0, ts=512):
    m = proj.shape[0]
    c0 = col0 // HG_W
    spb = seq // ts
    mat, masks = _hg_tables()
    col = lambda off: pl.BlockSpec((ts, HG_W), lambda b, s: (b * spb + s, c0 + off))
    return pl.pallas_call(
        functools.partial(_hg_kernel, ts=ts),
        grid=(batch, spb),
        in_specs=[col(0), pl.BlockSpec((ts, HG_W), lambda b, s: (b * spb + s, 0)), col(1), col(2),
                  pl.BlockSpec((1, HG_W), lambda b, s: (0, 0)),
                  pl.BlockSpec((1, LANES), lambda b, s: (0, 0)),
                  pl.BlockSpec(mat.shape, lambda b, s: (0, 0)),
                  pl.BlockSpec(masks.shape, lambda b, s: (0, 0, 0))],
        out_specs=pl.BlockSpec((ts, HG_W), lambda b, s: (b * spb + s, 0)),
        out_shape=jax.ShapeDtypeStruct((m, HG_W), BF16),
        scratch_shapes=[pltpu.VMEM((HG_HEADS, HG_VAL_DIM, HG_KEY_DIM), F32)],
        compiler_params=_cparams(("arbitrary", "arbitrary")),
        name="hgrn2",
    )(proj, side, proj, proj, lb.reshape(1, HG_W), norm_g.reshape(1, HG_VAL_DIM),
      jnp.asarray(mat, BF16), jnp.asarray(masks))


def _lb_kernel(x_ref, o_ref):
    x = x_ref[...]
    e = jnp.exp(x - jnp.max(x, axis=0, keepdims=True))
    p = e / jnp.sum(e, axis=0, keepdims=True)
    depth = x.shape[0]
    run = jnp.zeros_like(p[0:1])
    for l in range(depth):
        run = run + p[l:l + 1]
        o_ref[l:l + 1, :] = run - p[0:1]


def _lower_bounds(hg_lower_bounds):
    return pl.pallas_call(
        _lb_kernel,
        out_shape=jax.ShapeDtypeStruct(hg_lower_bounds.shape, F32),
        name="hg_lower_bounds",
    )(hg_lower_bounds.astype(F32))


_COL_Q = 0
_COL_CONV = ATT_Q_W + 2 * ATT_KV_W + IDX_Q_W
_COL_HG = _COL_CONV + 3 * CONV_WIDTH
_COL_GATE = _COL_HG + 3 * HG_W
_SIDE_SMALL = HG_W
_SIDE_W = HG_W + LANES


def _w_in_segments(d_model):
    sizes = (ATT_Q_W, ATT_KV_W, ATT_KV_W, IDX_Q_W, IDX_HEAD_DIM, IDX_HEADS,
             CONV_WIDTH, CONV_WIDTH, CONV_WIDTH, HG_W, HG_W, HG_W, HG_W, N_BRANCHES * d_model)
    offs = np.cumsum((0,) + sizes).tolist()

    def runs(parts):
        out, dst = [], 0
        for n in parts:
            if out and out[-1][0] + out[-1][2] == offs[n]:
                out[-1][2] += sizes[n]
            else:
                out.append([offs[n], dst, sizes[n]])
            dst += sizes[n]
        return [tuple(r) for r in out], dst

    main, main_w = runs((0, 1, 2, 3, 6, 7, 8, 9, 11, 12, 13))
    side, side_used = runs((10, 4, 5))
    return main, main_w, side, side_used, offs[-1]


def _w_in_kernel(w_ref, main_ref, side_ref, *, main, side, side_used):
    piece = 8 * LANES

    def copy(dst_ref, src, dst, width):
        for o in range(0, width, piece):
            s, wd = src + o, min(piece, width - o)
            a0 = (s // LANES) * LANES
            a1 = min(-(-(s + wd) // LANES) * LANES, w_ref.shape[1])
            dst_ref[:, dst + o:dst + o + wd] = w_ref[:, a0:a1][:, s - a0:s - a0 + wd].astype(dst_ref.dtype)

    for src, dst, width in main:
        copy(main_ref, src, dst, width)
    for src, dst, width in side:
        copy(side_ref, src, dst, width)
    side_ref[:, side_used:] = jnp.zeros((side_ref.shape[0], side_ref.shape[1] - side_used), side_ref.dtype)


def _prep_w_in(w_in, tr=128):
    depth, d, d_in = w_in.shape
    main, main_w, side, side_used, total = _w_in_segments(d)
    assert total == d_in
    return pl.pallas_call(
        functools.partial(_w_in_kernel, main=main, side=side, side_used=side_used),
        grid=(depth, d // tr),
        in_specs=[pl.BlockSpec((None, tr, d_in), lambda l, i: (l, i, 0))],
        out_specs=[pl.BlockSpec((None, tr, main_w), lambda l, i: (l, i, 0)),
                   pl.BlockSpec((None, tr, _SIDE_W), lambda l, i: (l, i, 0))],
        out_shape=[jax.ShapeDtypeStruct((depth, d, main_w), BF16),
                   jax.ShapeDtypeStruct((depth, d, _SIDE_W), BF16)],
        compiler_params=_cparams(("arbitrary", "arbitrary")),
        name="prep_w_in",
    )(w_in)


def kernel(x, c, positions, ada_w, ada_b, norm_mix_g, w_in, conv_w, hg_lower_bounds, hg_norm_g,
           w_o_attn, w_o_conv, w_o_hgrn, w_out, norm_mlp_g, w_mlp1, w_mlp2, final_norm_g):
    batch, seq, d = x.shape
    depth = ada_w.shape[0]
    m = batch * seq
    topk = min(INDEX_TOPK, seq // 4)
    lb_all = _lower_bounds(hg_lower_bounds)
    mod = _modulation(c, ada_w, ada_b)
    w_main, w_side = _prep_w_in(w_in)
    tables = _rope_tables(positions)
    x2 = x.reshape(m, d)
    for l in range(depth):
        sh1, sc1, g1, sh2, sc2, g2 = [mod[l, :, n * d:(n + 1) * d] for n in range(ADA_CHUNKS)]
        h = _norm(x2, norm_mix_g[l], seq, sh1, sc1)
        proj = _matmul(h, w_main, out_dtype=BF16, tn=1024, layer=l)
        side = _matmul(h, w_side, out_dtype=F32, tn=_SIDE_W, layer=l)
        q, k, vt, qi, ki, wit = _rope_split(proj, side, tables)
        att = _dsa_attention(q, k, vt, qi, ki, wit, batch, seq, topk)
        cv = _short_conv(proj, conv_w[l], seq, _COL_CONV)
        hg = _hgrn2(proj, side, lb_all[l], hg_norm_g[l], batch, seq, _COL_HG)
        x2, h = _merge_out(att, cv, hg, w_o_attn[l].astype(BF16), w_o_conv[l].astype(BF16),
                           w_o_hgrn[l].astype(BF16), proj, _COL_GATE, w_out[l].astype(BF16), x2, g1,
                           norm_mlp_g[l], sh2, sc2, seq)
        a = _matmul(h, w_mlp1[l].astype(BF16), out_dtype=BF16, epilogue="relu2")
        x2 = _matmul(a, w_mlp2[l].astype(BF16), out_dtype=F32, epilogue="residual", tm=512,
                     res=x2, gate_row=g2, seq=seq)
    out = _norm(x2, final_norm_g, seq, out_dtype=x.dtype)
    return out.reshape(batch, seq, d)
```

```python
import functools

import numpy as np
import jax
import jax.numpy as jnp
from jax import lax
from jax.experimental import pallas as pl
from jax.experimental.pallas import tpu as pltpu

ATT_HEADS = 8
ATT_KV_HEADS = 2
ATT_HEAD_DIM = 128
IDX_HEADS = 8
IDX_HEAD_DIM = 64
INDEX_TOPK = 256
ROPE_THETA = 500000.0
ROPE_FRACTION = 4
MASK_VALUE = -1e30
CONV_WIDTH = 1024
CONV_K = 3
HG_HEADS = 8
HG_KEY_DIM = 128
HG_VAL_DIM = 128
F_MIN = 1e-30
N_BRANCHES = 3
NORM_EPS = 1e-6
ADA_CHUNKS = 6

ATT_Q_W = ATT_HEADS * ATT_HEAD_DIM
ATT_KV_W = ATT_KV_HEADS * ATT_HEAD_DIM
IDX_Q_W = IDX_HEADS * IDX_HEAD_DIM
HG_W = HG_HEADS * HG_KEY_DIM
ATT_LOGIT_SCALE = (ATT_HEAD_DIM ** -0.5) * float(np.log2(np.e))

LANES = 128
SUBLANES = 8
HALO_ROWS = 16
VMEM_LIMIT_BYTES = 56 * 1024 * 1024

INT_MIN = -2 ** 31
SEARCH_VARIANTS = 8
HG_CHUNK = 64
HG_LEVELS = 6
HG_SPLIT = 3

BF16 = jnp.bfloat16
F32 = jnp.float32


def _cparams(sem):
    return pltpu.CompilerParams(dimension_semantics=sem, vmem_limit_bytes=VMEM_LIMIT_BYTES)


def _dot(a, b):
    return jnp.dot(a, b, preferred_element_type=F32)


def _dot_nt(a, b):
    return lax.dot_general(a, b, (((1,), (1,)), ((), ())), preferred_element_type=F32)


def _sigmoid(x):
    return 1.0 / (1.0 + jnp.exp(-x))


def _mod_kernel(c_ref, w_ref, b_ref, o_ref):
    c = c_ref[...]
    ca = (c * _sigmoid(c)).astype(BF16)
    o_ref[...] = _dot(ca, w_ref[...].astype(BF16)) + b_ref[...]


def _modulation(c, ada_w, ada_b):
    depth, d, n = ada_w.shape
    b = c.shape[0]
    bp = -(-b // SUBLANES) * SUBLANES
    cp = jnp.pad(c, ((0, bp - b), (0, 0)))
    tn = 1024
    out = pl.pallas_call(
        _mod_kernel,
        grid=(depth, n // tn),
        in_specs=[
            pl.BlockSpec((bp, d), lambda l, j: (0, 0)),
            pl.BlockSpec((None, d, tn), lambda l, j: (l, 0, j)),
            pl.BlockSpec((None, 1, tn), lambda l, j: (l, 0, j)),
        ],
        out_specs=pl.BlockSpec((None, bp, tn), lambda l, j: (l, 0, j)),
        out_shape=jax.ShapeDtypeStruct((depth, bp, n), F32),
        compiler_params=_cparams(("arbitrary", "arbitrary")),
        name="adaln_mod",
    )(cp, ada_w, ada_b.reshape(depth, 1, n))
    return out[:, :b]


def _norm_kernel(x_ref, g_ref, *rest, modulate):
    if modulate:
        sh_ref, sc_ref, o_ref = rest
    else:
        (o_ref,) = rest
    x = x_ref[...]
    y = x * lax.rsqrt(jnp.mean(x * x, axis=-1, keepdims=True) + NORM_EPS) * g_ref[...]
    if modulate:
        y = y * (1.0 + sc_ref[0]) + sh_ref[0]
    o_ref[...] = y.astype(o_ref.dtype)


def _norm(x2, g, seq, shift=None, scale=None, out_dtype=BF16, tm=512):
    m, d = x2.shape
    modulate = shift is not None
    tpb = seq // tm
    in_specs = [pl.BlockSpec((tm, d), lambda i: (i, 0)),
                pl.BlockSpec((1, d), lambda i: (0, 0))]
    args = [x2, g.reshape(1, d)]
    if modulate:
        row = pl.BlockSpec((1, 1, d), lambda i: (i // tpb, 0, 0))
        in_specs += [row, row]
        args += [shift[:, None, :], scale[:, None, :]]
    return pl.pallas_call(
        functools.partial(_norm_kernel, modulate=modulate),
        grid=(m // tm,),
        in_specs=in_specs,
        out_specs=pl.BlockSpec((tm, d), lambda i: (i, 0)),
        out_shape=jax.ShapeDtypeStruct((m, d), out_dtype),
        compiler_params=_cparams(("arbitrary",)),
        name="rmsnorm",
    )(*args)


def _merge_kernel(att_ref, cv_ref, hg_ref, wa_ref, wc_ref, wh_ref, ga_ref, gc_ref, gh_ref,
                  wo_ref, x_ref, gr_ref, ng_ref, sh_ref, sc_ref, o_ref, h_ref):
    merged = (_sigmoid(ga_ref[...].astype(F32)) * _dot(att_ref[...], wa_ref[...])
              + _sigmoid(gc_ref[...].astype(F32)) * _dot(cv_ref[...], wc_ref[...])
              + _sigmoid(gh_ref[...].astype(F32)) * _dot(hg_ref[...], wh_ref[...]))
    y = x_ref[...] + gr_ref[0] * _dot(merged.astype(BF16), wo_ref[...])
    o_ref[...] = y
    yn = y * lax.rsqrt(jnp.mean(y * y, axis=-1, keepdims=True) + NORM_EPS) * ng_ref[...]
    h_ref[...] = (yn * (1.0 + sc_ref[0]) + sh_ref[0]).astype(h_ref.dtype)


def _merge_out(att, cv, hg, wa, wc, wh, proj, gate_off, wout, x2, gate_row, norm_g, shift, scale, seq, tm=256):
    m, d = x2.shape
    tpb = seq // tm
    g0 = gate_off // d
    act = lambda a: pl.BlockSpec((tm, a.shape[1]), lambda i: (i, 0))
    resident = lambda w: pl.BlockSpec(w.shape, lambda i: (0, 0), pipeline_mode=pl.Buffered(1))
    gate = lambda b: pl.BlockSpec((tm, d), lambda i: (i, g0 + b))
    row = pl.BlockSpec((1, 1, d), lambda i: (i // tpb, 0, 0))
    full = pl.BlockSpec((tm, d), lambda i: (i, 0))
    return pl.pallas_call(
        _merge_kernel,
        grid=(m // tm,),
        in_specs=[act(att), act(cv), act(hg), resident(wa), resident(wc), resident(wh),
                  gate(0), gate(1), gate(2), resident(wout), full, row,
                  pl.BlockSpec((1, d), lambda i: (0, 0)), row, row],
        out_specs=[full, full],
        out_shape=[jax.ShapeDtypeStruct((m, d), F32), jax.ShapeDtypeStruct((m, d), BF16)],
        compiler_params=_cparams(("arbitrary",)),
        name="merge_out",
    )(att, cv, hg, wa, wc, wh, proj, proj, proj, wout, x2, gate_row[:, None, :],
      norm_g.reshape(1, d), shift[:, None, :], scale[:, None, :])


def _mm_kernel(*refs, epilogue, nk):
    a_ref, w_ref = refs[0], refs[1]
    pos = 2
    extra = []
    n_extra = {"none": 0, "relu2": 0, "residual": 2}[epilogue]
    for _ in range(n_extra):
        extra.append(refs[pos])
        pos += 1
    o_ref = refs[pos]
    acc_ref = refs[pos + 1] if nk > 1 else None

    def finish(y):
        if epilogue == "relu2":
            y = jnp.square(jnp.maximum(y, 0.0))
        elif epilogue == "residual":
            y = extra[0][...] + extra[1][0] * y
        o_ref[...] = y.astype(o_ref.dtype)

    if nk == 1:
        finish(_dot(a_ref[...], w_ref[...]))
    else:
        k = pl.program_id(2)

        @pl.when(k == 0)
        def _():
            acc_ref[...] = jnp.zeros_like(acc_ref)

        acc_ref[...] += _dot(a_ref[...], w_ref[...])

        @pl.when(k == nk - 1)
        def _():
            finish(acc_ref[...])


def _matmul(a, w, *, out_dtype, epilogue="none", tm=1024, tn=512, tk=None,
            res=None, gate_row=None, seq=None):
    m = a.shape[0]
    kdim, n = w.shape
    tk = kdim if tk is None else tk
    nk = kdim // tk
    tn = min(tn, n)
    in_specs = [pl.BlockSpec((tm, tk), lambda i, j, k: (i, k)),
                pl.BlockSpec((tk, tn), lambda i, j, k: (k, j))]
    args = [a, w]
    if epilogue == "residual":
        tpb = seq // tm
        in_specs += [pl.BlockSpec((tm, tn), lambda i, j, k: (i, j)),
                     pl.BlockSpec((1, 1, tn), lambda i, j, k: (i // tpb, 0, j))]
        args += [res, gate_row[:, None, :]]
    scratch = [pltpu.VMEM((tm, tn), F32)] if nk > 1 else []
    return pl.pallas_call(
        functools.partial(_mm_kernel, epilogue=epilogue, nk=nk),
        grid=(m // tm, n // tn, nk),
        in_specs=in_specs,
        out_specs=pl.BlockSpec((tm, tn), lambda i, j, k: (i, j)),
        out_shape=jax.ShapeDtypeStruct((m, n), out_dtype),
        scratch_shapes=scratch,
        compiler_params=_cparams(("arbitrary", "arbitrary", "arbitrary")),
        name="matmul_" + epilogue,
    )(*args)


def _rope_rows(period, half, rot):
    j = np.arange(LANES) % period
    inv = ROPE_THETA ** (-(np.arange(half, dtype=np.float32)) / np.float32(half))
    inv_row = np.where(j < rot, inv.astype(np.float32)[j % half], 0.0).astype(np.float32)
    sign_row = np.where(j < half, -1.0, np.where(j < rot, 1.0, 0.0)).astype(np.float32)
    first_row = (j < half).astype(np.float32)
    return np.stack([inv_row, sign_row, first_row])[:, None, :]


def _rope_apply(x, cos_t, sin_s, first, half):
    up = pltpu.roll(x, LANES - half, axis=1)
    dn = pltpu.roll(x, half, axis=1)
    sw = jnp.where(first > 0.5, up, dn)
    return x * cos_t + sw * sin_s


def _rope_table_kernel(pos_ref, rows_a_ref, rows_i_ref, ca_ref, sa_ref, ci_ref, si_ref):
    pos = pos_ref[...].astype(F32)
    ang_a = pos * rows_a_ref[0]
    ca_ref[...] = jnp.cos(ang_a)
    sa_ref[...] = jnp.sin(ang_a) * rows_a_ref[1]
    ang_i = pos * rows_i_ref[0]
    ci_ref[...] = jnp.cos(ang_i)
    si_ref[...] = jnp.sin(ang_i) * rows_i_ref[1]


def _rope_consts():
    rows_a = jnp.asarray(_rope_rows(ATT_HEAD_DIM, ATT_HEAD_DIM // ROPE_FRACTION // 2,
                                    ATT_HEAD_DIM // ROPE_FRACTION))
    rows_i = jnp.asarray(_rope_rows(IDX_HEAD_DIM, IDX_HEAD_DIM // ROPE_FRACTION // 2,
                                    IDX_HEAD_DIM // ROPE_FRACTION))
    return rows_a, rows_i


def _rope_tables(positions, tm=512):
    m = positions.size
    rows_a, rows_i = _rope_consts()
    rows_spec = pl.BlockSpec((3, 1, LANES), lambda i: (0, 0, 0))
    tab = pl.BlockSpec((tm, LANES), lambda i: (i, 0))
    return pl.pallas_call(
        _rope_table_kernel,
        grid=(m // tm,),
        in_specs=[pl.BlockSpec((tm, 1), lambda i: (i, 0)), rows_spec, rows_spec],
        out_specs=[tab] * 4,
        out_shape=[jax.ShapeDtypeStruct((m, LANES), F32)] * 4,
        compiler_params=_cparams(("arbitrary",)),
        name="rope_tables",
    )(positions.reshape(m, 1), rows_a, rows_i)


def _rope_kernel(ca_ref, sa_ref, ci_ref, si_ref, rows_a_ref, rows_i_ref, q_ref, kv_ref, qi_ref, sm_ref,
                 qo_ref, ko_ref, vo_ref, qio_ref, kio_ref, wio_ref, *, w_scale):
    cos_a, sin_a, first_a = ca_ref[...], sa_ref[...], rows_a_ref[2]
    cos_i, sin_i, first_i = ci_ref[...], si_ref[...], rows_i_ref[2]
    half_a = ATT_HEAD_DIM // ROPE_FRACTION // 2
    half_i = IDX_HEAD_DIM // ROPE_FRACTION // 2
    for h in range(ATT_HEADS):
        sl = slice(h * LANES, (h + 1) * LANES)
        qo_ref[:, sl] = (_rope_apply(q_ref[:, sl].astype(F32), cos_a, sin_a, first_a, half_a)
                         * ATT_LOGIT_SCALE).astype(BF16)
    for h in range(ATT_KV_HEADS):
        sl = slice(h * LANES, (h + 1) * LANES)
        ko_ref[:, sl] = _rope_apply(kv_ref[:, sl].astype(F32), cos_a, sin_a, first_a, half_a).astype(BF16)
    vo_ref[...] = kv_ref[:, ATT_KV_W:].astype(F32).T.astype(BF16)
    for h in range(IDX_Q_W // LANES):
        sl = slice(h * LANES, (h + 1) * LANES)
        qio_ref[:, sl] = _rope_apply(qi_ref[:, sl].astype(F32), cos_i, sin_i, first_i, half_i).astype(BF16)
    sm = sm_ref[...]
    ki = _rope_apply(sm, cos_i, sin_i, first_i, half_i)
    kio_ref[...] = ki[:, :IDX_HEAD_DIM].astype(BF16)
    wio_ref[...] = sm.T[IDX_HEAD_DIM:IDX_HEAD_DIM + IDX_HEADS, :] * w_scale


def _rope_split(proj, side, tables, tm=512):
    m = proj.shape[0]
    rows_a, rows_i = _rope_consts()
    w_scale = (IDX_HEADS ** -0.5) * (IDX_HEAD_DIM ** -0.5)
    rows_spec = pl.BlockSpec((3, 1, LANES), lambda i: (0, 0, 0))
    tab = pl.BlockSpec((tm, LANES), lambda i: (i, 0))
    outs = pl.pallas_call(
        functools.partial(_rope_kernel, w_scale=w_scale),
        grid=(m // tm,),
        in_specs=[
            tab, tab, tab, tab,
            rows_spec, rows_spec,
            pl.BlockSpec((tm, ATT_Q_W), lambda i: (i, 0)),
            pl.BlockSpec((tm, 2 * ATT_KV_W), lambda i: (i, ATT_Q_W // (2 * ATT_KV_W))),
            pl.BlockSpec((tm, IDX_Q_W), lambda i: (i, (ATT_Q_W + 2 * ATT_KV_W) // IDX_Q_W)),
            pl.BlockSpec((tm, LANES), lambda i: (i, _SIDE_SMALL // LANES)),
        ],
        out_specs=[
            pl.BlockSpec((tm, ATT_Q_W), lambda i: (i, 0)),
            pl.BlockSpec((tm, ATT_KV_W), lambda i: (i, 0)),
            pl.BlockSpec((ATT_KV_W, tm), lambda i: (0, i)),
            pl.BlockSpec((tm, IDX_Q_W), lambda i: (i, 0)),
            pl.BlockSpec((tm, IDX_HEAD_DIM), lambda i: (i, 0)),
            pl.BlockSpec((IDX_HEADS, tm), lambda i: (0, i)),
        ],
        out_shape=[
            jax.ShapeDtypeStruct((m, ATT_Q_W), BF16),
            jax.ShapeDtypeStruct((m, ATT_KV_W), BF16),
            jax.ShapeDtypeStruct((ATT_KV_W, m), BF16),
            jax.ShapeDtypeStruct((m, IDX_Q_W), BF16),
            jax.ShapeDtypeStruct((m, IDX_HEAD_DIM), BF16),
            jax.ShapeDtypeStruct((IDX_HEADS, m), F32),
        ],
        compiler_params=_cparams(("arbitrary",)),
        name="rope_split",
    )(*tables, rows_a, rows_i, proj, proj, proj, side)
    return outs


def _attn_kernel(qi_ref, wi_ref, ki_ref, q_ref, k_ref, vt_ref, o_ref, skey_ref,
                 da_ref, db_ref, sa_ref, sb_ref, *, tq, kc, topk, seq):
    i = pl.program_id(1)
    nchunk = lax.shift_right_logical((i + 1) * tq + (kc - 1), int(np.log2(kc)))
    qpos = i * tq + lax.broadcasted_iota(jnp.int32, (1, tq), 1)
    rb = 8 * SUBLANES

    qi = qi_ref[0]
    wi = wi_ref[...]
    qi_s = jnp.concatenate([qi[:, h * IDX_HEAD_DIM:(h + 1) * IDX_HEAD_DIM] for h in range(IDX_HEADS)],
                           axis=0)

    kh = kc // 2
    last_half = seq // kh - 1

    def score_dots(half, dst_ref):
        k0 = pl.multiple_of(jnp.minimum(half, last_half) * kh, kh)
        dst_ref[...] = _dot_nt(ki_ref[0, pl.ds(k0, kh), :], qi_s)

    def score_keys(half, src_ref):
        k0 = pl.multiple_of(half * kh, kh)
        for j in range(kh // rb):
            rows = slice(j * rb, (j + 1) * rb)
            acc = jnp.zeros((rb, tq), F32)
            for h in range(IDX_HEADS):
                acc = acc + jnp.maximum(src_ref[rows, h * tq:(h + 1) * tq], 0.0) * wi[h:h + 1, :]
            kpos = k0 + j * rb + lax.broadcasted_iota(jnp.int32, (rb, tq), 0)
            bits = pltpu.bitcast(acc, jnp.int32)
            skey = jnp.where(bits < 0, bits ^ jnp.int32(0x7FFFFFFF), bits)
            skey = jnp.where(kpos <= qpos, skey, jnp.int32(INT_MIN))
            skey_ref[pl.ds(k0 + j * rb, rb), :] = skey

    def score_chunk(c, carry):
        score_dots(2 * c + 1, db_ref)
        score_keys(2 * c, da_ref)
        score_dots(2 * c + 2, da_ref)
        score_keys(2 * c + 1, db_ref)
        return carry

    score_dots(0, da_ref)
    lax.fori_loop(0, nchunk, score_chunk, 0)

    nacc = 4

    def count_where(pred_fn):
        row_iota = lax.broadcasted_iota(jnp.int32, (SUBLANES, tq), 0)

        def chunk_body(c, parts):
            k0 = pl.multiple_of(c * kc, kc)
            parts = list(parts)
            xs = skey_ref[pl.ds(k0, kc), :]
            for j in range(kc // SUBLANES):
                x = xs[j * SUBLANES:(j + 1) * SUBLANES]
                parts[j % nacc] = parts[j % nacc] + jnp.where(pred_fn(x, k0 + j * SUBLANES + row_iota), 1, 0)
            return tuple(parts)
        zero = jnp.zeros((SUBLANES, tq), jnp.int32)
        parts = lax.fori_loop(0, nchunk, chunk_body, (zero,) * nacc)
        return jnp.sum(sum(parts[1:], parts[0]), axis=0, keepdims=True)

    nfull = seq // kc
    step = max(nfull // SEARCH_VARIANTS, 1)
    variant = lax.div(nchunk + (step - 1), step) - 1

    def fill_chunk(c, carry):
        rows = pl.ds(pl.multiple_of(c * kc, kc), kc)
        skey_ref[rows, :] = jnp.full((kc, tq), INT_MIN, jnp.int32)
        return carry

    lax.fori_loop(nchunk, (variant + 1) * step, fill_chunk, 0)

    def search_all(nc):
        def bit_body(b, prefix):
            cand = prefix | lax.shift_left(jnp.int32(1), 31 - b)
            cand_s = jnp.broadcast_to(cand ^ jnp.int32(INT_MIN), (SUBLANES, tq))
            parts = [jnp.zeros((SUBLANES, tq), jnp.int32)] * nacc
            for j in range(nc * kc // SUBLANES):
                x = skey_ref[j * SUBLANES:(j + 1) * SUBLANES, :]
                parts[j % nacc] = parts[j % nacc] + jnp.where(x >= cand_s, 1, 0)
            cnt = jnp.sum(sum(parts[1:], parts[0]), axis=0, keepdims=True)
            return jnp.where(cnt >= topk, cand, prefix)
        return lax.fori_loop(0, 32, bit_body, jnp.zeros((1, tq), jnp.int32))

    prefix = lax.switch(variant, [functools.partial(search_all, nc) for nc in range(step, nfull + 1, step)])
    thr = prefix ^ jnp.int32(INT_MIN)

    n_gt = count_where(lambda x, _: x > thr)
    n_eq = count_where(lambda x, _: x == thr)
    need = topk - n_gt
    excess = jnp.where((prefix != 0) & (n_eq > need), 1, 0)
    any_excess = jnp.max(excess) > 0
    idx_bits = int(np.log2(seq))

    def tie_search():
        def tie_bit(b, x):
            cand = x | lax.shift_left(jnp.int32(1), idx_bits - 1 - b)
            below = count_where(lambda xk, kidx: (xk == thr) & (kidx < cand))
            return jnp.where(below < need, cand, x)
        return lax.fori_loop(0, idx_bits, tie_bit, jnp.zeros((1, tq), jnp.int32))

    tie_last = lax.cond(any_excess, tie_search, lambda: jnp.full((1, tq), seq, jnp.int32))

    groups = ATT_KV_HEADS
    rep = ATT_HEADS // ATT_KV_HEADS
    q = q_ref[0]
    q_g = [jnp.concatenate([q[:, (g * rep + r) * LANES:(g * rep + r + 1) * LANES] for r in range(rep)], axis=0)
           for g in range(groups)]

    def qk_dots(half, dst_ref):
        k0 = pl.multiple_of(jnp.minimum(half, last_half) * kh, kh)
        for g in range(groups):
            dst_ref[g] = _dot_nt(k_ref[0, pl.ds(k0, kh), g * LANES:(g + 1) * LANES], q_g[g])

    def softmax_pv(half, src_ref, carry):
        k0 = pl.multiple_of(half * kh, kh)
        x = skey_ref[pl.ds(k0, kh), :]
        kidx = k0 + lax.broadcasted_iota(jnp.int32, (kh, tq), 0)
        sel = (x > thr) | ((x == thr) & (kidx <= tie_last))
        sel = sel & (x != jnp.int32(INT_MIN))
        bias = jnp.where(sel, 0.0, MASK_VALUE)
        bias = jnp.concatenate([bias] * rep, axis=1)
        new = []
        for g in range(groups):
            m_old, l_old, acc_old = carry[g]
            vtg = vt_ref[g * LANES:(g + 1) * LANES, pl.ds(k0, kh)]
            sm = src_ref[g] + bias
            m_new = jnp.maximum(m_old, jnp.max(sm, axis=0, keepdims=True))
            p = jnp.exp2(sm - m_new)
            alpha = jnp.exp2(m_old - m_new)
            l_new = alpha * l_old + jnp.sum(p, axis=0, keepdims=True)
            acc_new = alpha * acc_old + _dot(vtg, p.astype(BF16))
            new.append((m_new, l_new, acc_new))
        return tuple(new)

    def attn_chunk(c, carry):
        qk_dots(2 * c + 1, sb_ref)
        carry = softmax_pv(2 * c, sa_ref, carry)
        qk_dots(2 * c + 2, sa_ref)
        return softmax_pv(2 * c + 1, sb_ref, carry)

    init = tuple((jnp.full((1, rep * tq), MASK_VALUE, F32), jnp.zeros((1, rep * tq), F32),
                  jnp.zeros((LANES, rep * tq), F32)) for _ in range(groups))
    qk_dots(0, sa_ref)
    fin = lax.fori_loop(0, nchunk, attn_chunk, init)
    for g in range(groups):
        _, l_f, acc_f = fin[g]
        o_t = acc_f / l_f
        for r in range(rep):
            h = g * rep + r
            o_ref[0, :, h * LANES:(h + 1) * LANES] = o_t[:, r * tq:(r + 1) * tq].T.astype(o_ref.dtype)


def _dsa_attention(q, k, vt, qi, ki, wit, batch, seq, topk, tq=128, kc=512):
    nblk = seq // tq
    rep = ATT_HEADS // ATT_KV_HEADS
    r3 = lambda a: a.reshape(batch, seq, a.shape[-1])
    qblk = lambda w: pl.BlockSpec((1, tq, w), lambda b, i: (b, i, 0))
    full = lambda w: pl.BlockSpec((1, seq, w), lambda b, i: (b, 0, 0))
    out = pl.pallas_call(
        functools.partial(_attn_kernel, tq=tq, kc=kc, topk=topk, seq=seq),
        grid=(batch, nblk),
        in_specs=[qblk(IDX_Q_W), pl.BlockSpec((IDX_HEADS, tq), lambda b, i: (0, b * nblk + i)),
                  full(IDX_HEAD_DIM), qblk(ATT_Q_W), full(ATT_KV_W),
                  pl.BlockSpec((ATT_KV_W, seq), lambda b, i: (0, b))],
        out_specs=qblk(ATT_Q_W),
        out_shape=jax.ShapeDtypeStruct((batch, seq, ATT_Q_W), BF16),
        scratch_shapes=[pltpu.VMEM((seq, tq), jnp.int32),
                        pltpu.VMEM((kc // 2, IDX_HEADS * tq), F32), pltpu.VMEM((kc // 2, IDX_HEADS * tq), F32),
                        pltpu.VMEM((ATT_KV_HEADS, kc // 2, rep * tq), F32),
                        pltpu.VMEM((ATT_KV_HEADS, kc // 2, rep * tq), F32)],
        compiler_params=_cparams(("arbitrary", "arbitrary")),
        name="dsa_attention",
    )(r3(qi), wit, r3(ki), r3(q), r3(k), vt)
    return out.reshape(batch * seq, ATT_Q_W)


def _conv_kernel(cb_ref, cc_ref, cx_ref, hc_ref, hx_ref, w_ref, o_ref, *, tm, seq):
    i = pl.program_id(0)
    u = cc_ref[...].astype(F32) * cx_ref[...].astype(F32)
    halo = hc_ref[...].astype(F32) * hx_ref[...].astype(F32)
    halo = jnp.where((i * tm) % seq == 0, 0.0, halo)
    row = lax.broadcasted_iota(jnp.int32, u.shape, 0)
    u1 = pltpu.roll(u, 1, axis=0)
    u2 = pltpu.roll(u, 2, axis=0)
    h1 = halo[HALO_ROWS - 1:HALO_ROWS, :]
    h2 = halo[HALO_ROWS - 2:HALO_ROWS - 1, :]
    u1 = jnp.where(row == 0, h1, u1)
    u2 = jnp.where(row == 0, h2, jnp.where(row == 1, h1, u2))
    w = w_ref[...]
    conv = u2 * w[0:1, :] + u1 * w[1:2, :] + u * w[2:3, :]
    o_ref[...] = (cb_ref[...].astype(F32) * conv).astype(o_ref.dtype)


def _short_conv(proj, conv_w, seq, col0, tm=512, tc=512):
    m = proj.shape[0]
    nb = CONV_WIDTH // tc
    c0 = col0 // tc
    rpb = tm // HALO_ROWS
    main = lambda off: pl.BlockSpec((tm, tc), lambda i, j: (i, c0 + off * nb + j))
    halo = lambda off: pl.BlockSpec((HALO_ROWS, tc),
                                    lambda i, j: (jnp.maximum(i * rpb - 1, 0), c0 + off * nb + j))
    return pl.pallas_call(
        functools.partial(_conv_kernel, tm=tm, seq=seq),
        grid=(m // tm, nb),
        in_specs=[main(0), main(1), main(2), halo(1), halo(2),
                  pl.BlockSpec((CONV_K, tc), lambda i, j: (0, j))],
        out_specs=pl.BlockSpec((tm, tc), lambda i, j: (i, j)),
        out_shape=jax.ShapeDtypeStruct((m, CONV_WIDTH), BF16),
        compiler_params=_cparams(("arbitrary", "arbitrary")),
        name="short_conv",
    )(proj, proj, proj, proj, proj, conv_w)


def _hg_tables():
    c = HG_CHUNK
    t = np.arange(c)
    mats, masks = [], []
    for l in range(HG_LEVELS):
        h = 1 << l
        start = (t // (2 * h)) * (2 * h)
        p = start + h - 1
        right = (t - start) >= h
        u = t[None, :]
        r_m = right[:, None] & (u > p[:, None]) & (u <= t[:, None])
        l_m = (~right)[:, None] & (u > t[:, None]) & (u <= p[:, None])
        mats.append(r_m | l_m)
        same = (t[:, None] // (2 * h)) == (t[None, :] // (2 * h))
        masks.append(same & right[:, None] & (~right)[None, :])
    mats.append(t[None, :] <= t[:, None])
    mats.append(t[None, :] > t[:, None])
    mat = np.concatenate(mats, 0).astype(np.float32)
    return np.concatenate([mat] * HG_SPLIT, 1), np.stack(masks).astype(np.float32)


def _split3(x):
    hi = x.astype(BF16)
    r1 = x - hi.astype(F32)
    mid = r1.astype(BF16)
    lo = (r1 - mid.astype(F32)).astype(BF16)
    return jnp.concatenate([hi, mid, lo], axis=0)


def _hg_kernel(hq_ref, hf_ref, hi_ref, hg_ref, lb_ref, g_ref, mat_ref, mask_ref, o_ref, state_ref, *, ts):
    c = HG_CHUNK

    @pl.when(pl.program_id(1) == 0)
    def _():
        state_ref[...] = jnp.zeros_like(state_ref)

    g = g_ref[...]

    def chunk(ci, carry):
        rows = pl.ds(pl.multiple_of(ci * c, c), c)
        lb = lb_ref[...]
        f = lb + (1.0 - lb) * _sigmoid(hf_ref[rows, :])
        logf = jnp.log(jnp.maximum(f, F_MIN))
        kin = 1.0 - f
        q = hq_ref[rows, :].astype(F32)
        v16 = hi_ref[rows, :]
        e = jnp.exp(_dot(mat_ref[...], _split3(logf)))
        qd = [(q * e[l * c:(l + 1) * c]).astype(BF16) for l in range(HG_LEVELS + 1)]
        kd = [(kin * e[l * c:(l + 1) * c]).astype(BF16) for l in range(HG_LEVELS)]
        e_b = e[HG_LEVELS * c:(HG_LEVELS + 1) * c]
        kk = (kin * e[(HG_LEVELS + 1) * c:(HG_LEVELS + 2) * c]).astype(BF16)
        qk = q * kin
        gate = hg_ref[rows, :].astype(F32)
        gate = gate * _sigmoid(gate)
        for h in range(HG_HEADS):
            cols = slice(h * LANES, (h + 1) * LANES)
            attn = jnp.zeros((c, c), F32)
            for l in range(HG_LEVELS):
                attn = attn + _dot_nt(qd[l][:, cols], kd[l][:, cols]) * mask_ref[l]
            state_t = state_ref[h]
            v16h = v16[:, cols]
            diag = jnp.sum(qk[:, cols], axis=1, keepdims=True)
            out = (_dot(attn.astype(BF16), v16h) + diag * v16h.astype(F32)
                   + _dot_nt(qd[HG_LEVELS][:, cols], state_t.astype(BF16)))
            kv_t = lax.dot_general(v16h, kk[:, cols], (((0,), (0,)), ((), ())),
                                   preferred_element_type=F32)
            state_ref[h] = state_t * e_b[c - 1:c, cols] + kv_t
            y = out * lax.rsqrt(jnp.mean(out * out, axis=-1, keepdims=True) + NORM_EPS) * g
            o_ref[rows, cols] = (y * gate[:, cols]).astype(o_ref.dtype)
        return carry

    lax.fori_loop(0, ts // c, chunk, 0)


def _hgrn2(proj, side, lb, norm_g, batch, seq, col0, ts=512):
    m = proj.shape[0]
    c0 = col0 // HG_W
    spb = seq // ts
    mat, masks = _hg_tables()
    col = lambda off: pl.BlockSpec((ts, HG_W), lambda b, s: (b * spb + s, c0 + off))
    return pl.pallas_call(
        functools.partial(_hg_kernel, ts=ts),
        grid=(batch, spb),
        in_specs=[col(0), pl.BlockSpec((ts, HG_W), lambda b, s: (b * spb + s, 0)), col(1), col(2),
                  pl.BlockSpec((1, HG_W), lambda b, s: (0, 0)),
                  pl.BlockSpec((1, LANES), lambda b, s: (0, 0)),
                  pl.BlockSpec(mat.shape, lambda b, s: (0, 0)),
                  pl.BlockSpec(masks.shape, lambda b, s: (0, 0, 0))],
        out_specs=pl.BlockSpec((ts, HG_W), lambda b, s: (b * spb + s, 0)),
        out_shape=jax.ShapeDtypeStruct((m, HG_W), BF16),
        scratch_shapes=[pltpu.VMEM((HG_HEADS, HG_VAL_DIM, HG_KEY_DIM), F32)],
        compiler_params=_cparams(("arbitrary", "arbitrary")),
        name="hgrn2",
    )(proj, side, proj, proj, lb.reshape(1, HG_W), norm_g.reshape(1, HG_VAL_DIM),
      jnp.asarray(mat, BF16), jnp.asarray(masks))


def _lb_kernel(x_ref, o_ref):
    x = x_ref[...]
    e = jnp.exp(x - jnp.max(x, axis=0, keepdims=True))
    p = e / jnp.sum(e, axis=0, keepdims=True)
    depth = x.shape[0]
    run = jnp.zeros_like(p[0:1])
    for l in range(depth):
        run = run + p[l:l + 1]
        o_ref[l:l + 1, :] = run - p[0:1]


def _lower_bounds(hg_lower_bounds):
    return pl.pallas_call(
        _lb_kernel,
        out_shape=jax.ShapeDtypeStruct(hg_lower_bounds.shape, F32),
        name="hg_lower_bounds",
    )(hg_lower_bounds.astype(F32))


_COL_Q = 0
_COL_CONV = ATT_Q_W + 2 * ATT_KV_W + IDX_Q_W
_COL_HG = _COL_CONV + 3 * CONV_WIDTH
_COL_GATE = _COL_HG + 3 * HG_W
_SIDE_SMALL = HG_W
_SIDE_W = HG_W + LANES


def _split_w_in(w_in, d_model):
    sizes = (ATT_Q_W, ATT_KV_W, ATT_KV_W, IDX_Q_W, IDX_HEAD_DIM, IDX_HEADS,
             CONV_WIDTH, CONV_WIDTH, CONV_WIDTH, HG_W, HG_W, HG_W, HG_W, N_BRANCHES * d_model)
    offs = np.cumsum((0,) + sizes)
    part = lambda n: w_in[:, offs[n]:offs[n + 1]]
    main = jnp.concatenate([part(n) for n in (0, 1, 2, 3, 6, 7, 8, 9, 11, 12, 13)], axis=1).astype(BF16)
    side = jnp.concatenate([part(10), part(4), part(5)], axis=1)
    side = jnp.pad(side, ((0, 0), (0, _SIDE_W - side.shape[1]))).astype(BF16)
    return main, side


def kernel(x, c, positions, ada_w, ada_b, norm_mix_g, w_in, conv_w, hg_lower_bounds, hg_norm_g,
           w_o_attn, w_o_conv, w_o_hgrn, w_out, norm_mlp_g, w_mlp1, w_mlp2, final_norm_g):
    batch, seq, d = x.shape
    depth = ada_w.shape[0]
    m = batch * seq
    topk = min(INDEX_TOPK, seq // 4)
    lb_all = _lower_bounds(hg_lower_bounds)
    mod = _modulation(c, ada_w, ada_b)
    tables = _rope_tables(positions)
    x2 = x.reshape(m, d)
    for l in range(depth):
        sh1, sc1, g1, sh2, sc2, g2 = [mod[l, :, n * d:(n + 1) * d] for n in range(ADA_CHUNKS)]
        w_main, w_side = _split_w_in(w_in[l], d)
        h = _norm(x2, norm_mix_g[l], seq, sh1, sc1)
        proj = _matmul(h, w_main, out_dtype=BF16, tn=1024)
        side = _matmul(h, w_side, out_dtype=F32, tn=_SIDE_W)
        q, k, vt, qi, ki, wit = _rope_split(proj, side, tables)
        att = _dsa_attention(q, k, vt, qi, ki, wit, batch, seq, topk)
        cv = _short_conv(proj, conv_w[l], seq, _COL_CONV)
        hg = _hgrn2(proj, side, lb_all[l], hg_norm_g[l], batch, seq, _COL_HG)
        x2, h = _merge_out(att, cv, hg, w_o_attn[l].astype(BF16), w_o_conv[l].astype(BF16),
                           w_o_hgrn[l].astype(BF16), proj, _COL_GATE, w_out[l].astype(BF16), x2, g1,
                           norm_mlp_g[l], sh2, sc2, seq)
        a = _matmul(h, w_mlp1[l].astype(BF16), out_dtype=BF16, epilogue="relu2")
        x2 = _matmul(a, w_mlp2[l].astype(BF16), out_dtype=F32, epilogue="residual", tm=512,
                     res=x2, gate_row=g2, seq=seq)
    out = _norm(x2, final_norm_g, seq, out_dtype=x.dtype)
    return out.reshape(batch, seq, d)
```

```python
import functools

import numpy as np
import jax
import jax.numpy as jnp
from jax import lax
from jax.experimental import pallas as pl
from jax.experimental.pallas import tpu as pltpu

ATT_HEADS = 8
ATT_KV_HEADS = 2
ATT_HEAD_DIM = 128
IDX_HEADS = 8
IDX_HEAD_DIM = 64
INDEX_TOPK = 256
ROPE_THETA = 500000.0
ROPE_FRACTION = 4
MASK_VALUE = -1e30
CONV_WIDTH = 1024
CONV_K = 3
HG_HEADS = 8
HG_KEY_DIM = 128
HG_VAL_DIM = 128
F_MIN = 1e-30
N_BRANCHES = 3
NORM_EPS = 1e-6
ADA_CHUNKS = 6

ATT_Q_W = ATT_HEADS * ATT_HEAD_DIM
ATT_KV_W = ATT_KV_HEADS * ATT_HEAD_DIM
IDX_Q_W = IDX_HEADS * IDX_HEAD_DIM
HG_W = HG_HEADS * HG_KEY_DIM
ATT_LOGIT_SCALE = (ATT_HEAD_DIM ** -0.5) * float(np.log2(np.e))

LANES = 128
SUBLANES = 8
HALO_ROWS = 16
VMEM_LIMIT_BYTES = 56 * 1024 * 1024

INT_MIN = -2 ** 31
SEARCH_VARIANTS = 8
HG_CHUNK = 64
HG_LEVELS = 6
HG_SPLIT = 3

BF16 = jnp.bfloat16
F32 = jnp.float32


def _cparams(sem):
    return pltpu.CompilerParams(dimension_semantics=sem, vmem_limit_bytes=VMEM_LIMIT_BYTES)


def _dot(a, b):
    return jnp.dot(a, b, preferred_element_type=F32)


def _dot_nt(a, b):
    return lax.dot_general(a, b, (((1,), (1,)), ((), ())), preferred_element_type=F32)


def _sigmoid(x):
    return 1.0 / (1.0 + jnp.exp(-x))


def _mod_kernel(c_ref, w_ref, b_ref, o_ref):
    c = c_ref[...]
    ca = (c * _sigmoid(c)).astype(BF16)
    o_ref[...] = _dot(ca, w_ref[...].astype(BF16)) + b_ref[...]


def _modulation(c, ada_w, ada_b):
    depth, d, n = ada_w.shape
    b = c.shape[0]
    bp = -(-b // SUBLANES) * SUBLANES
    cp = jnp.pad(c, ((0, bp - b), (0, 0)))
    tn = 1024
    out = pl.pallas_call(
        _mod_kernel,
        grid=(depth, n // tn),
        in_specs=[
            pl.BlockSpec((bp, d), lambda l, j: (0, 0)),
            pl.BlockSpec((None, d, tn), lambda l, j: (l, 0, j)),
            pl.BlockSpec((None, 1, tn), lambda l, j: (l, 0, j)),
        ],
        out_specs=pl.BlockSpec((None, bp, tn), lambda l, j: (l, 0, j)),
        out_shape=jax.ShapeDtypeStruct((depth, bp, n), F32),
        compiler_params=_cparams(("arbitrary", "arbitrary")),
        name="adaln_mod",
    )(cp, ada_w, ada_b.reshape(depth, 1, n))
    return out[:, :b]


def _norm_kernel(x_ref, g_ref, *rest, modulate):
    if modulate:
        sh_ref, sc_ref, o_ref = rest
    else:
        (o_ref,) = rest
    x = x_ref[...]
    y = x * lax.rsqrt(jnp.mean(x * x, axis=-1, keepdims=True) + NORM_EPS) * g_ref[...]
    if modulate:
        y = y * (1.0 + sc_ref[0]) + sh_ref[0]
    o_ref[...] = y.astype(o_ref.dtype)


def _norm(x2, g, seq, shift=None, scale=None, out_dtype=BF16, tm=512):
    m, d = x2.shape
    modulate = shift is not None
    tpb = seq // tm
    in_specs = [pl.BlockSpec((tm, d), lambda i: (i, 0)),
                pl.BlockSpec((1, d), lambda i: (0, 0))]
    args = [x2, g.reshape(1, d)]
    if modulate:
        row = pl.BlockSpec((1, 1, d), lambda i: (i // tpb, 0, 0))
        in_specs += [row, row]
        args += [shift[:, None, :], scale[:, None, :]]
    return pl.pallas_call(
        functools.partial(_norm_kernel, modulate=modulate),
        grid=(m // tm,),
        in_specs=in_specs,
        out_specs=pl.BlockSpec((tm, d), lambda i: (i, 0)),
        out_shape=jax.ShapeDtypeStruct((m, d), out_dtype),
        compiler_params=_cparams(("arbitrary",)),
        name="rmsnorm",
    )(*args)


def _merge_kernel(att_ref, cv_ref, hg_ref, wa_ref, wc_ref, wh_ref, ga_ref, gc_ref, gh_ref,
                  wo_ref, x_ref, gr_ref, ng_ref, sh_ref, sc_ref, o_ref, h_ref):
    merged = (_sigmoid(ga_ref[...].astype(F32)) * _dot(att_ref[...], wa_ref[...])
              + _sigmoid(gc_ref[...].astype(F32)) * _dot(cv_ref[...], wc_ref[...])
              + _sigmoid(gh_ref[...].astype(F32)) * _dot(hg_ref[...], wh_ref[...]))
    y = x_ref[...] + gr_ref[0] * _dot(merged.astype(BF16), wo_ref[...])
    o_ref[...] = y
    yn = y * lax.rsqrt(jnp.mean(y * y, axis=-1, keepdims=True) + NORM_EPS) * ng_ref[...]
    h_ref[...] = (yn * (1.0 + sc_ref[0]) + sh_ref[0]).astype(h_ref.dtype)


def _merge_out(att, cv, hg, wa, wc, wh, proj, gate_off, wout, x2, gate_row, norm_g, shift, scale, seq, tm=256):
    m, d = x2.shape
    tpb = seq // tm
    g0 = gate_off // d
    act = lambda a: pl.BlockSpec((tm, a.shape[1]), lambda i: (i, 0))
    resident = lambda w: pl.BlockSpec(w.shape, lambda i: (0, 0), pipeline_mode=pl.Buffered(1))
    gate = lambda b: pl.BlockSpec((tm, d), lambda i: (i, g0 + b))
    row = pl.BlockSpec((1, 1, d), lambda i: (i // tpb, 0, 0))
    full = pl.BlockSpec((tm, d), lambda i: (i, 0))
    return pl.pallas_call(
        _merge_kernel,
        grid=(m // tm,),
        in_specs=[act(att), act(cv), act(hg), resident(wa), resident(wc), resident(wh),
                  gate(0), gate(1), gate(2), resident(wout), full, row,
                  pl.BlockSpec((1, d), lambda i: (0, 0)), row, row],
        out_specs=[full, full],
        out_shape=[jax.ShapeDtypeStruct((m, d), F32), jax.ShapeDtypeStruct((m, d), BF16)],
        compiler_params=_cparams(("arbitrary",)),
        name="merge_out",
    )(att, cv, hg, wa, wc, wh, proj, proj, proj, wout, x2, gate_row[:, None, :],
      norm_g.reshape(1, d), shift[:, None, :], scale[:, None, :])


def _mm_kernel(*refs, epilogue, nk):
    a_ref, w_ref = refs[0], refs[1]
    pos = 2
    extra = []
    n_extra = {"none": 0, "relu2": 0, "residual": 2}[epilogue]
    for _ in range(n_extra):
        extra.append(refs[pos])
        pos += 1
    o_ref = refs[pos]
    acc_ref = refs[pos + 1] if nk > 1 else None

    def finish(y):
        if epilogue == "relu2":
            y = jnp.square(jnp.maximum(y, 0.0))
        elif epilogue == "residual":
            y = extra[0][...] + extra[1][0] * y
        o_ref[...] = y.astype(o_ref.dtype)

    if nk == 1:
        finish(_dot(a_ref[...], w_ref[...]))
    else:
        k = pl.program_id(2)

        @pl.when(k == 0)
        def _():
            acc_ref[...] = jnp.zeros_like(acc_ref)

        acc_ref[...] += _dot(a_ref[...], w_ref[...])

        @pl.when(k == nk - 1)
        def _():
            finish(acc_ref[...])


def _matmul(a, w, *, out_dtype, epilogue="none", tm=1024, tn=512, tk=None,
            res=None, gate_row=None, seq=None):
    m = a.shape[0]
    kdim, n = w.shape
    tk = kdim if tk is None else tk
    nk = kdim // tk
    tn = min(tn, n)
    w_mode = pl.Buffered(1) if (tn == n and nk == 1) else None
    in_specs = [pl.BlockSpec((tm, tk), lambda i, j, k: (i, k)),
                pl.BlockSpec((tk, tn), lambda i, j, k: (k, j), pipeline_mode=w_mode)]
    args = [a, w]
    if epilogue == "residual":
        tpb = seq // tm
        in_specs += [pl.BlockSpec((tm, tn), lambda i, j, k: (i, j)),
                     pl.BlockSpec((1, 1, tn), lambda i, j, k: (i // tpb, 0, j))]
        args += [res, gate_row[:, None, :]]
    scratch = [pltpu.VMEM((tm, tn), F32)] if nk > 1 else []
    return pl.pallas_call(
        functools.partial(_mm_kernel, epilogue=epilogue, nk=nk),
        grid=(m // tm, n // tn, nk),
        in_specs=in_specs,
        out_specs=pl.BlockSpec((tm, tn), lambda i, j, k: (i, j)),
        out_shape=jax.ShapeDtypeStruct((m, n), out_dtype),
        scratch_shapes=scratch,
        compiler_params=_cparams(("arbitrary", "arbitrary", "arbitrary")),
        name="matmul_" + epilogue,
    )(*args)


def _rope_rows(period, half, rot):
    j = np.arange(LANES) % period
    inv = ROPE_THETA ** (-(np.arange(half, dtype=np.float32)) / np.float32(half))
    inv_row = np.where(j < rot, inv.astype(np.float32)[j % half], 0.0).astype(np.float32)
    sign_row = np.where(j < half, -1.0, np.where(j < rot, 1.0, 0.0)).astype(np.float32)
    first_row = (j < half).astype(np.float32)
    return np.stack([inv_row, sign_row, first_row])[:, None, :]


def _rope_apply(x, cos_t, sin_s, first, half):
    up = pltpu.roll(x, LANES - half, axis=1)
    dn = pltpu.roll(x, half, axis=1)
    sw = jnp.where(first > 0.5, up, dn)
    return x * cos_t + sw * sin_s


def _rope_table_kernel(pos_ref, rows_a_ref, rows_i_ref, ca_ref, sa_ref, ci_ref, si_ref):
    pos = pos_ref[...].astype(F32)
    ang_a = pos * rows_a_ref[0]
    ca_ref[...] = jnp.cos(ang_a)
    sa_ref[...] = jnp.sin(ang_a) * rows_a_ref[1]
    ang_i = pos * rows_i_ref[0]
    ci_ref[...] = jnp.cos(ang_i)
    si_ref[...] = jnp.sin(ang_i) * rows_i_ref[1]


def _rope_consts():
    rows_a = jnp.asarray(_rope_rows(ATT_HEAD_DIM, ATT_HEAD_DIM // ROPE_FRACTION // 2,
                                    ATT_HEAD_DIM // ROPE_FRACTION))
    rows_i = jnp.asarray(_rope_rows(IDX_HEAD_DIM, IDX_HEAD_DIM // ROPE_FRACTION // 2,
                                    IDX_HEAD_DIM // ROPE_FRACTION))
    return rows_a, rows_i


def _rope_tables(positions, tm=512):
    m = positions.size
    rows_a, rows_i = _rope_consts()
    rows_spec = pl.BlockSpec((3, 1, LANES), lambda i: (0, 0, 0))
    tab = pl.BlockSpec((tm, LANES), lambda i: (i, 0))
    return pl.pallas_call(
        _rope_table_kernel,
        grid=(m // tm,),
        in_specs=[pl.BlockSpec((tm, 1), lambda i: (i, 0)), rows_spec, rows_spec],
        out_specs=[tab] * 4,
        out_shape=[jax.ShapeDtypeStruct((m, LANES), F32)] * 4,
        compiler_params=_cparams(("arbitrary",)),
        name="rope_tables",
    )(positions.reshape(m, 1), rows_a, rows_i)


def _rope_kernel(ca_ref, sa_ref, ci_ref, si_ref, rows_a_ref, rows_i_ref, q_ref, kv_ref, qi_ref, sm_ref,
                 qo_ref, ko_ref, vo_ref, qio_ref, kio_ref, wio_ref, *, w_scale):
    cos_a, sin_a, first_a = ca_ref[...], sa_ref[...], rows_a_ref[2]
    cos_i, sin_i, first_i = ci_ref[...], si_ref[...], rows_i_ref[2]
    half_a = ATT_HEAD_DIM // ROPE_FRACTION // 2
    half_i = IDX_HEAD_DIM // ROPE_FRACTION // 2
    for h in range(ATT_HEADS):
        sl = slice(h * LANES, (h + 1) * LANES)
        qo_ref[:, sl] = (_rope_apply(q_ref[:, sl].astype(F32), cos_a, sin_a, first_a, half_a)
                         * ATT_LOGIT_SCALE).astype(BF16)
    for h in range(ATT_KV_HEADS):
        sl = slice(h * LANES, (h + 1) * LANES)
        ko_ref[:, sl] = _rope_apply(kv_ref[:, sl].astype(F32), cos_a, sin_a, first_a, half_a).astype(BF16)
    vo_ref[...] = kv_ref[:, ATT_KV_W:].astype(F32).T.astype(BF16)
    for h in range(IDX_Q_W // LANES):
        sl = slice(h * LANES, (h + 1) * LANES)
        qio_ref[:, sl] = _rope_apply(qi_ref[:, sl].astype(F32), cos_i, sin_i, first_i, half_i).astype(BF16)
    sm = sm_ref[...]
    ki = _rope_apply(sm, cos_i, sin_i, first_i, half_i)
    kio_ref[...] = ki[:, :IDX_HEAD_DIM].astype(BF16)
    wio_ref[...] = sm.T[IDX_HEAD_DIM:IDX_HEAD_DIM + IDX_HEADS, :] * w_scale


def _rope_split(proj, side, tables, tm=512):
    m = proj.shape[0]
    rows_a, rows_i = _rope_consts()
    w_scale = (IDX_HEADS ** -0.5) * (IDX_HEAD_DIM ** -0.5)
    rows_spec = pl.BlockSpec((3, 1, LANES), lambda i: (0, 0, 0))
    tab = pl.BlockSpec((tm, LANES), lambda i: (i, 0))
    outs = pl.pallas_call(
        functools.partial(_rope_kernel, w_scale=w_scale),
        grid=(m // tm,),
        in_specs=[
            tab, tab, tab, tab,
            rows_spec, rows_spec,
            pl.BlockSpec((tm, ATT_Q_W), lambda i: (i, 0)),
            pl.BlockSpec((tm, 2 * ATT_KV_W), lambda i: (i, ATT_Q_W // (2 * ATT_KV_W))),
            pl.BlockSpec((tm, IDX_Q_W), lambda i: (i, (ATT_Q_W + 2 * ATT_KV_W) // IDX_Q_W)),
            pl.BlockSpec((tm, LANES), lambda i: (i, _SIDE_SMALL // LANES)),
        ],
        out_specs=[
            pl.BlockSpec((tm, ATT_Q_W), lambda i: (i, 0)),
            pl.BlockSpec((tm, ATT_KV_W), lambda i: (i, 0)),
            pl.BlockSpec((ATT_KV_W, tm), lambda i: (0, i)),
            pl.BlockSpec((tm, IDX_Q_W), lambda i: (i, 0)),
            pl.BlockSpec((tm, IDX_HEAD_DIM), lambda i: (i, 0)),
            pl.BlockSpec((IDX_HEADS, tm), lambda i: (0, i)),
        ],
        out_shape=[
            jax.ShapeDtypeStruct((m, ATT_Q_W), BF16),
            jax.ShapeDtypeStruct((m, ATT_KV_W), BF16),
            jax.ShapeDtypeStruct((ATT_KV_W, m), BF16),
            jax.ShapeDtypeStruct((m, IDX_Q_W), BF16),
            jax.ShapeDtypeStruct((m, IDX_HEAD_DIM), BF16),
            jax.ShapeDtypeStruct((IDX_HEADS, m), F32),
        ],
        compiler_params=_cparams(("arbitrary",)),
        name="rope_split",
    )(*tables, rows_a, rows_i, proj, proj, proj, side)
    return outs


def _attn_kernel(qi_ref, wi_ref, ki_ref, q_ref, k_ref, vt_ref, o_ref, skey_ref,
                 da_ref, db_ref, sa_ref, sb_ref, *, tq, kc, topk, seq):
    i = pl.program_id(1)
    nchunk = lax.shift_right_logical((i + 1) * tq + (kc - 1), int(np.log2(kc)))
    qpos = i * tq + lax.broadcasted_iota(jnp.int32, (1, tq), 1)
    rb = 8 * SUBLANES

    qi = qi_ref[0]
    wi = wi_ref[...]
    qi_s = jnp.concatenate([qi[:, h * IDX_HEAD_DIM:(h + 1) * IDX_HEAD_DIM] for h in range(IDX_HEADS)],
                           axis=0)

    kh = kc // 2
    last_half = seq // kh - 1

    def score_dots(half, dst_ref):
        k0 = pl.multiple_of(jnp.minimum(half, last_half) * kh, kh)
        dst_ref[...] = _dot_nt(ki_ref[0, pl.ds(k0, kh), :], qi_s)

    def score_keys(half, src_ref):
        k0 = pl.multiple_of(half * kh, kh)
        for j in range(kh // rb):
            rows = slice(j * rb, (j + 1) * rb)
            acc = jnp.zeros((rb, tq), F32)
            for h in range(IDX_HEADS):
                acc = acc + jnp.maximum(src_ref[rows, h * tq:(h + 1) * tq], 0.0) * wi[h:h + 1, :]
            kpos = k0 + j * rb + lax.broadcasted_iota(jnp.int32, (rb, tq), 0)
            bits = pltpu.bitcast(acc, jnp.int32)
            skey = jnp.where(bits < 0, bits ^ jnp.int32(0x7FFFFFFF), bits)
            skey = jnp.where(kpos <= qpos, skey, jnp.int32(INT_MIN))
            skey_ref[pl.ds(k0 + j * rb, rb), :] = skey

    def score_chunk(c, carry):
        score_dots(2 * c + 1, db_ref)
        score_keys(2 * c, da_ref)
        score_dots(2 * c + 2, da_ref)
        score_keys(2 * c + 1, db_ref)
        return carry

    score_dots(0, da_ref)
    lax.fori_loop(0, nchunk, score_chunk, 0)

    nacc = 4

    def count_where(pred_fn):
        row_iota = lax.broadcasted_iota(jnp.int32, (SUBLANES, tq), 0)

        def chunk_body(c, parts):
            k0 = pl.multiple_of(c * kc, kc)
            parts = list(parts)
            xs = skey_ref[pl.ds(k0, kc), :]
            for j in range(kc // SUBLANES):
                x = xs[j * SUBLANES:(j + 1) * SUBLANES]
                parts[j % nacc] = parts[j % nacc] + jnp.where(pred_fn(x, k0 + j * SUBLANES + row_iota), 1, 0)
            return tuple(parts)
        zero = jnp.zeros((SUBLANES, tq), jnp.int32)
        parts = lax.fori_loop(0, nchunk, chunk_body, (zero,) * nacc)
        return jnp.sum(sum(parts[1:], parts[0]), axis=0, keepdims=True)

    nfull = seq // kc
    step = max(nfull // SEARCH_VARIANTS, 1)
    variant = lax.div(nchunk + (step - 1), step) - 1

    def fill_chunk(c, carry):
        rows = pl.ds(pl.multiple_of(c * kc, kc), kc)
        skey_ref[rows, :] = jnp.full((kc, tq), INT_MIN, jnp.int32)
        return carry

    lax.fori_loop(nchunk, (variant + 1) * step, fill_chunk, 0)

    def search_all(nc):
        def bit_body(b, prefix):
            cand = prefix | lax.shift_left(jnp.int32(1), 31 - b)
            cand_s = jnp.broadcast_to(cand ^ jnp.int32(INT_MIN), (SUBLANES, tq))
            parts = [jnp.zeros((SUBLANES, tq), jnp.int32)] * nacc
            for j in range(nc * kc // SUBLANES):
                x = skey_ref[j * SUBLANES:(j + 1) * SUBLANES, :]
                parts[j % nacc] = parts[j % nacc] + jnp.where(x >= cand_s, 1, 0)
            cnt = jnp.sum(sum(parts[1:], parts[0]), axis=0, keepdims=True)
            return jnp.where(cnt >= topk, cand, prefix)
        return lax.fori_loop(0, 32, bit_body, jnp.zeros((1, tq), jnp.int32))

    prefix = lax.switch(variant, [functools.partial(search_all, nc) for nc in range(step, nfull + 1, step)])
    thr = prefix ^ jnp.int32(INT_MIN)

    n_gt = count_where(lambda x, _: x > thr)
    n_eq = count_where(lambda x, _: x == thr)
    need = topk - n_gt
    excess = jnp.where((prefix != 0) & (n_eq > need), 1, 0)
    any_excess = jnp.max(excess) > 0
    idx_bits = int(np.log2(seq))

    def tie_search():
        def tie_bit(b, x):
            cand = x | lax.shift_left(jnp.int32(1), idx_bits - 1 - b)
            below = count_where(lambda xk, kidx: (xk == thr) & (kidx < cand))
            return jnp.where(below < need, cand, x)
        return lax.fori_loop(0, idx_bits, tie_bit, jnp.zeros((1, tq), jnp.int32))

    tie_last = lax.cond(any_excess, tie_search, lambda: jnp.full((1, tq), seq, jnp.int32))

    groups = ATT_KV_HEADS
    rep = ATT_HEADS // ATT_KV_HEADS
    q = q_ref[0]
    q_g = [jnp.concatenate([q[:, (g * rep + r) * LANES:(g * rep + r + 1) * LANES] for r in range(rep)], axis=0)
           for g in range(groups)]

    def qk_dots(half, dst_ref):
        k0 = pl.multiple_of(jnp.minimum(half, last_half) * kh, kh)
        for g in range(groups):
            dst_ref[g] = _dot_nt(k_ref[0, pl.ds(k0, kh), g * LANES:(g + 1) * LANES], q_g[g])

    def softmax_pv(half, src_ref, carry):
        k0 = pl.multiple_of(half * kh, kh)
        x = skey_ref[pl.ds(k0, kh), :]
        kidx = k0 + lax.broadcasted_iota(jnp.int32, (kh, tq), 0)
        sel = (x > thr) | ((x == thr) & (kidx <= tie_last))
        sel = sel & (x != jnp.int32(INT_MIN))
        bias = jnp.where(sel, 0.0, MASK_VALUE)
        bias = jnp.concatenate([bias] * rep, axis=1)
        new = []
        for g in range(groups):
            m_old, l_old, acc_old = carry[g]
            vtg = vt_ref[g * LANES:(g + 1) * LANES, pl.ds(k0, kh)]
            sm = src_ref[g] + bias
            m_new = jnp.maximum(m_old, jnp.max(sm, axis=0, keepdims=True))
            p = jnp.exp2(sm - m_new)
            alpha = jnp.exp2(m_old - m_new)
            l_new = alpha * l_old + jnp.sum(p, axis=0, keepdims=True)
            acc_new = alpha * acc_old + _dot(vtg, p.astype(BF16))
            new.append((m_new, l_new, acc_new))
        return tuple(new)

    def attn_chunk(c, carry):
        qk_dots(2 * c + 1, sb_ref)
        carry = softmax_pv(2 * c, sa_ref, carry)
        qk_dots(2 * c + 2, sa_ref)
        return softmax_pv(2 * c + 1, sb_ref, carry)

    init = tuple((jnp.full((1, rep * tq), MASK_VALUE, F32), jnp.zeros((1, rep * tq), F32),
                  jnp.zeros((LANES, rep * tq), F32)) for _ in range(groups))
    qk_dots(0, sa_ref)
    fin = lax.fori_loop(0, nchunk, attn_chunk, init)
    for g in range(groups):
        _, l_f, acc_f = fin[g]
        o_t = acc_f / l_f
        for r in range(rep):
            h = g * rep + r
            o_ref[0, :, h * LANES:(h + 1) * LANES] = o_t[:, r * tq:(r + 1) * tq].T.astype(o_ref.dtype)


def _dsa_attention(q, k, vt, qi, ki, wit, batch, seq, topk, tq=128, kc=512):
    nblk = seq // tq
    rep = ATT_HEADS // ATT_KV_HEADS
    r3 = lambda a: a.reshape(batch, seq, a.shape[-1])
    qblk = lambda w: pl.BlockSpec((1, tq, w), lambda b, i: (b, i, 0))
    full = lambda w: pl.BlockSpec((1, seq, w), lambda b, i: (b, 0, 0))
    out = pl.pallas_call(
        functools.partial(_attn_kernel, tq=tq, kc=kc, topk=topk, seq=seq),
        grid=(batch, nblk),
        in_specs=[qblk(IDX_Q_W), pl.BlockSpec((IDX_HEADS, tq), lambda b, i: (0, b * nblk + i)),
                  full(IDX_HEAD_DIM), qblk(ATT_Q_W), full(ATT_KV_W),
                  pl.BlockSpec((ATT_KV_W, seq), lambda b, i: (0, b))],
        out_specs=qblk(ATT_Q_W),
        out_shape=jax.ShapeDtypeStruct((batch, seq, ATT_Q_W), BF16),
        scratch_shapes=[pltpu.VMEM((seq, tq), jnp.int32),
                        pltpu.VMEM((kc // 2, IDX_HEADS * tq), F32), pltpu.VMEM((kc // 2, IDX_HEADS * tq), F32),
                        pltpu.VMEM((ATT_KV_HEADS, kc // 2, rep * tq), F32),
                        pltpu.VMEM((ATT_KV_HEADS, kc // 2, rep * tq), F32)],
        compiler_params=_cparams(("arbitrary", "arbitrary")),
        name="dsa_attention",
    )(r3(qi), wit, r3(ki), r3(q), r3(k), vt)
    return out.reshape(batch * seq, ATT_Q_W)


def _conv_kernel(cb_ref, cc_ref, cx_ref, hc_ref, hx_ref, w_ref, o_ref, *, tm, seq):
    i = pl.program_id(0)
    u = cc_ref[...].astype(F32) * cx_ref[...].astype(F32)
    halo = hc_ref[...].astype(F32) * hx_ref[...].astype(F32)
    halo = jnp.where((i * tm) % seq == 0, 0.0, halo)
    row = lax.broadcasted_iota(jnp.int32, u.shape, 0)
    u1 = pltpu.roll(u, 1, axis=0)
    u2 = pltpu.roll(u, 2, axis=0)
    h1 = halo[HALO_ROWS - 1:HALO_ROWS, :]
    h2 = halo[HALO_ROWS - 2:HALO_ROWS - 1, :]
    u1 = jnp.where(row == 0, h1, u1)
    u2 = jnp.where(row == 0, h2, jnp.where(row == 1, h1, u2))
    w = w_ref[...]
    conv = u2 * w[0:1, :] + u1 * w[1:2, :] + u * w[2:3, :]
    o_ref[...] = (cb_ref[...].astype(F32) * conv).astype(o_ref.dtype)


def _short_conv(proj, conv_w, seq, col0, tm=512, tc=512):
    m = proj.shape[0]
    nb = CONV_WIDTH // tc
    c0 = col0 // tc
    rpb = tm // HALO_ROWS
    main = lambda off: pl.BlockSpec((tm, tc), lambda i, j: (i, c0 + off * nb + j))
    halo = lambda off: pl.BlockSpec((HALO_ROWS, tc),
                                    lambda i, j: (jnp.maximum(i * rpb - 1, 0), c0 + off * nb + j))
    return pl.pallas_call(
        functools.partial(_conv_kernel, tm=tm, seq=seq),
        grid=(m // tm, nb),
        in_specs=[main(0), main(1), main(2), halo(1), halo(2),
                  pl.BlockSpec((CONV_K, tc), lambda i, j: (0, j))],
        out_specs=pl.BlockSpec((tm, tc), lambda i, j: (i, j)),
        out_shape=jax.ShapeDtypeStruct((m, CONV_WIDTH), BF16),
        compiler_params=_cparams(("arbitrary", "arbitrary")),
        name="short_conv",
    )(proj, proj, proj, proj, proj, conv_w)


def _hg_tables():
    c = HG_CHUNK
    t = np.arange(c)
    mats, masks = [], []
    for l in range(HG_LEVELS):
        h = 1 << l
        start = (t // (2 * h)) * (2 * h)
        p = start + h - 1
        right = (t - start) >= h
        u = t[None, :]
        r_m = right[:, None] & (u > p[:, None]) & (u <= t[:, None])
        l_m = (~right)[:, None] & (u > t[:, None]) & (u <= p[:, None])
        mats.append(r_m | l_m)
        same = (t[:, None] // (2 * h)) == (t[None, :] // (2 * h))
        masks.append(same & right[:, None] & (~right)[None, :])
    mats.append(t[None, :] <= t[:, None])
    mats.append(t[None, :] > t[:, None])
    mat = np.concatenate(mats, 0).astype(np.float32)
    return np.concatenate([mat] * HG_SPLIT, 1), np.stack(masks).astype(np.float32)


def _split3(x):
    hi = x.astype(BF16)
    r1 = x - hi.astype(F32)
    mid = r1.astype(BF16)
    lo = (r1 - mid.astype(F32)).astype(BF16)
    return jnp.concatenate([hi, mid, lo], axis=0)


def _hg_kernel(hq_ref, hf_ref, hi_ref, hg_ref, lb_ref, g_ref, mat_ref, mask_ref, o_ref, state_ref, *, ts):
    c = HG_CHUNK

    @pl.when(pl.program_id(1) == 0)
    def _():
        state_ref[...] = jnp.zeros_like(state_ref)

    g = g_ref[...]

    def chunk(ci, carry):
        rows = pl.ds(pl.multiple_of(ci * c, c), c)
        lb = lb_ref[...]
        f = lb + (1.0 - lb) * _sigmoid(hf_ref[rows, :])
        logf = jnp.log(jnp.maximum(f, F_MIN))
        kin = 1.0 - f
        q = hq_ref[rows, :].astype(F32)
        v16 = hi_ref[rows, :]
        e = jnp.exp(_dot(mat_ref[...], _split3(logf)))
        qd = [(q * e[l * c:(l + 1) * c]).astype(BF16) for l in range(HG_LEVELS + 1)]
        kd = [(kin * e[l * c:(l + 1) * c]).astype(BF16) for l in range(HG_LEVELS)]
        e_b = e[HG_LEVELS * c:(HG_LEVELS + 1) * c]
        kk = (kin * e[(HG_LEVELS + 1) * c:(HG_LEVELS + 2) * c]).astype(BF16)
        qk = q * kin
        gate = hg_ref[rows, :].astype(F32)
        gate = gate * _sigmoid(gate)
        for h in range(HG_HEADS):
            cols = slice(h * LANES, (h + 1) * LANES)
            attn = jnp.zeros((c, c), F32)
            for l in range(HG_LEVELS):
                attn = attn + _dot_nt(qd[l][:, cols], kd[l][:, cols]) * mask_ref[l]
            state_t = state_ref[h]
            v16h = v16[:, cols]
            diag = jnp.sum(qk[:, cols], axis=1, keepdims=True)
            out = (_dot(attn.astype(BF16), v16h) + diag * v16h.astype(F32)
                   + _dot_nt(qd[HG_LEVELS][:, cols], state_t.astype(BF16)))
            kv_t = lax.dot_general(v16h, kk[:, cols], (((0,), (0,)), ((), ())),
                                   preferred_element_type=F32)
            state_ref[h] = state_t * e_b[c - 1:c, cols] + kv_t
            y = out * lax.rsqrt(jnp.mean(out * out, axis=-1, keepdims=True) + NORM_EPS) * g
            o_ref[rows, cols] = (y * gate[:, cols]).astype(o_ref.dtype)
        return carry

    lax.fori_loop(0, ts // c, chunk, 0)


def _hgrn2(proj, side, lb, norm_g, batch, seq, col0, ts=512):
    m = proj.shape[0]
    c0 = col0 // HG_W
    spb = seq // ts
    mat, masks = _hg_tables()
    col = lambda off: pl.BlockSpec((ts, HG_W), lambda b, s: (b * spb + s, c0 + off))
    return pl.pallas_call(
        functools.partial(_hg_kernel, ts=ts),
        grid=(batch, spb),
        in_specs=[col(0), pl.BlockSpec((ts, HG_W), lambda b, s: (b * spb + s, 0)), col(1), col(2),
                  pl.BlockSpec((1, HG_W), lambda b, s: (0, 0)),
                  pl.BlockSpec((1, LANES), lambda b, s: (0, 0)),
                  pl.BlockSpec(mat.shape, lambda b, s: (0, 0)),
                  pl.BlockSpec(masks.shape, lambda b, s: (0, 0, 0))],
        out_specs=pl.BlockSpec((ts, HG_W), lambda b, s: (b * spb + s, 0)),
        out_shape=jax.ShapeDtypeStruct((m, HG_W), BF16),
        scratch_shapes=[pltpu.VMEM((HG_HEADS, HG_VAL_DIM, HG_KEY_DIM), F32)],
        compiler_params=_cparams(("arbitrary", "arbitrary")),
        name="hgrn2",
    )(proj, side, proj, proj, lb.reshape(1, HG_W), norm_g.reshape(1, HG_VAL_DIM),
      jnp.asarray(mat, BF16), jnp.asarray(masks))


def _lb_kernel(x_ref, o_ref):
    x = x_ref[...]
    e = jnp.exp(x - jnp.max(x, axis=0, keepdims=True))
    p = e / jnp.sum(e, axis=0, keepdims=True)
    depth = x.shape[0]
    run = jnp.zeros_like(p[0:1])
    for l in range(depth):
        run = run + p[l:l + 1]
        o_ref[l:l + 1, :] = run - p[0:1]


def _lower_bounds(hg_lower_bounds):
    return pl.pallas_call(
        _lb_kernel,
        out_shape=jax.ShapeDtypeStruct(hg_lower_bounds.shape, F32),
        name="hg_lower_bounds",
    )(hg_lower_bounds.astype(F32))


_COL_Q = 0
_COL_CONV = ATT_Q_W + 2 * ATT_KV_W + IDX_Q_W
_COL_HG = _COL_CONV + 3 * CONV_WIDTH
_COL_GATE = _COL_HG + 3 * HG_W
_SIDE_SMALL = HG_W
_SIDE_W = HG_W + LANES


def _split_w_in(w_in, d_model):
    sizes = (ATT_Q_W, ATT_KV_W, ATT_KV_W, IDX_Q_W, IDX_HEAD_DIM, IDX_HEADS,
             CONV_WIDTH, CONV_WIDTH, CONV_WIDTH, HG_W, HG_W, HG_W, HG_W, N_BRANCHES * d_model)
    offs = np.cumsum((0,) + sizes)
    part = lambda n: w_in[:, offs[n]:offs[n + 1]]
    main = jnp.concatenate([part(n) for n in (0, 1, 2, 3, 6, 7, 8, 9, 11, 12, 13)], axis=1).astype(BF16)
    side = jnp.concatenate([part(10), part(4), part(5)], axis=1)
    side = jnp.pad(side, ((0, 0), (0, _SIDE_W - side.shape[1]))).astype(BF16)
    return main, side


def kernel(x, c, positions, ada_w, ada_b, norm_mix_g, w_in, conv_w, hg_lower_bounds, hg_norm_g,
           w_o_attn, w_o_conv, w_o_hgrn, w_out, norm_mlp_g, w_mlp1, w_mlp2, final_norm_g):
    batch, seq, d = x.shape
    depth = ada_w.shape[0]
    m = batch * seq
    topk = min(INDEX_TOPK, seq // 4)
    lb_all = _lower_bounds(hg_lower_bounds)
    mod = _modulation(c, ada_w, ada_b)
    tables = _rope_tables(positions)
    x2 = x.reshape(m, d)
    for l in range(depth):
        sh1, sc1, g1, sh2, sc2, g2 = [mod[l, :, n * d:(n + 1) * d] for n in range(ADA_CHUNKS)]
        w_main, w_side = _split_w_in(w_in[l], d)
        h = _norm(x2, norm_mix_g[l], seq, sh1, sc1)
        proj = _matmul(h, w_main, out_dtype=BF16, tn=1024)
        side = _matmul(h, w_side, out_dtype=F32, tn=_SIDE_W)
        q, k, vt, qi, ki, wit = _rope_split(proj, side, tables)
        att = _dsa_attention(q, k, vt, qi, ki, wit, batch, seq, topk)
        cv = _short_conv(proj, conv_w[l], seq, _COL_CONV)
        hg = _hgrn2(proj, side, lb_all[l], hg_norm_g[l], batch, seq, _COL_HG)
        x2, h = _merge_out(att, cv, hg, w_o_attn[l].astype(BF16), w_o_conv[l].astype(BF16),
                           w_o_hgrn[l].astype(BF16), proj, _COL_GATE, w_out[l].astype(BF16), x2, g1,
                           norm_mlp_g[l], sh2, sc2, seq)
        a = _matmul(h, w_mlp1[l].astype(BF16), out_dtype=BF16, epilogue="relu2", tm=256, tn=4 * d)
        x2 = _matmul(a, w_mlp2[l].astype(BF16), out_dtype=F32, epilogue="residual", tm=256, tn=d,
                     res=x2, gate_row=g2, seq=seq)
    out = _norm(x2, final_norm_g, seq, out_dtype=x.dtype)
    return out.reshape(batch, seq, d)
```

```python
import functools

import numpy as np
import jax
import jax.numpy as jnp
from jax import lax
from jax.experimental import pallas as pl
from jax.experimental.pallas import tpu as pltpu

ATT_HEADS = 8
ATT_KV_HEADS = 2
ATT_HEAD_DIM = 128
IDX_HEADS = 8
IDX_HEAD_DIM = 64
INDEX_TOPK = 256
ROPE_THETA = 500000.0
ROPE_FRACTION = 4
MASK_VALUE = -1e30
CONV_WIDTH = 1024
CONV_K = 3
HG_HEADS = 8
HG_KEY_DIM = 128
HG_VAL_DIM = 128
F_MIN = 1e-30
N_BRANCHES = 3
NORM_EPS = 1e-6
ADA_CHUNKS = 6

ATT_Q_W = ATT_HEADS * ATT_HEAD_DIM
ATT_KV_W = ATT_KV_HEADS * ATT_HEAD_DIM
IDX_Q_W = IDX_HEADS * IDX_HEAD_DIM
HG_W = HG_HEADS * HG_KEY_DIM
ATT_LOGIT_SCALE = (ATT_HEAD_DIM ** -0.5) * float(np.log2(np.e))

LANES = 128
SUBLANES = 8
HALO_ROWS = 16
VMEM_LIMIT_BYTES = 56 * 1024 * 1024

INT_MIN = -2 ** 31
SEARCH_VARIANTS = 8
HG_CHUNK = 64
HG_LEVELS = 6
HG_SPLIT = 3

BF16 = jnp.bfloat16
F32 = jnp.float32


def _cparams(sem):
    return pltpu.CompilerParams(dimension_semantics=sem, vmem_limit_bytes=VMEM_LIMIT_BYTES)


def _dot(a, b):
    return jnp.dot(a, b, preferred_element_type=F32)


def _dot_nt(a, b):
    return lax.dot_general(a, b, (((1,), (1,)), ((), ())), preferred_element_type=F32)


def _sigmoid(x):
    return 1.0 / (1.0 + jnp.exp(-x))


def _mod_kernel(c_ref, w_ref, b_ref, o_ref):
    c = c_ref[...]
    ca = (c * _sigmoid(c)).astype(BF16)
    o_ref[...] = _dot(ca, w_ref[...].astype(BF16)) + b_ref[...]


def _modulation(c, ada_w, ada_b):
    depth, d, n = ada_w.shape
    b = c.shape[0]
    bp = -(-b // SUBLANES) * SUBLANES
    cp = jnp.pad(c, ((0, bp - b), (0, 0)))
    tn = 1024
    out = pl.pallas_call(
        _mod_kernel,
        grid=(depth, n // tn),
        in_specs=[
            pl.BlockSpec((bp, d), lambda l, j: (0, 0)),
            pl.BlockSpec((None, d, tn), lambda l, j: (l, 0, j)),
            pl.BlockSpec((None, 1, tn), lambda l, j: (l, 0, j)),
        ],
        out_specs=pl.BlockSpec((None, bp, tn), lambda l, j: (l, 0, j)),
        out_shape=jax.ShapeDtypeStruct((depth, bp, n), F32),
        compiler_params=_cparams(("arbitrary", "arbitrary")),
        name="adaln_mod",
    )(cp, ada_w, ada_b.reshape(depth, 1, n))
    return out[:, :b]


def _norm_kernel(x_ref, g_ref, *rest, modulate):
    if modulate:
        sh_ref, sc_ref, o_ref = rest
    else:
        (o_ref,) = rest
    x = x_ref[...]
    y = x * lax.rsqrt(jnp.mean(x * x, axis=-1, keepdims=True) + NORM_EPS) * g_ref[...]
    if modulate:
        y = y * (1.0 + sc_ref[0]) + sh_ref[0]
    o_ref[...] = y.astype(o_ref.dtype)


def _norm(x2, g, seq, shift=None, scale=None, out_dtype=BF16, tm=512):
    m, d = x2.shape
    modulate = shift is not None
    tpb = seq // tm
    in_specs = [pl.BlockSpec((tm, d), lambda i: (i, 0)),
                pl.BlockSpec((1, d), lambda i: (0, 0))]
    args = [x2, g.reshape(1, d)]
    if modulate:
        row = pl.BlockSpec((1, 1, d), lambda i: (i // tpb, 0, 0))
        in_specs += [row, row]
        args += [shift[:, None, :], scale[:, None, :]]
    return pl.pallas_call(
        functools.partial(_norm_kernel, modulate=modulate),
        grid=(m // tm,),
        in_specs=in_specs,
        out_specs=pl.BlockSpec((tm, d), lambda i: (i, 0)),
        out_shape=jax.ShapeDtypeStruct((m, d), out_dtype),
        compiler_params=_cparams(("arbitrary",)),
        name="rmsnorm",
    )(*args)


def _merge_kernel(att_ref, cv_ref, hg_ref, wa_ref, wc_ref, wh_ref, ga_ref, gc_ref, gh_ref,
                  wo_ref, x_ref, gr_ref, ng_ref, sh_ref, sc_ref, o_ref, h_ref):
    merged = (_sigmoid(ga_ref[...].astype(F32)) * _dot(att_ref[...], wa_ref[...])
              + _sigmoid(gc_ref[...].astype(F32)) * _dot(cv_ref[...], wc_ref[...])
              + _sigmoid(gh_ref[...].astype(F32)) * _dot(hg_ref[...], wh_ref[...]))
    y = x_ref[...] + gr_ref[0] * _dot(merged.astype(BF16), wo_ref[...])
    o_ref[...] = y
    yn = y * lax.rsqrt(jnp.mean(y * y, axis=-1, keepdims=True) + NORM_EPS) * ng_ref[...]
    h_ref[...] = (yn * (1.0 + sc_ref[0]) + sh_ref[0]).astype(h_ref.dtype)


def _merge_out(att, cv, hg, wa, wc, wh, proj, gate_off, wout, x2, gate_row, norm_g, shift, scale, seq,
               layer, tm=256):
    m, d = x2.shape
    tpb = seq // tm
    g0 = gate_off // d
    act = lambda a: pl.BlockSpec((tm, a.shape[1]), lambda i: (i, 0))
    resident = lambda w: pl.BlockSpec((None,) + w.shape[1:], lambda i: (layer, 0, 0),
                                      pipeline_mode=pl.Buffered(1))
    gate = lambda b: pl.BlockSpec((tm, d), lambda i: (i, g0 + b))
    row = pl.BlockSpec((1, 1, d), lambda i: (i // tpb, 0, 0))
    full = pl.BlockSpec((tm, d), lambda i: (i, 0))
    return pl.pallas_call(
        _merge_kernel,
        grid=(m // tm,),
        in_specs=[act(att), act(cv), act(hg), resident(wa), resident(wc), resident(wh),
                  gate(0), gate(1), gate(2), resident(wout), full, row,
                  pl.BlockSpec((1, d), lambda i: (0, 0)), row, row],
        out_specs=[full, full],
        out_shape=[jax.ShapeDtypeStruct((m, d), F32), jax.ShapeDtypeStruct((m, d), BF16)],
        compiler_params=_cparams(("arbitrary",)),
        name="merge_out",
    )(att, cv, hg, wa, wc, wh, proj, proj, proj, wout, x2, gate_row[:, None, :],
      norm_g.reshape(1, d), shift[:, None, :], scale[:, None, :])


def _mm_kernel(*refs, epilogue, nk):
    a_ref, w_ref = refs[0], refs[1]
    pos = 2
    extra = []
    n_extra = {"none": 0, "relu2": 0, "residual": 2}[epilogue]
    for _ in range(n_extra):
        extra.append(refs[pos])
        pos += 1
    o_ref = refs[pos]
    acc_ref = refs[pos + 1] if nk > 1 else None

    def finish(y):
        if epilogue == "relu2":
            y = jnp.square(jnp.maximum(y, 0.0))
        elif epilogue == "residual":
            y = extra[0][...] + extra[1][0] * y
        o_ref[...] = y.astype(o_ref.dtype)

    if nk == 1:
        finish(_dot(a_ref[...], w_ref[...]))
    else:
        k = pl.program_id(2)

        @pl.when(k == 0)
        def _():
            acc_ref[...] = jnp.zeros_like(acc_ref)

        acc_ref[...] += _dot(a_ref[...], w_ref[...])

        @pl.when(k == nk - 1)
        def _():
            finish(acc_ref[...])


def _matmul(a, w, *, out_dtype, epilogue="none", tm=1024, tn=512, tk=None,
            res=None, gate_row=None, seq=None, layer=0):
    m = a.shape[0]
    kdim, n = w.shape[-2:]
    tk = kdim if tk is None else tk
    nk = kdim // tk
    tn = min(tn, n)
    w_mode = pl.Buffered(1) if (tn == n and nk == 1) else None
    in_specs = [pl.BlockSpec((tm, tk), lambda i, j, k: (i, k)),
                pl.BlockSpec((None, tk, tn), lambda i, j, k: (layer, k, j), pipeline_mode=w_mode)]
    args = [a, w]
    if epilogue == "residual":
        tpb = seq // tm
        in_specs += [pl.BlockSpec((tm, tn), lambda i, j, k: (i, j)),
                     pl.BlockSpec((1, 1, tn), lambda i, j, k: (i // tpb, 0, j))]
        args += [res, gate_row[:, None, :]]
    scratch = [pltpu.VMEM((tm, tn), F32)] if nk > 1 else []
    return pl.pallas_call(
        functools.partial(_mm_kernel, epilogue=epilogue, nk=nk),
        grid=(m // tm, n // tn, nk),
        in_specs=in_specs,
        out_specs=pl.BlockSpec((tm, tn), lambda i, j, k: (i, j)),
        out_shape=jax.ShapeDtypeStruct((m, n), out_dtype),
        scratch_shapes=scratch,
        compiler_params=_cparams(("arbitrary", "arbitrary", "arbitrary")),
        name="matmul_" + epilogue,
    )(*args)


def _rope_rows(period, half, rot):
    j = np.arange(LANES) % period
    inv = ROPE_THETA ** (-(np.arange(half, dtype=np.float32)) / np.float32(half))
    inv_row = np.where(j < rot, inv.astype(np.float32)[j % half], 0.0).astype(np.float32)
    sign_row = np.where(j < half, -1.0, np.where(j < rot, 1.0, 0.0)).astype(np.float32)
    first_row = (j < half).astype(np.float32)
    return np.stack([inv_row, sign_row, first_row])[:, None, :]


def _rope_apply(x, cos_t, sin_s, first, half):
    up = pltpu.roll(x, LANES - half, axis=1)
    dn = pltpu.roll(x, half, axis=1)
    sw = jnp.where(first > 0.5, up, dn)
    return x * cos_t + sw * sin_s


def _rope_table_kernel(pos_ref, rows_a_ref, rows_i_ref, ca_ref, sa_ref, ci_ref, si_ref):
    pos = pos_ref[...].astype(F32)
    ang_a = pos * rows_a_ref[0]
    ca_ref[...] = jnp.cos(ang_a)
    sa_ref[...] = jnp.sin(ang_a) * rows_a_ref[1]
    ang_i = pos * rows_i_ref[0]
    ci_ref[...] = jnp.cos(ang_i)
    si_ref[...] = jnp.sin(ang_i) * rows_i_ref[1]


def _rope_consts():
    rows_a = jnp.asarray(_rope_rows(ATT_HEAD_DIM, ATT_HEAD_DIM // ROPE_FRACTION // 2,
                                    ATT_HEAD_DIM // ROPE_FRACTION))
    rows_i = jnp.asarray(_rope_rows(IDX_HEAD_DIM, IDX_HEAD_DIM // ROPE_FRACTION // 2,
                                    IDX_HEAD_DIM // ROPE_FRACTION))
    return rows_a, rows_i


def _rope_tables(positions, tm=512):
    m = positions.size
    rows_a, rows_i = _rope_consts()
    rows_spec = pl.BlockSpec((3, 1, LANES), lambda i: (0, 0, 0))
    tab = pl.BlockSpec((tm, LANES), lambda i: (i, 0))
    return pl.pallas_call(
        _rope_table_kernel,
        grid=(m // tm,),
        in_specs=[pl.BlockSpec((tm, 1), lambda i: (i, 0)), rows_spec, rows_spec],
        out_specs=[tab] * 4,
        out_shape=[jax.ShapeDtypeStruct((m, LANES), F32)] * 4,
        compiler_params=_cparams(("arbitrary",)),
        name="rope_tables",
    )(positions.reshape(m, 1), rows_a, rows_i)


def _rope_kernel(ca_ref, sa_ref, ci_ref, si_ref, rows_a_ref, rows_i_ref, q_ref, kv_ref, qi_ref, sm_ref,
                 qo_ref, ko_ref, vo_ref, qio_ref, kio_ref, wio_ref, *, w_scale):
    cos_a, sin_a, first_a = ca_ref[...], sa_ref[...], rows_a_ref[2]
    cos_i, sin_i, first_i = ci_ref[...], si_ref[...], rows_i_ref[2]
    half_a = ATT_HEAD_DIM // ROPE_FRACTION // 2
    half_i = IDX_HEAD_DIM // ROPE_FRACTION // 2
    for h in range(ATT_HEADS):
        sl = slice(h * LANES, (h + 1) * LANES)
        qo_ref[:, sl] = (_rope_apply(q_ref[:, sl].astype(F32), cos_a, sin_a, first_a, half_a)
                         * ATT_LOGIT_SCALE).astype(BF16)
    for h in range(ATT_KV_HEADS):
        sl = slice(h * LANES, (h + 1) * LANES)
        ko_ref[:, sl] = _rope_apply(kv_ref[:, sl].astype(F32), cos_a, sin_a, first_a, half_a).astype(BF16)
    vo_ref[...] = kv_ref[:, ATT_KV_W:].astype(F32).T.astype(BF16)
    for h in range(IDX_Q_W // LANES):
        sl = slice(h * LANES, (h + 1) * LANES)
        qio_ref[:, sl] = _rope_apply(qi_ref[:, sl].astype(F32), cos_i, sin_i, first_i, half_i).astype(BF16)
    sm = sm_ref[...]
    ki = _rope_apply(sm, cos_i, sin_i, first_i, half_i)
    kio_ref[...] = ki[:, :IDX_HEAD_DIM].astype(BF16)
    wio_ref[...] = sm.T[IDX_HEAD_DIM:IDX_HEAD_DIM + IDX_HEADS, :] * w_scale


def _rope_split(proj, side, tables, tm=512):
    m = proj.shape[0]
    rows_a, rows_i = _rope_consts()
    w_scale = (IDX_HEADS ** -0.5) * (IDX_HEAD_DIM ** -0.5)
    rows_spec = pl.BlockSpec((3, 1, LANES), lambda i: (0, 0, 0))
    tab = pl.BlockSpec((tm, LANES), lambda i: (i, 0))
    outs = pl.pallas_call(
        functools.partial(_rope_kernel, w_scale=w_scale),
        grid=(m // tm,),
        in_specs=[
            tab, tab, tab, tab,
            rows_spec, rows_spec,
            pl.BlockSpec((tm, ATT_Q_W), lambda i: (i, 0)),
            pl.BlockSpec((tm, 2 * ATT_KV_W), lambda i: (i, ATT_Q_W // (2 * ATT_KV_W))),
            pl.BlockSpec((tm, IDX_Q_W), lambda i: (i, (ATT_Q_W + 2 * ATT_KV_W) // IDX_Q_W)),
            pl.BlockSpec((tm, LANES), lambda i: (i, _SIDE_SMALL // LANES)),
        ],
        out_specs=[
            pl.BlockSpec((tm, ATT_Q_W), lambda i: (i, 0)),
            pl.BlockSpec((tm, ATT_KV_W), lambda i: (i, 0)),
            pl.BlockSpec((ATT_KV_W, tm), lambda i: (0, i)),
            pl.BlockSpec((tm, IDX_Q_W), lambda i: (i, 0)),
            pl.BlockSpec((tm, IDX_HEAD_DIM), lambda i: (i, 0)),
            pl.BlockSpec((IDX_HEADS, tm), lambda i: (0, i)),
        ],
        out_shape=[
            jax.ShapeDtypeStruct((m, ATT_Q_W), BF16),
            jax.ShapeDtypeStruct((m, ATT_KV_W), BF16),
            jax.ShapeDtypeStruct((ATT_KV_W, m), BF16),
            jax.ShapeDtypeStruct((m, IDX_Q_W), BF16),
            jax.ShapeDtypeStruct((m, IDX_HEAD_DIM), BF16),
            jax.ShapeDtypeStruct((IDX_HEADS, m), F32),
        ],
        compiler_params=_cparams(("arbitrary",)),
        name="rope_split",
    )(*tables, rows_a, rows_i, proj, proj, proj, side)
    return outs


def _attn_kernel(qi_ref, wi_ref, ki_ref, q_ref, k_ref, vt_ref, o_ref, skey_ref,
                 da_ref, db_ref, sa_ref, sb_ref, *, tq, kc, topk, seq):
    i = pl.program_id(1)
    nchunk = lax.shift_right_logical((i + 1) * tq + (kc - 1), int(np.log2(kc)))
    qpos = i * tq + lax.broadcasted_iota(jnp.int32, (1, tq), 1)
    rb = 8 * SUBLANES

    qi = qi_ref[0]
    wi = wi_ref[...]
    qi_s = jnp.concatenate([qi[:, h * IDX_HEAD_DIM:(h + 1) * IDX_HEAD_DIM] for h in range(IDX_HEADS)],
                           axis=0)

    kh = kc // 2
    last_half = seq // kh - 1

    def score_dots(half, dst_ref):
        k0 = pl.multiple_of(jnp.minimum(half, last_half) * kh, kh)
        dst_ref[...] = _dot_nt(ki_ref[0, pl.ds(k0, kh), :], qi_s)

    def score_keys(half, src_ref):
        k0 = pl.multiple_of(half * kh, kh)
        for j in range(kh // rb):
            rows = slice(j * rb, (j + 1) * rb)
            acc = jnp.zeros((rb, tq), F32)
            for h in range(IDX_HEADS):
                acc = acc + jnp.maximum(src_ref[rows, h * tq:(h + 1) * tq], 0.0) * wi[h:h + 1, :]
            kpos = k0 + j * rb + lax.broadcasted_iota(jnp.int32, (rb, tq), 0)
            bits = pltpu.bitcast(acc, jnp.int32)
            skey = jnp.where(bits < 0, bits ^ jnp.int32(0x7FFFFFFF), bits)
            skey = jnp.where(kpos <= qpos, skey, jnp.int32(INT_MIN))
            skey_ref[pl.ds(k0 + j * rb, rb), :] = skey

    def score_chunk(c, carry):
        score_dots(2 * c + 1, db_ref)
        score_keys(2 * c, da_ref)
        score_dots(2 * c + 2, da_ref)
        score_keys(2 * c + 1, db_ref)
        return carry

    score_dots(0, da_ref)
    lax.fori_loop(0, nchunk, score_chunk, 0)

    nacc = 4

    def count_where(pred_fn):
        row_iota = lax.broadcasted_iota(jnp.int32, (SUBLANES, tq), 0)

        def chunk_body(c, parts):
            k0 = pl.multiple_of(c * kc, kc)
            parts = list(parts)
            xs = skey_ref[pl.ds(k0, kc), :]
            for j in range(kc // SUBLANES):
                x = xs[j * SUBLANES:(j + 1) * SUBLANES]
                parts[j % nacc] = parts[j % nacc] + jnp.where(pred_fn(x, k0 + j * SUBLANES + row_iota), 1, 0)
            return tuple(parts)
        zero = jnp.zeros((SUBLANES, tq), jnp.int32)
        parts = lax.fori_loop(0, nchunk, chunk_body, (zero,) * nacc)
        return jnp.sum(sum(parts[1:], parts[0]), axis=0, keepdims=True)

    nfull = seq // kc
    step = max(nfull // SEARCH_VARIANTS, 1)
    variant = lax.div(nchunk + (step - 1), step) - 1

    def fill_chunk(c, carry):
        rows = pl.ds(pl.multiple_of(c * kc, kc), kc)
        skey_ref[rows, :] = jnp.full((kc, tq), INT_MIN, jnp.int32)
        return carry

    lax.fori_loop(nchunk, (variant + 1) * step, fill_chunk, 0)

    def search_all(nc):
        def bit_body(b, prefix):
            cand = prefix | lax.shift_left(jnp.int32(1), 31 - b)
            cand_s = jnp.broadcast_to(cand ^ jnp.int32(INT_MIN), (SUBLANES, tq))
            parts = [jnp.zeros((SUBLANES, tq), jnp.int32)] * nacc
            for j in range(nc * kc // SUBLANES):
                x = skey_ref[j * SUBLANES:(j + 1) * SUBLANES, :]
                parts[j % nacc] = parts[j % nacc] + jnp.where(x >= cand_s, 1, 0)
            cnt = jnp.sum(sum(parts[1:], parts[0]), axis=0, keepdims=True)
            return jnp.where(cnt >= topk, cand, prefix)
        return lax.fori_loop(0, 32, bit_body, jnp.zeros((1, tq), jnp.int32))

    prefix = lax.switch(variant, [functools.partial(search_all, nc) for nc in range(step, nfull + 1, step)])
    thr = prefix ^ jnp.int32(INT_MIN)

    n_gt = count_where(lambda x, _: x > thr)
    n_eq = count_where(lambda x, _: x == thr)
    need = topk - n_gt
    excess = jnp.where((prefix != 0) & (n_eq > need), 1, 0)
    any_excess = jnp.max(excess) > 0
    idx_bits = int(np.log2(seq))

    def tie_search():
        def tie_bit(b, x):
            cand = x | lax.shift_left(jnp.int32(1), idx_bits - 1 - b)
            below = count_where(lambda xk, kidx: (xk == thr) & (kidx < cand))
            return jnp.where(below < need, cand, x)
        return lax.fori_loop(0, idx_bits, tie_bit, jnp.zeros((1, tq), jnp.int32))

    tie_last = lax.cond(any_excess, tie_search, lambda: jnp.full((1, tq), seq, jnp.int32))

    groups = ATT_KV_HEADS
    rep = ATT_HEADS // ATT_KV_HEADS
    q = q_ref[0]
    q_g = [jnp.concatenate([q[:, (g * rep + r) * LANES:(g * rep + r + 1) * LANES] for r in range(rep)], axis=0)
           for g in range(groups)]

    def qk_dots(half, dst_ref):
        k0 = pl.multiple_of(jnp.minimum(half, last_half) * kh, kh)
        for g in range(groups):
            dst_ref[g] = _dot_nt(k_ref[0, pl.ds(k0, kh), g * LANES:(g + 1) * LANES], q_g[g])

    def softmax_pv(half, src_ref, carry):
        k0 = pl.multiple_of(half * kh, kh)
        x = skey_ref[pl.ds(k0, kh), :]
        kidx = k0 + lax.broadcasted_iota(jnp.int32, (kh, tq), 0)
        sel = (x > thr) | ((x == thr) & (kidx <= tie_last))
        sel = sel & (x != jnp.int32(INT_MIN))
        bias = jnp.where(sel, 0.0, MASK_VALUE)
        bias = jnp.concatenate([bias] * rep, axis=1)
        new = []
        for g in range(groups):
            m_old, l_old, acc_old = carry[g]
            vtg = vt_ref[g * LANES:(g + 1) * LANES, pl.ds(k0, kh)]
            sm = src_ref[g] + bias
            m_new = jnp.maximum(m_old, jnp.max(sm, axis=0, keepdims=True))
            p = jnp.exp2(sm - m_new)
            alpha = jnp.exp2(m_old - m_new)
            l_new = alpha * l_old + jnp.sum(p, axis=0, keepdims=True)
            acc_new = alpha * acc_old + _dot(vtg, p.astype(BF16))
            new.append((m_new, l_new, acc_new))
        return tuple(new)

    def attn_chunk(c, carry):
        qk_dots(2 * c + 1, sb_ref)
        carry = softmax_pv(2 * c, sa_ref, carry)
        qk_dots(2 * c + 2, sa_ref)
        return softmax_pv(2 * c + 1, sb_ref, carry)

    init = tuple((jnp.full((1, rep * tq), MASK_VALUE, F32), jnp.zeros((1, rep * tq), F32),
                  jnp.zeros((LANES, rep * tq), F32)) for _ in range(groups))
    qk_dots(0, sa_ref)
    fin = lax.fori_loop(0, nchunk, attn_chunk, init)
    for g in range(groups):
        _, l_f, acc_f = fin[g]
        o_t = acc_f / l_f
        for r in range(rep):
            h = g * rep + r
            o_ref[0, :, h * LANES:(h + 1) * LANES] = o_t[:, r * tq:(r + 1) * tq].T.astype(o_ref.dtype)


def _dsa_attention(q, k, vt, qi, ki, wit, batch, seq, topk, tq=128, kc=512):
    nblk = seq // tq
    rep = ATT_HEADS // ATT_KV_HEADS
    r3 = lambda a: a.reshape(batch, seq, a.shape[-1])
    qblk = lambda w: pl.BlockSpec((1, tq, w), lambda b, i: (b, i, 0))
    full = lambda w: pl.BlockSpec((1, seq, w), lambda b, i: (b, 0, 0))
    out = pl.pallas_call(
        functools.partial(_attn_kernel, tq=tq, kc=kc, topk=topk, seq=seq),
        grid=(batch, nblk),
        in_specs=[qblk(IDX_Q_W), pl.BlockSpec((IDX_HEADS, tq), lambda b, i: (0, b * nblk + i)),
                  full(IDX_HEAD_DIM), qblk(ATT_Q_W), full(ATT_KV_W),
                  pl.BlockSpec((ATT_KV_W, seq), lambda b, i: (0, b))],
        out_specs=qblk(ATT_Q_W),
        out_shape=jax.ShapeDtypeStruct((batch, seq, ATT_Q_W), BF16),
        scratch_shapes=[pltpu.VMEM((seq, tq), jnp.int32),
                        pltpu.VMEM((kc // 2, IDX_HEADS * tq), F32), pltpu.VMEM((kc // 2, IDX_HEADS * tq), F32),
                        pltpu.VMEM((ATT_KV_HEADS, kc // 2, rep * tq), F32),
                        pltpu.VMEM((ATT_KV_HEADS, kc // 2, rep * tq), F32)],
        compiler_params=_cparams(("arbitrary", "arbitrary")),
        name="dsa_attention",
    )(r3(qi), wit, r3(ki), r3(q), r3(k), vt)
    return out.reshape(batch * seq, ATT_Q_W)


def _conv_kernel(cb_ref, cc_ref, cx_ref, hc_ref, hx_ref, w_ref, o_ref, *, tm, seq):
    i = pl.program_id(0)
    u = cc_ref[...].astype(F32) * cx_ref[...].astype(F32)
    halo = hc_ref[...].astype(F32) * hx_ref[...].astype(F32)
    halo = jnp.where((i * tm) % seq == 0, 0.0, halo)
    row = lax.broadcasted_iota(jnp.int32, u.shape, 0)
    u1 = pltpu.roll(u, 1, axis=0)
    u2 = pltpu.roll(u, 2, axis=0)
    h1 = halo[HALO_ROWS - 1:HALO_ROWS, :]
    h2 = halo[HALO_ROWS - 2:HALO_ROWS - 1, :]
    u1 = jnp.where(row == 0, h1, u1)
    u2 = jnp.where(row == 0, h2, jnp.where(row == 1, h1, u2))
    w = w_ref[...]
    conv = u2 * w[0:1, :] + u1 * w[1:2, :] + u * w[2:3, :]
    o_ref[...] = (cb_ref[...].astype(F32) * conv).astype(o_ref.dtype)


def _short_conv(proj, conv_w, seq, col0, tm=512, tc=512):
    m = proj.shape[0]
    nb = CONV_WIDTH // tc
    c0 = col0 // tc
    rpb = tm // HALO_ROWS
    main = lambda off: pl.BlockSpec((tm, tc), lambda i, j: (i, c0 + off * nb + j))
    halo = lambda off: pl.BlockSpec((HALO_ROWS, tc),
                                    lambda i, j: (jnp.maximum(i * rpb - 1, 0), c0 + off * nb + j))
    return pl.pallas_call(
        functools.partial(_conv_kernel, tm=tm, seq=seq),
        grid=(m // tm, nb),
        in_specs=[main(0), main(1), main(2), halo(1), halo(2),
                  pl.BlockSpec((CONV_K, tc), lambda i, j: (0, j))],
        out_specs=pl.BlockSpec((tm, tc), lambda i, j: (i, j)),
        out_shape=jax.ShapeDtypeStruct((m, CONV_WIDTH), BF16),
        compiler_params=_cparams(("arbitrary", "arbitrary")),
        name="short_conv",
    )(proj, proj, proj, proj, proj, conv_w)


def _hg_tables():
    c = HG_CHUNK
    t = np.arange(c)
    mats, masks = [], []
    for l in range(HG_LEVELS):
        h = 1 << l
        start = (t // (2 * h)) * (2 * h)
        p = start + h - 1
        right = (t - start) >= h
        u = t[None, :]
        r_m = right[:, None] & (u > p[:, None]) & (u <= t[:, None])
        l_m = (~right)[:, None] & (u > t[:, None]) & (u <= p[:, None])
        mats.append(r_m | l_m)
        same = (t[:, None] // (2 * h)) == (t[None, :] // (2 * h))
        masks.append(same & right[:, None] & (~right)[None, :])
    mats.append(t[None, :] <= t[:, None])
    mats.append(t[None, :] > t[:, None])
    mat = np.concatenate(mats, 0).astype(np.float32)
    return np.concatenate([mat] * HG_SPLIT, 1), np.stack(masks).astype(np.float32)


def _split3(x):
    hi = x.astype(BF16)
    r1 = x - hi.astype(F32)
    mid = r1.astype(BF16)
    lo = (r1 - mid.astype(F32)).astype(BF16)
    return jnp.concatenate([hi, mid, lo], axis=0)


def _hg_kernel(hq_ref, hf_ref, hi_ref, hg_ref, lb_ref, g_ref, mat_ref, mask_ref, o_ref, state_ref, *, ts):
    c = HG_CHUNK

    @pl.when(pl.program_id(1) == 0)
    def _():
        state_ref[...] = jnp.zeros_like(state_ref)

    g = g_ref[...]

    def chunk(ci, carry):
        rows = pl.ds(pl.multiple_of(ci * c, c), c)
        lb = lb_ref[...]
        f = lb + (1.0 - lb) * _sigmoid(hf_ref[rows, :])
        logf = jnp.log(jnp.maximum(f, F_MIN))
        kin = 1.0 - f
        q = hq_ref[rows, :].astype(F32)
        v16 = hi_ref[rows, :]
        e = jnp.exp(_dot(mat_ref[...], _split3(logf)))
        qd = [(q * e[l * c:(l + 1) * c]).astype(BF16) for l in range(HG_LEVELS + 1)]
        kd = [(kin * e[l * c:(l + 1) * c]).astype(BF16) for l in range(HG_LEVELS)]
        e_b = e[HG_LEVELS * c:(HG_LEVELS + 1) * c]
        kk = (kin * e[(HG_LEVELS + 1) * c:(HG_LEVELS + 2) * c]).astype(BF16)
        qk = q * kin
        gate = hg_ref[rows, :].astype(F32)
        gate = gate * _sigmoid(gate)
        for h in range(HG_HEADS):
            cols = slice(h * LANES, (h + 1) * LANES)
            attn = jnp.zeros((c, c), F32)
            for l in range(HG_LEVELS):
                attn = attn + _dot_nt(qd[l][:, cols], kd[l][:, cols]) * mask_ref[l]
            state_t = state_ref[h]
            v16h = v16[:, cols]
            diag = jnp.sum(qk[:, cols], axis=1, keepdims=True)
            out = (_dot(attn.astype(BF16), v16h) + diag * v16h.astype(F32)
                   + _dot_nt(qd[HG_LEVELS][:, cols], state_t.astype(BF16)))
            kv_t = lax.dot_general(v16h, kk[:, cols], (((0,), (0,)), ((), ())),
                                   preferred_element_type=F32)
            state_ref[h] = state_t * e_b[c - 1:c, cols] + kv_t
            y = out * lax.rsqrt(jnp.mean(out * out, axis=-1, keepdims=True) + NORM_EPS) * g
            o_ref[rows, cols] = (y * gate[:, cols]).astype(o_ref.dtype)
        return carry

    lax.fori_loop(0, ts // c, chunk, 0)


def _hgrn2(proj, side, lb, norm_g, batch, seq, col0, ts=512):
    m = proj.shape[0]
    c0 = col0 // HG_W
    spb = seq // ts
    mat, masks = _hg_tables()
    col = lambda off: pl.BlockSpec((ts, HG_W), lambda b, s: (b * spb + s, c0 + off))
    return pl.pallas_call(
        functools.partial(_hg_kernel, ts=ts),
        grid=(batch, spb),
        in_specs=[col(0), pl.BlockSpec((ts, HG_W), lambda b, s: (b * spb + s, 0)), col(1), col(2),
                  pl.BlockSpec((1, HG_W), lambda b, s: (0, 0)),
                  pl.BlockSpec((1, LANES), lambda b, s: (0, 0)),
                  pl.BlockSpec(mat.shape, lambda b, s: (0, 0)),
                  pl.BlockSpec(masks.shape, lambda b, s: (0, 0, 0))],
        out_specs=pl.BlockSpec((ts, HG_W), lambda b, s: (b * spb + s, 0)),
        out_shape=jax.ShapeDtypeStruct((m, HG_W), BF16),
        scratch_shapes=[pltpu.VMEM((HG_HEADS, HG_VAL_DIM, HG_KEY_DIM), F32)],
        compiler_params=_cparams(("arbitrary", "arbitrary")),
        name="hgrn2",
    )(proj, side, proj, proj, lb.reshape(1, HG_W), norm_g.reshape(1, HG_VAL_DIM),
      jnp.asarray(mat, BF16), jnp.asarray(masks))


def _lb_kernel(x_ref, o_ref):
    x = x_ref[...]
    e = jnp.exp(x - jnp.max(x, axis=0, keepdims=True))
    p = e / jnp.sum(e, axis=0, keepdims=True)
    depth = x.shape[0]
    run = jnp.zeros_like(p[0:1])
    for l in range(depth):
        run = run + p[l:l + 1]
        o_ref[l:l + 1, :] = run - p[0:1]


def _lower_bounds(hg_lower_bounds):
    return pl.pallas_call(
        _lb_kernel,
        out_shape=jax.ShapeDtypeStruct(hg_lower_bounds.shape, F32),
        name="hg_lower_bounds",
    )(hg_lower_bounds.astype(F32))


_COL_Q = 0
_COL_CONV = ATT_Q_W + 2 * ATT_KV_W + IDX_Q_W
_COL_HG = _COL_CONV + 3 * CONV_WIDTH
_COL_GATE = _COL_HG + 3 * HG_W
_SIDE_SMALL = HG_W
_SIDE_W = HG_W + LANES


def _split_w_in(w_in, d_model):
    sizes = (ATT_Q_W, ATT_KV_W, ATT_KV_W, IDX_Q_W, IDX_HEAD_DIM, IDX_HEADS,
             CONV_WIDTH, CONV_WIDTH, CONV_WIDTH, HG_W, HG_W, HG_W, HG_W, N_BRANCHES * d_model)
    offs = np.cumsum((0,) + sizes)
    part = lambda n: w_in[..., offs[n]:offs[n + 1]]
    main = jnp.concatenate([part(n) for n in (0, 1, 2, 3, 6, 7, 8, 9, 11, 12, 13)], axis=-1).astype(BF16)
    side = jnp.concatenate([part(10), part(4), part(5)], axis=-1)
    side = jnp.pad(side, ((0, 0), (0, 0), (0, _SIDE_W - side.shape[-1]))).astype(BF16)
    return main, side


def kernel(x, c, positions, ada_w, ada_b, norm_mix_g, w_in, conv_w, hg_lower_bounds, hg_norm_g,
           w_o_attn, w_o_conv, w_o_hgrn, w_out, norm_mlp_g, w_mlp1, w_mlp2, final_norm_g):
    batch, seq, d = x.shape
    depth = ada_w.shape[0]
    m = batch * seq
    topk = min(INDEX_TOPK, seq // 4)
    lb_all = _lower_bounds(hg_lower_bounds)
    mod = _modulation(c, ada_w, ada_b)
    tables = _rope_tables(positions)
    w_main, w_side = _split_w_in(w_in, d)
    wa, wc, wh, wo = (w.astype(BF16) for w in (w_o_attn, w_o_conv, w_o_hgrn, w_out))
    w1, w2 = w_mlp1.astype(BF16), w_mlp2.astype(BF16)
    x2 = x.reshape(m, d)
    for l in range(depth):
        sh1, sc1, g1, sh2, sc2, g2 = [mod[l, :, n * d:(n + 1) * d] for n in range(ADA_CHUNKS)]
        h = _norm(x2, norm_mix_g[l], seq, sh1, sc1)
        proj = _matmul(h, w_main, out_dtype=BF16, tn=1024, layer=l)
        side = _matmul(h, w_side, out_dtype=F32, tn=_SIDE_W, layer=l)
        q, k, vt, qi, ki, wit = _rope_split(proj, side, tables)
        att = _dsa_attention(q, k, vt, qi, ki, wit, batch, seq, topk)
        cv = _short_conv(proj, conv_w[l], seq, _COL_CONV)
        hg = _hgrn2(proj, side, lb_all[l], hg_norm_g[l], batch, seq, _COL_HG)
        x2, h = _merge_out(att, cv, hg, wa, wc, wh, proj, _COL_GATE, wo, x2, g1,
                           norm_mlp_g[l], sh2, sc2, seq, l)
        a = _matmul(h, w1, out_dtype=BF16, epilogue="relu2", tm=256, tn=4 * d, layer=l)
        x2 = _matmul(a, w2, out_dtype=F32, epilogue="residual", tm=256, tn=d,
                     res=x2, gate_row=g2, seq=seq, layer=l)
    out = _norm(x2, final_norm_g, seq, out_dtype=x.dtype)
    return out.reshape(batch, seq, d)
```

```python
import functools

import numpy as np
import jax
import jax.numpy as jnp
from jax import lax
from jax.experimental import pallas as pl
from jax.experimental.pallas import tpu as pltpu

ATT_HEADS = 8
ATT_KV_HEADS = 2
ATT_HEAD_DIM = 128
IDX_HEADS = 8
IDX_HEAD_DIM = 64
INDEX_TOPK = 256
ROPE_THETA = 500000.0
ROPE_FRACTION = 4
MASK_VALUE = -1e30
CONV_WIDTH = 1024
CONV_K = 3
HG_HEADS = 8
HG_KEY_DIM = 128
HG_VAL_DIM = 128
F_MIN = 1e-30
N_BRANCHES = 3
NORM_EPS = 1e-6
ADA_CHUNKS = 6

ATT_Q_W = ATT_HEADS * ATT_HEAD_DIM
ATT_KV_W = ATT_KV_HEADS * ATT_HEAD_DIM
IDX_Q_W = IDX_HEADS * IDX_HEAD_DIM
HG_W = HG_HEADS * HG_KEY_DIM
ATT_LOGIT_SCALE = (ATT_HEAD_DIM ** -0.5) * float(np.log2(np.e))

LANES = 128
SUBLANES = 8
HALO_ROWS = 16
VMEM_LIMIT_BYTES = 56 * 1024 * 1024

INT_MIN = -2 ** 31
SEARCH_VARIANTS = 8
HG_CHUNK = 64
HG_LEVELS = 6
HG_SPLIT = 3

BF16 = jnp.bfloat16
F32 = jnp.float32


def _cparams(sem):
    return pltpu.CompilerParams(dimension_semantics=sem, vmem_limit_bytes=VMEM_LIMIT_BYTES)


def _dot(a, b):
    return jnp.dot(a, b, preferred_element_type=F32)


def _dot_nt(a, b):
    return lax.dot_general(a, b, (((1,), (1,)), ((), ())), preferred_element_type=F32)


def _sigmoid(x):
    return 1.0 / (1.0 + jnp.exp(-x))


def _mod_kernel(c_ref, w_ref, b_ref, o_ref):
    c = c_ref[...]
    ca = (c * _sigmoid(c)).astype(BF16)
    o_ref[...] = _dot(ca, w_ref[...].astype(BF16)) + b_ref[...]


def _modulation(c, ada_w, ada_b):
    depth, d, n = ada_w.shape
    b = c.shape[0]
    bp = -(-b // SUBLANES) * SUBLANES
    cp = jnp.pad(c, ((0, bp - b), (0, 0)))
    tn = 1024
    out = pl.pallas_call(
        _mod_kernel,
        grid=(depth, n // tn),
        in_specs=[
            pl.BlockSpec((bp, d), lambda l, j: (0, 0)),
            pl.BlockSpec((None, d, tn), lambda l, j: (l, 0, j)),
            pl.BlockSpec((None, 1, tn), lambda l, j: (l, 0, j)),
        ],
        out_specs=pl.BlockSpec((None, bp, tn), lambda l, j: (l, 0, j)),
        out_shape=jax.ShapeDtypeStruct((depth, bp, n), F32),
        compiler_params=_cparams(("arbitrary", "arbitrary")),
        name="adaln_mod",
    )(cp, ada_w, ada_b.reshape(depth, 1, n))
    return out[:, :b]


def _norm_kernel(x_ref, g_ref, sh_ref, sc_ref, o_ref):
    x = x_ref[...]
    y = x * lax.rsqrt(jnp.mean(x * x, axis=-1, keepdims=True) + NORM_EPS) * g_ref[...]
    o_ref[...] = (y * (1.0 + sc_ref[0]) + sh_ref[0]).astype(o_ref.dtype)


def _norm(x2, g, seq, shift, scale, out_dtype=BF16, tm=512):
    m, d = x2.shape
    tpb = seq // tm
    row = pl.BlockSpec((1, 1, d), lambda i: (i // tpb, 0, 0))
    return pl.pallas_call(
        _norm_kernel,
        grid=(m // tm,),
        in_specs=[pl.BlockSpec((tm, d), lambda i: (i, 0)), pl.BlockSpec((1, d), lambda i: (0, 0)), row, row],
        out_specs=pl.BlockSpec((tm, d), lambda i: (i, 0)),
        out_shape=jax.ShapeDtypeStruct((m, d), out_dtype),
        compiler_params=_cparams(("arbitrary",)),
        name="rmsnorm",
    )(x2, g.reshape(1, d), shift[:, None, :], scale[:, None, :])


def _merge_kernel(att_ref, cv_ref, hg_ref, wa_ref, wc_ref, wh_ref, ga_ref, gc_ref, gh_ref,
                  wo_ref, x_ref, gr_ref, ng_ref, sh_ref, sc_ref, o_ref, h_ref):
    merged = (_sigmoid(ga_ref[...].astype(F32)) * _dot(att_ref[...], wa_ref[...])
              + _sigmoid(gc_ref[...].astype(F32)) * _dot(cv_ref[...], wc_ref[...])
              + _sigmoid(gh_ref[...].astype(F32)) * _dot(hg_ref[...], wh_ref[...]))
    y = x_ref[...] + gr_ref[0] * _dot(merged.astype(BF16), wo_ref[...])
    o_ref[...] = y
    yn = y * lax.rsqrt(jnp.mean(y * y, axis=-1, keepdims=True) + NORM_EPS) * ng_ref[...]
    h_ref[...] = (yn * (1.0 + sc_ref[0]) + sh_ref[0]).astype(h_ref.dtype)


def _merge_out(att, cv, hg, wa, wc, wh, proj, gate_off, wout, x2, gate_row, norm_g, shift, scale, seq,
               layer, tm=256):
    m, d = x2.shape
    tpb = seq // tm
    g0 = gate_off // d
    act = lambda a: pl.BlockSpec((tm, a.shape[1]), lambda i: (i, 0))
    resident = lambda w: pl.BlockSpec((None,) + w.shape[1:], lambda i: (layer, 0, 0),
                                      pipeline_mode=pl.Buffered(1))
    gate = lambda b: pl.BlockSpec((tm, d), lambda i: (i, g0 + b))
    row = pl.BlockSpec((1, 1, d), lambda i: (i // tpb, 0, 0))
    full = pl.BlockSpec((tm, d), lambda i: (i, 0))
    return pl.pallas_call(
        _merge_kernel,
        grid=(m // tm,),
        in_specs=[act(att), act(cv), act(hg), resident(wa), resident(wc), resident(wh),
                  gate(0), gate(1), gate(2), resident(wout), full, row,
                  pl.BlockSpec((1, d), lambda i: (0, 0)), row, row],
        out_specs=[full, full],
        out_shape=[jax.ShapeDtypeStruct((m, d), F32), jax.ShapeDtypeStruct((m, d), BF16)],
        compiler_params=_cparams(("arbitrary",)),
        name="merge_out",
    )(att, cv, hg, wa, wc, wh, proj, proj, proj, wout, x2, gate_row[:, None, :],
      norm_g.reshape(1, d), shift[:, None, :], scale[:, None, :])


def _mm_kernel(a_ref, w_ref, o_ref, *, relu2):
    y = _dot(a_ref[...], w_ref[...])
    if relu2:
        y = jnp.square(jnp.maximum(y, 0.0))
    o_ref[...] = y.astype(o_ref.dtype)


def _matmul(a, w, *, out_dtype, relu2=False, tm=1024, tn=512, layer=0):
    m = a.shape[0]
    kdim, n = w.shape[-2:]
    tn = min(tn, n)
    w_mode = pl.Buffered(1) if tn == n else None
    return pl.pallas_call(
        functools.partial(_mm_kernel, relu2=relu2),
        grid=(m // tm, n // tn),
        in_specs=[pl.BlockSpec((tm, kdim), lambda i, j: (i, 0)),
                  pl.BlockSpec((None, kdim, tn), lambda i, j: (layer, 0, j), pipeline_mode=w_mode)],
        out_specs=pl.BlockSpec((tm, tn), lambda i, j: (i, j)),
        out_shape=jax.ShapeDtypeStruct((m, n), out_dtype),
        compiler_params=_cparams(("arbitrary", "arbitrary")),
        name="matmul_relu2" if relu2 else "matmul",
    )(a, w)


def _mlp_down_kernel(a_ref, w_ref, x_ref, gr_ref, ng_ref, sh_ref, sc_ref, *outs, emit_x):
    y = x_ref[...] + gr_ref[0] * _dot(a_ref[...], w_ref[...])
    if emit_x:
        outs[0][...] = y
    yn = y * lax.rsqrt(jnp.mean(y * y, axis=-1, keepdims=True) + NORM_EPS) * ng_ref[...]
    outs[-1][...] = (yn * (1.0 + sc_ref[0]) + sh_ref[0]).astype(outs[-1].dtype)


def _mlp_down(a, w, x2, gate_row, norm_g, shift, scale, seq, layer, *, emit_x, h_dtype, tm=256):
    m, d = x2.shape
    kdim = w.shape[1]
    tpb = seq // tm
    row = pl.BlockSpec((1, 1, d), lambda i: (i // tpb, 0, 0))
    full = pl.BlockSpec((tm, d), lambda i: (i, 0))
    out_specs, out_shape = [full], [jax.ShapeDtypeStruct((m, d), h_dtype)]
    if emit_x:
        out_specs, out_shape = [full] + out_specs, [jax.ShapeDtypeStruct((m, d), F32)] + out_shape
    outs = pl.pallas_call(
        functools.partial(_mlp_down_kernel, emit_x=emit_x),
        grid=(m // tm,),
        in_specs=[pl.BlockSpec((tm, kdim), lambda i: (i, 0)),
                  pl.BlockSpec((None, kdim, d), lambda i: (layer, 0, 0), pipeline_mode=pl.Buffered(1)),
                  full, row, pl.BlockSpec((1, d), lambda i: (0, 0)), row, row],
        out_specs=out_specs,
        out_shape=out_shape,
        compiler_params=_cparams(("arbitrary",)),
        name="mlp_down",
    )(a, w, x2, gate_row[:, None, :], norm_g.reshape(1, d), shift[:, None, :], scale[:, None, :])
    return outs if emit_x else outs[0]


def _rope_rows(period, half, rot):
    j = np.arange(LANES) % period
    inv = ROPE_THETA ** (-(np.arange(half, dtype=np.float32)) / np.float32(half))
    inv_row = np.where(j < rot, inv.astype(np.float32)[j % half], 0.0).astype(np.float32)
    sign_row = np.where(j < half, -1.0, np.where(j < rot, 1.0, 0.0)).astype(np.float32)
    first_row = (j < half).astype(np.float32)
    return np.stack([inv_row, sign_row, first_row])[:, None, :]


def _rope_apply(x, cos_t, sin_s, first, half):
    up = pltpu.roll(x, LANES - half, axis=1)
    dn = pltpu.roll(x, half, axis=1)
    sw = jnp.where(first > 0.5, up, dn)
    return x * cos_t + sw * sin_s


def _rope_table_kernel(pos_ref, rows_a_ref, rows_i_ref, ca_ref, sa_ref, ci_ref, si_ref):
    pos = pos_ref[...].astype(F32)
    ang_a = pos * rows_a_ref[0]
    ca_ref[...] = jnp.cos(ang_a)
    sa_ref[...] = jnp.sin(ang_a) * rows_a_ref[1]
    ang_i = pos * rows_i_ref[0]
    ci_ref[...] = jnp.cos(ang_i)
    si_ref[...] = jnp.sin(ang_i) * rows_i_ref[1]


def _rope_consts():
    rows_a = jnp.asarray(_rope_rows(ATT_HEAD_DIM, ATT_HEAD_DIM // ROPE_FRACTION // 2,
                                    ATT_HEAD_DIM // ROPE_FRACTION))
    rows_i = jnp.asarray(_rope_rows(IDX_HEAD_DIM, IDX_HEAD_DIM // ROPE_FRACTION // 2,
                                    IDX_HEAD_DIM // ROPE_FRACTION))
    return rows_a, rows_i


def _rope_tables(positions, tm=512):
    m = positions.size
    rows_a, rows_i = _rope_consts()
    rows_spec = pl.BlockSpec((3, 1, LANES), lambda i: (0, 0, 0))
    tab = pl.BlockSpec((tm, LANES), lambda i: (i, 0))
    return pl.pallas_call(
        _rope_table_kernel,
        grid=(m // tm,),
        in_specs=[pl.BlockSpec((tm, 1), lambda i: (i, 0)), rows_spec, rows_spec],
        out_specs=[tab] * 4,
        out_shape=[jax.ShapeDtypeStruct((m, LANES), F32)] * 4,
        compiler_params=_cparams(("arbitrary",)),
        name="rope_tables",
    )(positions.reshape(m, 1), rows_a, rows_i)


def _rope_kernel(ca_ref, sa_ref, ci_ref, si_ref, rows_a_ref, rows_i_ref, q_ref, kv_ref, qi_ref, sm_ref,
                 qo_ref, ko_ref, vo_ref, qio_ref, kio_ref, wio_ref, *, w_scale):
    cos_a, sin_a, first_a = ca_ref[...], sa_ref[...], rows_a_ref[2]
    cos_i, sin_i, first_i = ci_ref[...], si_ref[...], rows_i_ref[2]
    half_a = ATT_HEAD_DIM // ROPE_FRACTION // 2
    half_i = IDX_HEAD_DIM // ROPE_FRACTION // 2
    for h in range(ATT_HEADS):
        sl = slice(h * LANES, (h + 1) * LANES)
        qo_ref[:, sl] = (_rope_apply(q_ref[:, sl].astype(F32), cos_a, sin_a, first_a, half_a)
                         * ATT_LOGIT_SCALE).astype(BF16)
    for h in range(ATT_KV_HEADS):
        sl = slice(h * LANES, (h + 1) * LANES)
        ko_ref[:, sl] = _rope_apply(kv_ref[:, sl].astype(F32), cos_a, sin_a, first_a, half_a).astype(BF16)
    vo_ref[...] = kv_ref[:, ATT_KV_W:].astype(F32).T.astype(BF16)
    for h in range(IDX_Q_W // LANES):
        sl = slice(h * LANES, (h + 1) * LANES)
        qio_ref[:, sl] = _rope_apply(qi_ref[:, sl].astype(F32), cos_i, sin_i, first_i, half_i).astype(BF16)
    sm = sm_ref[...]
    ki = _rope_apply(sm, cos_i, sin_i, first_i, half_i)
    kio_ref[...] = ki[:, :IDX_HEAD_DIM].astype(BF16)
    wio_ref[...] = sm.T[IDX_HEAD_DIM:IDX_HEAD_DIM + IDX_HEADS, :] * w_scale


def _rope_split(proj, side, tables, tm=512):
    m = proj.shape[0]
    rows_a, rows_i = _rope_consts()
    w_scale = (IDX_HEADS ** -0.5) * (IDX_HEAD_DIM ** -0.5)
    rows_spec = pl.BlockSpec((3, 1, LANES), lambda i: (0, 0, 0))
    tab = pl.BlockSpec((tm, LANES), lambda i: (i, 0))
    outs = pl.pallas_call(
        functools.partial(_rope_kernel, w_scale=w_scale),
        grid=(m // tm,),
        in_specs=[
            tab, tab, tab, tab,
            rows_spec, rows_spec,
            pl.BlockSpec((tm, ATT_Q_W), lambda i: (i, 0)),
            pl.BlockSpec((tm, 2 * ATT_KV_W), lambda i: (i, ATT_Q_W // (2 * ATT_KV_W))),
            pl.BlockSpec((tm, IDX_Q_W), lambda i: (i, (ATT_Q_W + 2 * ATT_KV_W) // IDX_Q_W)),
            pl.BlockSpec((tm, LANES), lambda i: (i, _SIDE_SMALL // LANES)),
        ],
        out_specs=[
            pl.BlockSpec((tm, ATT_Q_W), lambda i: (i, 0)),
            pl.BlockSpec((tm, ATT_KV_W), lambda i: (i, 0)),
            pl.BlockSpec((ATT_KV_W, tm), lambda i: (0, i)),
            pl.BlockSpec((tm, IDX_Q_W), lambda i: (i, 0)),
            pl.BlockSpec((tm, IDX_HEAD_DIM), lambda i: (i, 0)),
            pl.BlockSpec((IDX_HEADS, tm), lambda i: (0, i)),
        ],
        out_shape=[
            jax.ShapeDtypeStruct((m, ATT_Q_W), BF16),
            jax.ShapeDtypeStruct((m, ATT_KV_W), BF16),
            jax.ShapeDtypeStruct((ATT_KV_W, m), BF16),
            jax.ShapeDtypeStruct((m, IDX_Q_W), BF16),
            jax.ShapeDtypeStruct((m, IDX_HEAD_DIM), BF16),
            jax.ShapeDtypeStruct((IDX_HEADS, m), F32),
        ],
        compiler_params=_cparams(("arbitrary",)),
        name="rope_split",
    )(*tables, rows_a, rows_i, proj, proj, proj, side)
    return outs


def _attn_kernel(qi_ref, wi_ref, ki_ref, q_ref, k_ref, vt_ref, o_ref, skey_ref,
                 da_ref, db_ref, sa_ref, sb_ref, *, tq, kc, topk, seq):
    i = pl.program_id(1)
    nchunk = lax.shift_right_logical((i + 1) * tq + (kc - 1), int(np.log2(kc)))
    qpos = i * tq + lax.broadcasted_iota(jnp.int32, (1, tq), 1)
    rb = 8 * SUBLANES

    qi = qi_ref[0]
    wi = wi_ref[...]
    qi_s = jnp.concatenate([qi[:, h * IDX_HEAD_DIM:(h + 1) * IDX_HEAD_DIM] for h in range(IDX_HEADS)],
                           axis=0)

    kh = kc // 2
    last_half = seq // kh - 1

    def score_dots(half, dst_ref):
        k0 = pl.multiple_of(jnp.minimum(half, last_half) * kh, kh)
        dst_ref[...] = _dot_nt(ki_ref[0, pl.ds(k0, kh), :], qi_s)

    def score_keys(half, src_ref):
        k0 = pl.multiple_of(half * kh, kh)
        for j in range(kh // rb):
            rows = slice(j * rb, (j + 1) * rb)
            acc = jnp.zeros((rb, tq), F32)
            for h in range(IDX_HEADS):
                acc = acc + jnp.maximum(src_ref[rows, h * tq:(h + 1) * tq], 0.0) * wi[h:h + 1, :]
            kpos = k0 + j * rb + lax.broadcasted_iota(jnp.int32, (rb, tq), 0)
            bits = pltpu.bitcast(acc, jnp.int32)
            skey = jnp.where(bits < 0, bits ^ jnp.int32(0x7FFFFFFF), bits)
            skey = jnp.where(kpos <= qpos, skey, jnp.int32(INT_MIN))
            skey_ref[pl.ds(k0 + j * rb, rb), :] = skey

    def score_chunk(c, carry):
        score_dots(2 * c + 1, db_ref)
        score_keys(2 * c, da_ref)
        score_dots(2 * c + 2, da_ref)
        score_keys(2 * c + 1, db_ref)
        return carry

    score_dots(0, da_ref)
    lax.fori_loop(0, nchunk, score_chunk, 0)

    nacc = 4

    def count_where(pred_fn):
        row_iota = lax.broadcasted_iota(jnp.int32, (SUBLANES, tq), 0)

        def chunk_body(c, parts):
            k0 = pl.multiple_of(c * kc, kc)
            parts = list(parts)
            xs = skey_ref[pl.ds(k0, kc), :]
            for j in range(kc // SUBLANES):
                x = xs[j * SUBLANES:(j + 1) * SUBLANES]
                parts[j % nacc] = parts[j % nacc] + jnp.where(pred_fn(x, k0 + j * SUBLANES + row_iota), 1, 0)
            return tuple(parts)
        zero = jnp.zeros((SUBLANES, tq), jnp.int32)
        parts = lax.fori_loop(0, nchunk, chunk_body, (zero,) * nacc)
        return jnp.sum(sum(parts[1:], parts[0]), axis=0, keepdims=True)

    nfull = seq // kc
    step = max(nfull // SEARCH_VARIANTS, 1)
    variant = lax.div(nchunk + (step - 1), step) - 1

    def fill_chunk(c, carry):
        rows = pl.ds(pl.multiple_of(c * kc, kc), kc)
        skey_ref[rows, :] = jnp.full((kc, tq), INT_MIN, jnp.int32)
        return carry

    lax.fori_loop(nchunk, (variant + 1) * step, fill_chunk, 0)

    def search_all(nc):
        def bit_body(b, prefix):
            cand = prefix | lax.shift_left(jnp.int32(1), 31 - b)
            cand_s = jnp.broadcast_to(cand ^ jnp.int32(INT_MIN), (SUBLANES, tq))
            parts = [jnp.zeros((SUBLANES, tq), jnp.int32)] * nacc
            for j in range(nc * kc // SUBLANES):
                x = skey_ref[j * SUBLANES:(j + 1) * SUBLANES, :]
                parts[j % nacc] = parts[j % nacc] + jnp.where(x >= cand_s, 1, 0)
            cnt = jnp.sum(sum(parts[1:], parts[0]), axis=0, keepdims=True)
            return jnp.where(cnt >= topk, cand, prefix)
        return lax.fori_loop(0, 32, bit_body, jnp.zeros((1, tq), jnp.int32))

    prefix = lax.switch(variant, [functools.partial(search_all, nc) for nc in range(step, nfull + 1, step)])
    thr = prefix ^ jnp.int32(INT_MIN)

    n_gt = count_where(lambda x, _: x > thr)
    n_eq = count_where(lambda x, _: x == thr)
    need = topk - n_gt
    excess = jnp.where((prefix != 0) & (n_eq > need), 1, 0)
    any_excess = jnp.max(excess) > 0
    idx_bits = int(np.log2(seq))

    def tie_search():
        def tie_bit(b, x):
            cand = x | lax.shift_left(jnp.int32(1), idx_bits - 1 - b)
            below = count_where(lambda xk, kidx: (xk == thr) & (kidx < cand))
            return jnp.where(below < need, cand, x)
        return lax.fori_loop(0, idx_bits, tie_bit, jnp.zeros((1, tq), jnp.int32))

    tie_last = lax.cond(any_excess, tie_search, lambda: jnp.full((1, tq), seq, jnp.int32))

    groups = ATT_KV_HEADS
    rep = ATT_HEADS // ATT_KV_HEADS
    q = q_ref[0]
    q_g = [jnp.concatenate([q[:, (g * rep + r) * LANES:(g * rep + r + 1) * LANES] for r in range(rep)], axis=0)
           for g in range(groups)]

    def qk_dots(half, dst_ref):
        k0 = pl.multiple_of(jnp.minimum(half, last_half) * kh, kh)
        for g in range(groups):
            dst_ref[g] = _dot_nt(k_ref[0, pl.ds(k0, kh), g * LANES:(g + 1) * LANES], q_g[g])

    def softmax_pv(half, src_ref, carry):
        k0 = pl.multiple_of(half * kh, kh)
        x = skey_ref[pl.ds(k0, kh), :]
        kidx = k0 + lax.broadcasted_iota(jnp.int32, (kh, tq), 0)
        sel = (x > thr) | ((x == thr) & (kidx <= tie_last))
        sel = sel & (x != jnp.int32(INT_MIN))
        bias = jnp.where(sel, 0.0, MASK_VALUE)
        bias = jnp.concatenate([bias] * rep, axis=1)
        new = []
        for g in range(groups):
            m_old, l_old, acc_old = carry[g]
            vtg = vt_ref[g * LANES:(g + 1) * LANES, pl.ds(k0, kh)]
            sm = src_ref[g] + bias
            m_new = jnp.maximum(m_old, jnp.max(sm, axis=0, keepdims=True))
            p = jnp.exp2(sm - m_new)
            alpha = jnp.exp2(m_old - m_new)
            l_new = alpha * l_old + jnp.sum(p, axis=0, keepdims=True)
            acc_new = alpha * acc_old + _dot(vtg, p.astype(BF16))
            new.append((m_new, l_new, acc_new))
        return tuple(new)

    def attn_chunk(c, carry):
        qk_dots(2 * c + 1, sb_ref)
        carry = softmax_pv(2 * c, sa_ref, carry)
        qk_dots(2 * c + 2, sa_ref)
        return softmax_pv(2 * c + 1, sb_ref, carry)

    init = tuple((jnp.full((1, rep * tq), MASK_VALUE, F32), jnp.zeros((1, rep * tq), F32),
                  jnp.zeros((LANES, rep * tq), F32)) for _ in range(groups))
    qk_dots(0, sa_ref)
    fin = lax.fori_loop(0, nchunk, attn_chunk, init)
    for g in range(groups):
        _, l_f, acc_f = fin[g]
        o_t = acc_f / l_f
        for r in range(rep):
            h = g * rep + r
            o_ref[0, :, h * LANES:(h + 1) * LANES] = o_t[:, r * tq:(r + 1) * tq].T.astype(o_ref.dtype)


def _dsa_attention(q, k, vt, qi, ki, wit, batch, seq, topk, tq=128, kc=512):
    nblk = seq // tq
    rep = ATT_HEADS // ATT_KV_HEADS
    r3 = lambda a: a.reshape(batch, seq, a.shape[-1])
    qblk = lambda w: pl.BlockSpec((1, tq, w), lambda b, i: (b, i, 0))
    full = lambda w: pl.BlockSpec((1, seq, w), lambda b, i: (b, 0, 0))
    out = pl.pallas_call(
        functools.partial(_attn_kernel, tq=tq, kc=kc, topk=topk, seq=seq),
        grid=(batch, nblk),
        in_specs=[qblk(IDX_Q_W), pl.BlockSpec((IDX_HEADS, tq), lambda b, i: (0, b * nblk + i)),
                  full(IDX_HEAD_DIM), qblk(ATT_Q_W), full(ATT_KV_W),
                  pl.BlockSpec((ATT_KV_W, seq), lambda b, i: (0, b))],
        out_specs=qblk(ATT_Q_W),
        out_shape=jax.ShapeDtypeStruct((batch, seq, ATT_Q_W), BF16),
        scratch_shapes=[pltpu.VMEM((seq, tq), jnp.int32),
                        pltpu.VMEM((kc // 2, IDX_HEADS * tq), F32), pltpu.VMEM((kc // 2, IDX_HEADS * tq), F32),
                        pltpu.VMEM((ATT_KV_HEADS, kc // 2, rep * tq), F32),
                        pltpu.VMEM((ATT_KV_HEADS, kc // 2, rep * tq), F32)],
        compiler_params=_cparams(("arbitrary", "arbitrary")),
        name="dsa_attention",
    )(r3(qi), wit, r3(ki), r3(q), r3(k), vt)
    return out.reshape(batch * seq, ATT_Q_W)


def _conv_kernel(cb_ref, cc_ref, cx_ref, hc_ref, hx_ref, w_ref, o_ref, *, tm, seq):
    i = pl.program_id(0)
    u = cc_ref[...].astype(F32) * cx_ref[...].astype(F32)
    halo = hc_ref[...].astype(F32) * hx_ref[...].astype(F32)
    halo = jnp.where((i * tm) % seq == 0, 0.0, halo)
    row = lax.broadcasted_iota(jnp.int32, u.shape, 0)
    u1 = pltpu.roll(u, 1, axis=0)
    u2 = pltpu.roll(u, 2, axis=0)
    h1 = halo[HALO_ROWS - 1:HALO_ROWS, :]
    h2 = halo[HALO_ROWS - 2:HALO_ROWS - 1, :]
    u1 = jnp.where(row == 0, h1, u1)
    u2 = jnp.where(row == 0, h2, jnp.where(row == 1, h1, u2))
    w = w_ref[...]
    conv = u2 * w[0:1, :] + u1 * w[1:2, :] + u * w[2:3, :]
    o_ref[...] = (cb_ref[...].astype(F32) * conv).astype(o_ref.dtype)


def _short_conv(proj, conv_w, seq, col0, tm=512, tc=512):
    m = proj.shape[0]
    nb = CONV_WIDTH // tc
    c0 = col0 // tc
    rpb = tm // HALO_ROWS
    main = lambda off: pl.BlockSpec((tm, tc), lambda i, j: (i, c0 + off * nb + j))
    halo = lambda off: pl.BlockSpec((HALO_ROWS, tc),
                                    lambda i, j: (jnp.maximum(i * rpb - 1, 0), c0 + off * nb + j))
    return pl.pallas_call(
        functools.partial(_conv_kernel, tm=tm, seq=seq),
        grid=(m // tm, nb),
        in_specs=[main(0), main(1), main(2), halo(1), halo(2),
                  pl.BlockSpec((CONV_K, tc), lambda i, j: (0, j))],
        out_specs=pl.BlockSpec((tm, tc), lambda i, j: (i, j)),
        out_shape=jax.ShapeDtypeStruct((m, CONV_WIDTH), BF16),
        compiler_params=_cparams(("arbitrary", "arbitrary")),
        name="short_conv",
    )(proj, proj, proj, proj, proj, conv_w)


def _hg_tables():
    c = HG_CHUNK
    t = np.arange(c)
    mats, masks = [], []
    for l in range(HG_LEVELS):
        h = 1 << l
        start = (t // (2 * h)) * (2 * h)
        p = start + h - 1
        right = (t - start) >= h
        u = t[None, :]
        r_m = right[:, None] & (u > p[:, None]) & (u <= t[:, None])
        l_m = (~right)[:, None] & (u > t[:, None]) & (u <= p[:, None])
        mats.append(r_m | l_m)
        same = (t[:, None] // (2 * h)) == (t[None, :] // (2 * h))
        masks.append(same & right[:, None] & (~right)[None, :])
    mats.append(t[None, :] <= t[:, None])
    mats.append(t[None, :] > t[:, None])
    mat = np.concatenate(mats, 0).astype(np.float32)
    return np.concatenate([mat] * HG_SPLIT, 1), np.stack(masks).astype(np.float32)


def _split3(x):
    hi = x.astype(BF16)
    r1 = x - hi.astype(F32)
    mid = r1.astype(BF16)
    lo = (r1 - mid.astype(F32)).astype(BF16)
    return jnp.concatenate([hi, mid, lo], axis=0)


def _hg_kernel(hq_ref, hf_ref, hi_ref, hg_ref, lb_ref, g_ref, mat_ref, mask_ref, o_ref, state_ref, *, ts):
    c = HG_CHUNK

    @pl.when(pl.program_id(1) == 0)
    def _():
        state_ref[...] = jnp.zeros_like(state_ref)

    g = g_ref[...]

    def chunk(ci, carry):
        rows = pl.ds(pl.multiple_of(ci * c, c), c)
        lb = lb_ref[...]
        f = lb + (1.0 - lb) * _sigmoid(hf_ref[rows, :])
        logf = jnp.log(jnp.maximum(f, F_MIN))
        kin = 1.0 - f
        q = hq_ref[rows, :].astype(F32)
        v16 = hi_ref[rows, :]
        e = jnp.exp(_dot(mat_ref[...], _split3(logf)))
        qd = [(q * e[l * c:(l + 1) * c]).astype(BF16) for l in range(HG_LEVELS + 1)]
        kd = [(kin * e[l * c:(l + 1) * c]).astype(BF16) for l in range(HG_LEVELS)]
        e_b = e[HG_LEVELS * c:(HG_LEVELS + 1) * c]
        kk = (kin * e[(HG_LEVELS + 1) * c:(HG_LEVELS + 2) * c]).astype(BF16)
        qk = q * kin
        gate = hg_ref[rows, :].astype(F32)
        gate = gate * _sigmoid(gate)
        for h in range(HG_HEADS):
            cols = slice(h * LANES, (h + 1) * LANES)
            attn = jnp.zeros((c, c), F32)
            for l in range(HG_LEVELS):
                attn = attn + _dot_nt(qd[l][:, cols], kd[l][:, cols]) * mask_ref[l]
            state_t = state_ref[h]
            v16h = v16[:, cols]
            diag = jnp.sum(qk[:, cols], axis=1, keepdims=True)
            out = (_dot(attn.astype(BF16), v16h) + diag * v16h.astype(F32)
                   + _dot_nt(qd[HG_LEVELS][:, cols], state_t.astype(BF16)))
            kv_t = lax.dot_general(v16h, kk[:, cols], (((0,), (0,)), ((), ())),
                                   preferred_element_type=F32)
            state_ref[h] = state_t * e_b[c - 1:c, cols] + kv_t
            y = out * lax.rsqrt(jnp.mean(out * out, axis=-1, keepdims=True) + NORM_EPS) * g
            o_ref[rows, cols] = (y * gate[:, cols]).astype(o_ref.dtype)
        return carry

    lax.fori_loop(0, ts // c, chunk, 0)


def _hgrn2(proj, side, lb, norm_g, batch, seq, col0, ts=512):
    m = proj.shape[0]
    c0 = col0 // HG_W
    spb = seq // ts
    mat, masks = _hg_tables()
    col = lambda off: pl.BlockSpec((ts, HG_W), lambda b, s: (b * spb + s, c0 + off))
    return pl.pallas_call(
        functools.partial(_hg_kernel, ts=ts),
        grid=(batch, spb),
        in_specs=[col(0), pl.BlockSpec((ts, HG_W), lambda b, s: (b * spb + s, 0)), col(1), col(2),
                  pl.BlockSpec((1, HG_W), lambda b, s: (0, 0)),
                  pl.BlockSpec((1, LANES), lambda b, s: (0, 0)),
                  pl.BlockSpec(mat.shape, lambda b, s: (0, 0)),
                  pl.BlockSpec(masks.shape, lambda b, s: (0, 0, 0))],
        out_specs=pl.BlockSpec((ts, HG_W), lambda b, s: (b * spb + s, 0)),
        out_shape=jax.ShapeDtypeStruct((m, HG_W), BF16),
        scratch_shapes=[pltpu.VMEM((HG_HEADS, HG_VAL_DIM, HG_KEY_DIM), F32)],
        compiler_params=_cparams(("arbitrary", "arbitrary")),
        name="hgrn2",
    )(proj, side, proj, proj, lb.reshape(1, HG_W), norm_g.reshape(1, HG_VAL_DIM),
      jnp.asarray(mat, BF16), jnp.asarray(masks))


def _lb_kernel(x_ref, o_ref):
    x = x_ref[...]
    e = jnp.exp(x - jnp.max(x, axis=0, keepdims=True))
    p = e / jnp.sum(e, axis=0, keepdims=True)
    depth = x.shape[0]
    run = jnp.zeros_like(p[0:1])
    for l in range(depth):
        run = run + p[l:l + 1]
        o_ref[l:l + 1, :] = run - p[0:1]


def _lower_bounds(hg_lower_bounds):
    return pl.pallas_call(
        _lb_kernel,
        out_shape=jax.ShapeDtypeStruct(hg_lower_bounds.shape, F32),
        name="hg_lower_bounds",
    )(hg_lower_bounds.astype(F32))


_COL_Q = 0
_COL_CONV = ATT_Q_W + 2 * ATT_KV_W + IDX_Q_W
_COL_HG = _COL_CONV + 3 * CONV_WIDTH
_COL_GATE = _COL_HG + 3 * HG_W
_SIDE_SMALL = HG_W
_SIDE_W = HG_W + LANES


def _split_w_in(w_in, d_model):
    sizes = (ATT_Q_W, ATT_KV_W, ATT_KV_W, IDX_Q_W, IDX_HEAD_DIM, IDX_HEADS,
             CONV_WIDTH, CONV_WIDTH, CONV_WIDTH, HG_W, HG_W, HG_W, HG_W, N_BRANCHES * d_model)
    offs = np.cumsum((0,) + sizes)
    part = lambda n: w_in[..., offs[n]:offs[n + 1]]
    main = jnp.concatenate([part(n) for n in (0, 1, 2, 3, 6, 7, 8, 9, 11, 12, 13)], axis=-1).astype(BF16)
    side = jnp.concatenate([part(10), part(4), part(5)], axis=-1)
    side = jnp.pad(side, ((0, 0), (0, 0), (0, _SIDE_W - side.shape[-1]))).astype(BF16)
    return main, side


def kernel(x, c, positions, ada_w, ada_b, norm_mix_g, w_in, conv_w, hg_lower_bounds, hg_norm_g,
           w_o_attn, w_o_conv, w_o_hgrn, w_out, norm_mlp_g, w_mlp1, w_mlp2, final_norm_g):
    batch, seq, d = x.shape
    depth = ada_w.shape[0]
    m = batch * seq
    topk = min(INDEX_TOPK, seq // 4)
    lb_all = _lower_bounds(hg_lower_bounds)
    mod = _modulation(c, ada_w, ada_b)
    tables = _rope_tables(positions)
    w_main, w_side = _split_w_in(w_in, d)
    wa, wc, wh, wo = (w.astype(BF16) for w in (w_o_attn, w_o_conv, w_o_hgrn, w_out))
    w1, w2 = w_mlp1.astype(BF16), w_mlp2.astype(BF16)
    x2 = x.reshape(m, d)
    mods = [[mod[l, :, n * d:(n + 1) * d] for n in range(ADA_CHUNKS)] for l in range(depth)]
    h = _norm(x2, norm_mix_g[0], seq, mods[0][0], mods[0][1])
    for l in range(depth):
        sh1, sc1, g1, sh2, sc2, g2 = mods[l]
        proj = _matmul(h, w_main, out_dtype=BF16, tn=1024, layer=l)
        side = _matmul(h, w_side, out_dtype=F32, tn=_SIDE_W, layer=l)
        q, k, vt, qi, ki, wit = _rope_split(proj, side, tables)
        att = _dsa_attention(q, k, vt, qi, ki, wit, batch, seq, topk)
        cv = _short_conv(proj, conv_w[l], seq, _COL_CONV)
        hg = _hgrn2(proj, side, lb_all[l], hg_norm_g[l], batch, seq, _COL_HG)
        x2, h = _merge_out(att, cv, hg, wa, wc, wh, proj, _COL_GATE, wo, x2, g1,
                           norm_mlp_g[l], sh2, sc2, seq, l)
        a = _matmul(h, w1, out_dtype=BF16, relu2=True, tm=256, tn=4 * d, layer=l)
        if l + 1 < depth:
            x2, h = _mlp_down(a, w2, x2, g2, norm_mix_g[l + 1], mods[l + 1][0], mods[l + 1][1], seq, l,
                              emit_x=True, h_dtype=BF16)
        else:
            zero = jnp.zeros_like(g2)
            out = _mlp_down(a, w2, x2, g2, final_norm_g, zero, zero, seq, l, emit_x=False, h_dtype=x.dtype)
    return out.reshape(batch, seq, d)
```

```python
import functools

import numpy as np
import jax
import jax.numpy as jnp
from jax import lax
from jax.experimental import pallas as pl
from jax.experimental.pallas import tpu as pltpu

ATT_HEADS = 8
ATT_KV_HEADS = 2
ATT_HEAD_DIM = 128
IDX_HEADS = 8
IDX_HEAD_DIM = 64
INDEX_TOPK = 256
ROPE_THETA = 500000.0
ROPE_FRACTION = 4
MASK_VALUE = -1e30
CONV_WIDTH = 1024
CONV_K = 3
HG_HEADS = 8
HG_KEY_DIM = 128
HG_VAL_DIM = 128
F_MIN = 1e-30
N_BRANCHES = 3
NORM_EPS = 1e-6
ADA_CHUNKS = 6

ATT_Q_W = ATT_HEADS * ATT_HEAD_DIM
ATT_KV_W = ATT_KV_HEADS * ATT_HEAD_DIM
IDX_Q_W = IDX_HEADS * IDX_HEAD_DIM
HG_W = HG_HEADS * HG_KEY_DIM
ATT_LOGIT_SCALE = (ATT_HEAD_DIM ** -0.5) * float(np.log2(np.e))

LANES = 128
SUBLANES = 8
HALO_ROWS = 16
VMEM_LIMIT_BYTES = 56 * 1024 * 1024

INT_MIN = -2 ** 31
SEARCH_VARIANTS = 8
HG_CHUNK = 64
HG_LEVELS = 6
HG_SPLIT = 3

BF16 = jnp.bfloat16
F32 = jnp.float32


def _cparams(sem):
    return pltpu.CompilerParams(dimension_semantics=sem, vmem_limit_bytes=VMEM_LIMIT_BYTES)


def _dot(a, b):
    return jnp.dot(a, b, preferred_element_type=F32)


def _dot_nt(a, b):
    return lax.dot_general(a, b, (((1,), (1,)), ((), ())), preferred_element_type=F32)


def _sigmoid(x):
    return 1.0 / (1.0 + jnp.exp(-x))


def _mod_kernel(c_ref, w_ref, b_ref, o_ref):
    c = c_ref[...]
    ca = (c * _sigmoid(c)).astype(BF16)
    o_ref[...] = _dot(ca, w_ref[...].astype(BF16)) + b_ref[...]


def _modulation(c, ada_w, ada_b):
    depth, d, n = ada_w.shape
    b = c.shape[0]
    bp = -(-b // SUBLANES) * SUBLANES
    cp = jnp.pad(c, ((0, bp - b), (0, 0)))
    tn = 1024
    out = pl.pallas_call(
        _mod_kernel,
        grid=(depth, n // tn),
        in_specs=[
            pl.BlockSpec((bp, d), lambda l, j: (0, 0)),
            pl.BlockSpec((None, d, tn), lambda l, j: (l, 0, j)),
            pl.BlockSpec((None, 1, tn), lambda l, j: (l, 0, j)),
        ],
        out_specs=pl.BlockSpec((None, bp, tn), lambda l, j: (l, 0, j)),
        out_shape=jax.ShapeDtypeStruct((depth, bp, n), F32),
        compiler_params=_cparams(("arbitrary", "arbitrary")),
        name="adaln_mod",
    )(cp, ada_w, ada_b.reshape(depth, 1, n))
    return out[:, :b]


def _norm_kernel(x_ref, g_ref, sh_ref, sc_ref, o_ref):
    x = x_ref[...]
    y = x * lax.rsqrt(jnp.mean(x * x, axis=-1, keepdims=True) + NORM_EPS) * g_ref[...]
    o_ref[...] = (y * (1.0 + sc_ref[0]) + sh_ref[0]).astype(o_ref.dtype)


def _norm(x2, g, seq, shift, scale, out_dtype=BF16, tm=512):
    m, d = x2.shape
    tpb = seq // tm
    row = pl.BlockSpec((1, 1, d), lambda i: (i // tpb, 0, 0))
    return pl.pallas_call(
        _norm_kernel,
        grid=(m // tm,),
        in_specs=[pl.BlockSpec((tm, d), lambda i: (i, 0)), pl.BlockSpec((1, d), lambda i: (0, 0)), row, row],
        out_specs=pl.BlockSpec((tm, d), lambda i: (i, 0)),
        out_shape=jax.ShapeDtypeStruct((m, d), out_dtype),
        compiler_params=_cparams(("arbitrary",)),
        name="rmsnorm",
    )(x2, g.reshape(1, d), shift[:, None, :], scale[:, None, :])


def _conv_tile(cb_ref, cc_ref, cx_ref, hc_ref, hx_ref, w_ref, at_seq_start):
    u = cc_ref[...].astype(F32) * cx_ref[...].astype(F32)
    halo = hc_ref[...].astype(F32) * hx_ref[...].astype(F32)
    halo = jnp.where(at_seq_start, 0.0, halo)
    row = lax.broadcasted_iota(jnp.int32, u.shape, 0)
    h1 = halo[HALO_ROWS - 1:HALO_ROWS, :]
    h2 = halo[HALO_ROWS - 2:HALO_ROWS - 1, :]
    u1 = jnp.where(row == 0, h1, pltpu.roll(u, 1, axis=0))
    u2 = jnp.where(row == 0, h2, jnp.where(row == 1, h1, pltpu.roll(u, 2, axis=0)))
    w = w_ref[...]
    conv = u2 * w[0:1, :] + u1 * w[1:2, :] + u * w[2:3, :]
    return cb_ref[...].astype(F32) * conv


def _merge_kernel(att_ref, cb_ref, cc_ref, cx_ref, hc_ref, hx_ref, cw_ref, hg_ref, wa_ref, wc_ref, wh_ref,
                  ga_ref, gc_ref, gh_ref, wo_ref, x_ref, gr_ref, ng_ref, sh_ref, sc_ref, o_ref, h_ref,
                  *, tm, seq):
    at_start = (pl.program_id(0) * tm) % seq == 0
    cv = _conv_tile(cb_ref, cc_ref, cx_ref, hc_ref, hx_ref, cw_ref, at_start).astype(BF16)
    merged = (_sigmoid(ga_ref[...].astype(F32)) * _dot(att_ref[...], wa_ref[...])
              + _sigmoid(gc_ref[...].astype(F32)) * _dot(cv, wc_ref[...])
              + _sigmoid(gh_ref[...].astype(F32)) * _dot(hg_ref[...], wh_ref[...]))
    y = x_ref[...] + gr_ref[0] * _dot(merged.astype(BF16), wo_ref[...])
    o_ref[...] = y
    yn = y * lax.rsqrt(jnp.mean(y * y, axis=-1, keepdims=True) + NORM_EPS) * ng_ref[...]
    h_ref[...] = (yn * (1.0 + sc_ref[0]) + sh_ref[0]).astype(h_ref.dtype)


def _merge_out(att, hg, wa, wc, wh, proj, conv_off, conv_w, gate_off, wout, x2, gate_row, norm_g, shift,
               scale, seq, layer, tm=256):
    m, d = x2.shape
    tpb = seq // tm
    g0 = gate_off // d
    c0 = conv_off // CONV_WIDTH
    rpb = tm // HALO_ROWS
    cmain = lambda off: pl.BlockSpec((tm, CONV_WIDTH), lambda i: (i, c0 + off))
    chalo = lambda off: pl.BlockSpec((HALO_ROWS, CONV_WIDTH), lambda i: (jnp.maximum(i * rpb - 1, 0), c0 + off))
    act = lambda a: pl.BlockSpec((tm, a.shape[1]), lambda i: (i, 0))
    resident = lambda w: pl.BlockSpec((None,) + w.shape[1:], lambda i: (layer, 0, 0),
                                      pipeline_mode=pl.Buffered(1))
    gate = lambda b: pl.BlockSpec((tm, d), lambda i: (i, g0 + b))
    row = pl.BlockSpec((1, 1, d), lambda i: (i // tpb, 0, 0))
    full = pl.BlockSpec((tm, d), lambda i: (i, 0))
    return pl.pallas_call(
        functools.partial(_merge_kernel, tm=tm, seq=seq),
        grid=(m // tm,),
        in_specs=[act(att), cmain(0), cmain(1), cmain(2), chalo(1), chalo(2),
                  pl.BlockSpec((CONV_K, CONV_WIDTH), lambda i: (0, 0)),
                  act(hg), resident(wa), resident(wc), resident(wh),
                  gate(0), gate(1), gate(2), resident(wout), full, row,
                  pl.BlockSpec((1, d), lambda i: (0, 0)), row, row],
        out_specs=[full, full],
        out_shape=[jax.ShapeDtypeStruct((m, d), F32), jax.ShapeDtypeStruct((m, d), BF16)],
        compiler_params=_cparams(("arbitrary",)),
        name="merge_out",
    )(att, proj, proj, proj, proj, proj, conv_w, hg, wa, wc, wh, proj, proj, proj, wout, x2,
      gate_row[:, None, :], norm_g.reshape(1, d), shift[:, None, :], scale[:, None, :])


def _mm_kernel(a_ref, w_ref, o_ref, *, relu2):
    y = _dot(a_ref[...], w_ref[...])
    if relu2:
        y = jnp.square(jnp.maximum(y, 0.0))
    o_ref[...] = y.astype(o_ref.dtype)


def _matmul(a, w, *, out_dtype, relu2=False, tm=1024, tn=512, layer=0):
    m = a.shape[0]
    kdim, n = w.shape[-2:]
    tn = min(tn, n)
    w_mode = pl.Buffered(1) if tn == n else None
    return pl.pallas_call(
        functools.partial(_mm_kernel, relu2=relu2),
        grid=(m // tm, n // tn),
        in_specs=[pl.BlockSpec((tm, kdim), lambda i, j: (i, 0)),
                  pl.BlockSpec((None, kdim, tn), lambda i, j: (layer, 0, j), pipeline_mode=w_mode)],
        out_specs=pl.BlockSpec((tm, tn), lambda i, j: (i, j)),
        out_shape=jax.ShapeDtypeStruct((m, n), out_dtype),
        compiler_params=_cparams(("arbitrary", "arbitrary")),
        name="matmul_relu2" if relu2 else "matmul",
    )(a, w)


def _mlp_down_kernel(a_ref, w_ref, x_ref, gr_ref, ng_ref, sh_ref, sc_ref, *outs, emit_x):
    y = x_ref[...] + gr_ref[0] * _dot(a_ref[...], w_ref[...])
    if emit_x:
        outs[0][...] = y
    yn = y * lax.rsqrt(jnp.mean(y * y, axis=-1, keepdims=True) + NORM_EPS) * ng_ref[...]
    outs[-1][...] = (yn * (1.0 + sc_ref[0]) + sh_ref[0]).astype(outs[-1].dtype)


def _mlp_down(a, w, x2, gate_row, norm_g, shift, scale, seq, layer, *, emit_x, h_dtype, tm=256):
    m, d = x2.shape
    kdim = w.shape[1]
    tpb = seq // tm
    row = pl.BlockSpec((1, 1, d), lambda i: (i // tpb, 0, 0))
    full = pl.BlockSpec((tm, d), lambda i: (i, 0))
    out_specs, out_shape = [full], [jax.ShapeDtypeStruct((m, d), h_dtype)]
    if emit_x:
        out_specs, out_shape = [full] + out_specs, [jax.ShapeDtypeStruct((m, d), F32)] + out_shape
    outs = pl.pallas_call(
        functools.partial(_mlp_down_kernel, emit_x=emit_x),
        grid=(m // tm,),
        in_specs=[pl.BlockSpec((tm, kdim), lambda i: (i, 0)),
                  pl.BlockSpec((None, kdim, d), lambda i: (layer, 0, 0), pipeline_mode=pl.Buffered(1)),
                  full, row, pl.BlockSpec((1, d), lambda i: (0, 0)), row, row],
        out_specs=out_specs,
        out_shape=out_shape,
        compiler_params=_cparams(("arbitrary",)),
        name="mlp_down",
    )(a, w, x2, gate_row[:, None, :], norm_g.reshape(1, d), shift[:, None, :], scale[:, None, :])
    return outs if emit_x else outs[0]


def _rope_rows(period, half, rot):
    j = np.arange(LANES) % period
    inv = ROPE_THETA ** (-(np.arange(half, dtype=np.float32)) / np.float32(half))
    inv_row = np.where(j < rot, inv.astype(np.float32)[j % half], 0.0).astype(np.float32)
    sign_row = np.where(j < half, -1.0, np.where(j < rot, 1.0, 0.0)).astype(np.float32)
    first_row = (j < half).astype(np.float32)
    return np.stack([inv_row, sign_row, first_row])[:, None, :]


def _rope_apply(x, cos_t, sin_s, first, half):
    up = pltpu.roll(x, LANES - half, axis=1)
    dn = pltpu.roll(x, half, axis=1)
    sw = jnp.where(first > 0.5, up, dn)
    return x * cos_t + sw * sin_s


def _rope_table_kernel(pos_ref, rows_a_ref, rows_i_ref, ca_ref, sa_ref, ci_ref, si_ref):
    pos = pos_ref[...].astype(F32)
    ang_a = pos * rows_a_ref[0]
    ca_ref[...] = jnp.cos(ang_a)
    sa_ref[...] = jnp.sin(ang_a) * rows_a_ref[1]
    ang_i = pos * rows_i_ref[0]
    ci_ref[...] = jnp.cos(ang_i)
    si_ref[...] = jnp.sin(ang_i) * rows_i_ref[1]


def _rope_consts():
    rows_a = jnp.asarray(_rope_rows(ATT_HEAD_DIM, ATT_HEAD_DIM // ROPE_FRACTION // 2,
                                    ATT_HEAD_DIM // ROPE_FRACTION))
    rows_i = jnp.asarray(_rope_rows(IDX_HEAD_DIM, IDX_HEAD_DIM // ROPE_FRACTION // 2,
                                    IDX_HEAD_DIM // ROPE_FRACTION))
    return rows_a, rows_i


def _rope_tables(positions, tm=512):
    m = positions.size
    rows_a, rows_i = _rope_consts()
    rows_spec = pl.BlockSpec((3, 1, LANES), lambda i: (0, 0, 0))
    tab = pl.BlockSpec((tm, LANES), lambda i: (i, 0))
    return pl.pallas_call(
        _rope_table_kernel,
        grid=(m // tm,),
        in_specs=[pl.BlockSpec((tm, 1), lambda i: (i, 0)), rows_spec, rows_spec],
        out_specs=[tab] * 4,
        out_shape=[jax.ShapeDtypeStruct((m, LANES), F32)] * 4,
        compiler_params=_cparams(("arbitrary",)),
        name="rope_tables",
    )(positions.reshape(m, 1), rows_a, rows_i)


def _rope_kernel(ca_ref, sa_ref, ci_ref, si_ref, rows_a_ref, rows_i_ref, q_ref, kv_ref, qi_ref, sm_ref,
                 qo_ref, ko_ref, vo_ref, qio_ref, kio_ref, wio_ref, *, w_scale):
    cos_a, sin_a, first_a = ca_ref[...], sa_ref[...], rows_a_ref[2]
    cos_i, sin_i, first_i = ci_ref[...], si_ref[...], rows_i_ref[2]
    half_a = ATT_HEAD_DIM // ROPE_FRACTION // 2
    half_i = IDX_HEAD_DIM // ROPE_FRACTION // 2
    for h in range(ATT_HEADS):
        sl = slice(h * LANES, (h + 1) * LANES)
        qo_ref[:, sl] = (_rope_apply(q_ref[:, sl].astype(F32), cos_a, sin_a, first_a, half_a)
                         * ATT_LOGIT_SCALE).astype(BF16)
    for h in range(ATT_KV_HEADS):
        sl = slice(h * LANES, (h + 1) * LANES)
        ko_ref[:, sl] = _rope_apply(kv_ref[:, sl].astype(F32), cos_a, sin_a, first_a, half_a).astype(BF16)
    vo_ref[...] = kv_ref[:, ATT_KV_W:].astype(F32).T.astype(BF16)
    for h in range(IDX_Q_W // LANES):
        sl = slice(h * LANES, (h + 1) * LANES)
        qio_ref[:, sl] = _rope_apply(qi_ref[:, sl].astype(F32), cos_i, sin_i, first_i, half_i).astype(BF16)
    sm = sm_ref[...]
    ki = _rope_apply(sm, cos_i, sin_i, first_i, half_i)
    kio_ref[...] = ki[:, :IDX_HEAD_DIM].astype(BF16)
    wio_ref[...] = sm.T[IDX_HEAD_DIM:IDX_HEAD_DIM + IDX_HEADS, :] * w_scale


def _rope_split(proj, side, tables, tm=512):
    m = proj.shape[0]
    rows_a, rows_i = _rope_consts()
    w_scale = (IDX_HEADS ** -0.5) * (IDX_HEAD_DIM ** -0.5)
    rows_spec = pl.BlockSpec((3, 1, LANES), lambda i: (0, 0, 0))
    tab = pl.BlockSpec((tm, LANES), lambda i: (i, 0))
    outs = pl.pallas_call(
        functools.partial(_rope_kernel, w_scale=w_scale),
        grid=(m // tm,),
        in_specs=[
            tab, tab, tab, tab,
            rows_spec, rows_spec,
            pl.BlockSpec((tm, ATT_Q_W), lambda i: (i, 0)),
            pl.BlockSpec((tm, 2 * ATT_KV_W), lambda i: (i, ATT_Q_W // (2 * ATT_KV_W))),
            pl.BlockSpec((tm, IDX_Q_W), lambda i: (i, (ATT_Q_W + 2 * ATT_KV_W) // IDX_Q_W)),
            pl.BlockSpec((tm, LANES), lambda i: (i, _SIDE_SMALL // LANES)),
        ],
        out_specs=[
            pl.BlockSpec((tm, ATT_Q_W), lambda i: (i, 0)),
            pl.BlockSpec((tm, ATT_KV_W), lambda i: (i, 0)),
            pl.BlockSpec((ATT_KV_W, tm), lambda i: (0, i)),
            pl.BlockSpec((tm, IDX_Q_W), lambda i: (i, 0)),
            pl.BlockSpec((tm, IDX_HEAD_DIM), lambda i: (i, 0)),
            pl.BlockSpec((IDX_HEADS, tm), lambda i: (0, i)),
        ],
        out_shape=[
            jax.ShapeDtypeStruct((m, ATT_Q_W), BF16),
            jax.ShapeDtypeStruct((m, ATT_KV_W), BF16),
            jax.ShapeDtypeStruct((ATT_KV_W, m), BF16),
            jax.ShapeDtypeStruct((m, IDX_Q_W), BF16),
            jax.ShapeDtypeStruct((m, IDX_HEAD_DIM), BF16),
            jax.ShapeDtypeStruct((IDX_HEADS, m), F32),
        ],
        compiler_params=_cparams(("arbitrary",)),
        name="rope_split",
    )(*tables, rows_a, rows_i, proj, proj, proj, side)
    return outs


def _attn_kernel(qi_ref, wi_ref, ki_ref, q_ref, k_ref, vt_ref, o_ref, skey_ref,
                 da_ref, db_ref, sa_ref, sb_ref, *, tq, kc, topk, seq):
    i = pl.program_id(1)
    nchunk = lax.shift_right_logical((i + 1) * tq + (kc - 1), int(np.log2(kc)))
    qpos = i * tq + lax.broadcasted_iota(jnp.int32, (1, tq), 1)
    rb = 8 * SUBLANES

    qi = qi_ref[0]
    wi = wi_ref[...]
    qi_s = jnp.concatenate([qi[:, h * IDX_HEAD_DIM:(h + 1) * IDX_HEAD_DIM] for h in range(IDX_HEADS)],
                           axis=0)

    kh = kc // 2
    last_half = seq // kh - 1

    def score_dots(half, dst_ref):
        k0 = pl.multiple_of(jnp.minimum(half, last_half) * kh, kh)
        dst_ref[...] = _dot_nt(ki_ref[0, pl.ds(k0, kh), :], qi_s)

    def score_keys(half, src_ref):
        k0 = pl.multiple_of(half * kh, kh)
        for j in range(kh // rb):
            rows = slice(j * rb, (j + 1) * rb)
            acc = jnp.zeros((rb, tq), F32)
            for h in range(IDX_HEADS):
                acc = acc + jnp.maximum(src_ref[rows, h * tq:(h + 1) * tq], 0.0) * wi[h:h + 1, :]
            kpos = k0 + j * rb + lax.broadcasted_iota(jnp.int32, (rb, tq), 0)
            bits = pltpu.bitcast(acc, jnp.int32)
            skey = jnp.where(bits < 0, bits ^ jnp.int32(0x7FFFFFFF), bits)
            skey = jnp.where(kpos <= qpos, skey, jnp.int32(INT_MIN))
            skey_ref[pl.ds(k0 + j * rb, rb), :] = skey

    def score_chunk(c, carry):
        score_dots(2 * c + 1, db_ref)
        score_keys(2 * c, da_ref)
        score_dots(2 * c + 2, da_ref)
        score_keys(2 * c + 1, db_ref)
        return carry

    score_dots(0, da_ref)
    lax.fori_loop(0, nchunk, score_chunk, 0)

    nacc = 4

    def count_where(pred_fn):
        row_iota = lax.broadcasted_iota(jnp.int32, (SUBLANES, tq), 0)

        def chunk_body(c, parts):
            k0 = pl.multiple_of(c * kc, kc)
            parts = list(parts)
            xs = skey_ref[pl.ds(k0, kc), :]
            for j in range(kc // SUBLANES):
                x = xs[j * SUBLANES:(j + 1) * SUBLANES]
                parts[j % nacc] = parts[j % nacc] + jnp.where(pred_fn(x, k0 + j * SUBLANES + row_iota), 1, 0)
            return tuple(parts)
        zero = jnp.zeros((SUBLANES, tq), jnp.int32)
        parts = lax.fori_loop(0, nchunk, chunk_body, (zero,) * nacc)
        return jnp.sum(sum(parts[1:], parts[0]), axis=0, keepdims=True)

    nfull = seq // kc
    step = max(nfull // SEARCH_VARIANTS, 1)
    variant = lax.div(nchunk + (step - 1), step) - 1

    def fill_chunk(c, carry):
        rows = pl.ds(pl.multiple_of(c * kc, kc), kc)
        skey_ref[rows, :] = jnp.full((kc, tq), INT_MIN, jnp.int32)
        return carry

    lax.fori_loop(nchunk, (variant + 1) * step, fill_chunk, 0)

    def search_all(nc):
        def bit_body(b, prefix):
            cand = prefix | lax.shift_left(jnp.int32(1), 31 - b)
            cand_s = jnp.broadcast_to(cand ^ jnp.int32(INT_MIN), (SUBLANES, tq))
            parts = [jnp.zeros((SUBLANES, tq), jnp.int32)] * nacc
            for j in range(nc * kc // SUBLANES):
                x = skey_ref[j * SUBLANES:(j + 1) * SUBLANES, :]
                parts[j % nacc] = parts[j % nacc] + jnp.where(x >= cand_s, 1, 0)
            cnt = jnp.sum(sum(parts[1:], parts[0]), axis=0, keepdims=True)
            return jnp.where(cnt >= topk, cand, prefix)
        return lax.fori_loop(0, 32, bit_body, jnp.zeros((1, tq), jnp.int32))

    prefix = lax.switch(variant, [functools.partial(search_all, nc) for nc in range(step, nfull + 1, step)])
    thr = prefix ^ jnp.int32(INT_MIN)

    n_gt = count_where(lambda x, _: x > thr)
    n_eq = count_where(lambda x, _: x == thr)
    need = topk - n_gt
    excess = jnp.where((prefix != 0) & (n_eq > need), 1, 0)
    any_excess = jnp.max(excess) > 0
    idx_bits = int(np.log2(seq))

    def tie_search():
        def tie_bit(b, x):
            cand = x | lax.shift_left(jnp.int32(1), idx_bits - 1 - b)
            below = count_where(lambda xk, kidx: (xk == thr) & (kidx < cand))
            return jnp.where(below < need, cand, x)
        return lax.fori_loop(0, idx_bits, tie_bit, jnp.zeros((1, tq), jnp.int32))

    tie_last = lax.cond(any_excess, tie_search, lambda: jnp.full((1, tq), seq, jnp.int32))

    groups = ATT_KV_HEADS
    rep = ATT_HEADS // ATT_KV_HEADS
    q = q_ref[0]
    q_g = [jnp.concatenate([q[:, (g * rep + r) * LANES:(g * rep + r + 1) * LANES] for r in range(rep)], axis=0)
           for g in range(groups)]

    def qk_dots(half, dst_ref):
        k0 = pl.multiple_of(jnp.minimum(half, last_half) * kh, kh)
        for g in range(groups):
            dst_ref[g] = _dot_nt(k_ref[0, pl.ds(k0, kh), g * LANES:(g + 1) * LANES], q_g[g])

    def softmax_pv(half, src_ref, carry):
        k0 = pl.multiple_of(half * kh, kh)
        x = skey_ref[pl.ds(k0, kh), :]
        kidx = k0 + lax.broadcasted_iota(jnp.int32, (kh, tq), 0)
        sel = (x > thr) | ((x == thr) & (kidx <= tie_last))
        sel = sel & (x != jnp.int32(INT_MIN))
        bias = jnp.where(sel, 0.0, MASK_VALUE)
        bias = jnp.concatenate([bias] * rep, axis=1)
        new = []
        for g in range(groups):
            m_old, l_old, acc_old = carry[g]
            vtg = vt_ref[g * LANES:(g + 1) * LANES, pl.ds(k0, kh)]
            sm = src_ref[g] + bias
            m_new = jnp.maximum(m_old, jnp.max(sm, axis=0, keepdims=True))
            p = jnp.exp2(sm - m_new)
            alpha = jnp.exp2(m_old - m_new)
            l_new = alpha * l_old + jnp.sum(p, axis=0, keepdims=True)
            acc_new = alpha * acc_old + _dot(vtg, p.astype(BF16))
            new.append((m_new, l_new, acc_new))
        return tuple(new)

    def attn_chunk(c, carry):
        qk_dots(2 * c + 1, sb_ref)
        carry = softmax_pv(2 * c, sa_ref, carry)
        qk_dots(2 * c + 2, sa_ref)
        return softmax_pv(2 * c + 1, sb_ref, carry)

    init = tuple((jnp.full((1, rep * tq), MASK_VALUE, F32), jnp.zeros((1, rep * tq), F32),
                  jnp.zeros((LANES, rep * tq), F32)) for _ in range(groups))
    qk_dots(0, sa_ref)
    fin = lax.fori_loop(0, nchunk, attn_chunk, init)
    for g in range(groups):
        _, l_f, acc_f = fin[g]
        o_t = acc_f / l_f
        for r in range(rep):
            h = g * rep + r
            o_ref[0, :, h * LANES:(h + 1) * LANES] = o_t[:, r * tq:(r + 1) * tq].T.astype(o_ref.dtype)


def _dsa_attention(q, k, vt, qi, ki, wit, batch, seq, topk, tq=128, kc=512):
    nblk = seq // tq
    rep = ATT_HEADS // ATT_KV_HEADS
    r3 = lambda a: a.reshape(batch, seq, a.shape[-1])
    qblk = lambda w: pl.BlockSpec((1, tq, w), lambda b, i: (b, i, 0))
    full = lambda w: pl.BlockSpec((1, seq, w), lambda b, i: (b, 0, 0))
    out = pl.pallas_call(
        functools.partial(_attn_kernel, tq=tq, kc=kc, topk=topk, seq=seq),
        grid=(batch, nblk),
        in_specs=[qblk(IDX_Q_W), pl.BlockSpec((IDX_HEADS, tq), lambda b, i: (0, b * nblk + i)),
                  full(IDX_HEAD_DIM), qblk(ATT_Q_W), full(ATT_KV_W),
                  pl.BlockSpec((ATT_KV_W, seq), lambda b, i: (0, b))],
        out_specs=qblk(ATT_Q_W),
        out_shape=jax.ShapeDtypeStruct((batch, seq, ATT_Q_W), BF16),
        scratch_shapes=[pltpu.VMEM((seq, tq), jnp.int32),
                        pltpu.VMEM((kc // 2, IDX_HEADS * tq), F32), pltpu.VMEM((kc // 2, IDX_HEADS * tq), F32),
                        pltpu.VMEM((ATT_KV_HEADS, kc // 2, rep * tq), F32),
                        pltpu.VMEM((ATT_KV_HEADS, kc // 2, rep * tq), F32)],
        compiler_params=_cparams(("arbitrary", "arbitrary")),
        name="dsa_attention",
    )(r3(qi), wit, r3(ki), r3(q), r3(k), vt)
    return out.reshape(batch * seq, ATT_Q_W)


def _hg_tables():
    c = HG_CHUNK
    t = np.arange(c)
    mats, masks = [], []
    for l in range(HG_LEVELS):
        h = 1 << l
        start = (t // (2 * h)) * (2 * h)
        p = start + h - 1
        right = (t - start) >= h
        u = t[None, :]
        r_m = right[:, None] & (u > p[:, None]) & (u <= t[:, None])
        l_m = (~right)[:, None] & (u > t[:, None]) & (u <= p[:, None])
        mats.append(r_m | l_m)
        same = (t[:, None] // (2 * h)) == (t[None, :] // (2 * h))
        masks.append(same & right[:, None] & (~right)[None, :])
    mats.append(t[None, :] <= t[:, None])
    mats.append(t[None, :] > t[:, None])
    mat = np.concatenate(mats, 0).astype(np.float32)
    return np.concatenate([mat] * HG_SPLIT, 1), np.stack(masks).astype(np.float32)


def _split3(x):
    hi = x.astype(BF16)
    r1 = x - hi.astype(F32)
    mid = r1.astype(BF16)
    lo = (r1 - mid.astype(F32)).astype(BF16)
    return jnp.concatenate([hi, mid, lo], axis=0)


def _hg_kernel(hq_ref, hf_ref, hi_ref, hg_ref, lb_ref, g_ref, mat_ref, mask_ref, o_ref, state_ref, *, ts):
    c = HG_CHUNK

    @pl.when(pl.program_id(1) == 0)
    def _():
        state_ref[...] = jnp.zeros_like(state_ref)

    g = g_ref[...]

    def chunk(ci, carry):
        rows = pl.ds(pl.multiple_of(ci * c, c), c)
        lb = lb_ref[...]
        f = lb + (1.0 - lb) * _sigmoid(hf_ref[rows, :])
        logf = jnp.log(jnp.maximum(f, F_MIN))
        kin = 1.0 - f
        q = hq_ref[rows, :].astype(F32)
        v16 = hi_ref[rows, :]
        e = jnp.exp(_dot(mat_ref[...], _split3(logf)))
        qd = [(q * e[l * c:(l + 1) * c]).astype(BF16) for l in range(HG_LEVELS + 1)]
        kd = [(kin * e[l * c:(l + 1) * c]).astype(BF16) for l in range(HG_LEVELS)]
        e_b = e[HG_LEVELS * c:(HG_LEVELS + 1) * c]
        kk = (kin * e[(HG_LEVELS + 1) * c:(HG_LEVELS + 2) * c]).astype(BF16)
        qk = q * kin
        gate = hg_ref[rows, :].astype(F32)
        gate = gate * _sigmoid(gate)
        for h in range(HG_HEADS):
            cols = slice(h * LANES, (h + 1) * LANES)
            attn = jnp.zeros((c, c), F32)
            for l in range(HG_LEVELS):
                attn = attn + _dot_nt(qd[l][:, cols], kd[l][:, cols]) * mask_ref[l]
            state_t = state_ref[h]
            v16h = v16[:, cols]
            diag = jnp.sum(qk[:, cols], axis=1, keepdims=True)
            out = (_dot(attn.astype(BF16), v16h) + diag * v16h.astype(F32)
                   + _dot_nt(qd[HG_LEVELS][:, cols], state_t.astype(BF16)))
            kv_t = lax.dot_general(v16h, kk[:, cols], (((0,), (0,)), ((), ())),
                                   preferred_element_type=F32)
            state_ref[h] = state_t * e_b[c - 1:c, cols] + kv_t
            y = out * lax.rsqrt(jnp.mean(out * out, axis=-1, keepdims=True) + NORM_EPS) * g
            o_ref[rows, cols] = (y * gate[:, cols]).astype(o_ref.dtype)
        return carry

    lax.fori_loop(0, ts // c, chunk, 0)


def _hgrn2(proj, side, lb, norm_g, batch, seq, col0, ts=512):
    m = proj.shape[0]
    c0 = col0 // HG_W
    spb = seq // ts
    mat, masks = _hg_tables()
    col = lambda off: pl.BlockSpec((ts, HG_W), lambda b, s: (b * spb + s, c0 + off))
    return pl.pallas_call(
        functools.partial(_hg_kernel, ts=ts),
        grid=(batch, spb),
        in_specs=[col(0), pl.BlockSpec((ts, HG_W), lambda b, s: (b * spb + s, 0)), col(1), col(2),
                  pl.BlockSpec((1, HG_W), lambda b, s: (0, 0)),
                  pl.BlockSpec((1, LANES), lambda b, s: (0, 0)),
                  pl.BlockSpec(mat.shape, lambda b, s: (0, 0)),
                  pl.BlockSpec(masks.shape, lambda b, s: (0, 0, 0))],
        out_specs=pl.BlockSpec((ts, HG_W), lambda b, s: (b * spb + s, 0)),
        out_shape=jax.ShapeDtypeStruct((m, HG_W), BF16),
        scratch_shapes=[pltpu.VMEM((HG_HEADS, HG_VAL_DIM, HG_KEY_DIM), F32)],
        compiler_params=_cparams(("arbitrary", "arbitrary")),
        name="hgrn2",
    )(proj, side, proj, proj, lb.reshape(1, HG_W), norm_g.reshape(1, HG_VAL_DIM),
      jnp.asarray(mat, BF16), jnp.asarray(masks))


def _lb_kernel(x_ref, o_ref):
    x = x_ref[...]
    e = jnp.exp(x - jnp.max(x, axis=0, keepdims=True))
    p = e / jnp.sum(e, axis=0, keepdims=True)
    depth = x.shape[0]
    run = jnp.zeros_like(p[0:1])
    for l in range(depth):
        run = run + p[l:l + 1]
        o_ref[l:l + 1, :] = run - p[0:1]


def _lower_bounds(hg_lower_bounds):
    return pl.pallas_call(
        _lb_kernel,
        out_shape=jax.ShapeDtypeStruct(hg_lower_bounds.shape, F32),
        name="hg_lower_bounds",
    )(hg_lower_bounds.astype(F32))


_COL_Q = 0
_COL_CONV = ATT_Q_W + 2 * ATT_KV_W + IDX_Q_W
_COL_HG = _COL_CONV + 3 * CONV_WIDTH
_COL_GATE = _COL_HG + 3 * HG_W
_SIDE_SMALL = HG_W
_SIDE_W = HG_W + LANES


def _split_w_in(w_in, d_model):
    sizes = (ATT_Q_W, ATT_KV_W, ATT_KV_W, IDX_Q_W, IDX_HEAD_DIM, IDX_HEADS,
             CONV_WIDTH, CONV_WIDTH, CONV_WIDTH, HG_W, HG_W, HG_W, HG_W, N_BRANCHES * d_model)
    offs = np.cumsum((0,) + sizes)
    part = lambda n: w_in[..., offs[n]:offs[n + 1]]
    main = jnp.concatenate([part(n) for n in (0, 1, 2, 3, 6, 7, 8, 9, 11, 12, 13)], axis=-1).astype(BF16)
    side = jnp.concatenate([part(10), part(4), part(5)], axis=-1)
    side = jnp.pad(side, ((0, 0), (0, 0), (0, _SIDE_W - side.shape[-1]))).astype(BF16)
    return main, side


def kernel(x, c, positions, ada_w, ada_b, norm_mix_g, w_in, conv_w, hg_lower_bounds, hg_norm_g,
           w_o_attn, w_o_conv, w_o_hgrn, w_out, norm_mlp_g, w_mlp1, w_mlp2, final_norm_g):
    batch, seq, d = x.shape
    depth = ada_w.shape[0]
    m = batch * seq
    topk = min(INDEX_TOPK, seq // 4)
    lb_all = _lower_bounds(hg_lower_bounds)
    mod = _modulation(c, ada_w, ada_b)
    tables = _rope_tables(positions)
    w_main, w_side = _split_w_in(w_in, d)
    wa, wc, wh, wo = (w.astype(BF16) for w in (w_o_attn, w_o_conv, w_o_hgrn, w_out))
    w1, w2 = w_mlp1.astype(BF16), w_mlp2.astype(BF16)
    x2 = x.reshape(m, d)
    mods = [[mod[l, :, n * d:(n + 1) * d] for n in range(ADA_CHUNKS)] for l in range(depth)]
    h = _norm(x2, norm_mix_g[0], seq, mods[0][0], mods[0][1])
    for l in range(depth):
        sh1, sc1, g1, sh2, sc2, g2 = mods[l]
        proj = _matmul(h, w_main, out_dtype=BF16, tn=1024, layer=l)
        side = _matmul(h, w_side, out_dtype=F32, tn=_SIDE_W, layer=l)
        q, k, vt, qi, ki, wit = _rope_split(proj, side, tables)
        att = _dsa_attention(q, k, vt, qi, ki, wit, batch, seq, topk)
        hg = _hgrn2(proj, side, lb_all[l], hg_norm_g[l], batch, seq, _COL_HG)
        x2, h = _merge_out(att, hg, wa, wc, wh, proj, _COL_CONV, conv_w[l], _COL_GATE, wo, x2, g1,
                           norm_mlp_g[l], sh2, sc2, seq, l)
        a = _matmul(h, w1, out_dtype=BF16, relu2=True, tm=256, tn=4 * d, layer=l)
        if l + 1 < depth:
            x2, h = _mlp_down(a, w2, x2, g2, norm_mix_g[l + 1], mods[l + 1][0], mods[l + 1][1], seq, l,
                              emit_x=True, h_dtype=BF16)
        else:
            zero = jnp.zeros_like(g2)
            out = _mlp_down(a, w2, x2, g2, final_norm_g, zero, zero, seq, l, emit_x=False, h_dtype=x.dtype)
    return out.reshape(batch, seq, d)
```

```python
import functools

import numpy as np
import jax
import jax.numpy as jnp
from jax import lax
from jax.experimental import pallas as pl
from jax.experimental.pallas import tpu as pltpu

ATT_HEADS = 8
ATT_KV_HEADS = 2
ATT_HEAD_DIM = 128
IDX_HEADS = 8
IDX_HEAD_DIM = 64
INDEX_TOPK = 256
ROPE_THETA = 500000.0
ROPE_FRACTION = 4
MASK_VALUE = -1e30
CONV_WIDTH = 1024
CONV_K = 3
HG_HEADS = 8
HG_KEY_DIM = 128
HG_VAL_DIM = 128
F_MIN = 1e-30
N_BRANCHES = 3
NORM_EPS = 1e-6
ADA_CHUNKS = 6

ATT_Q_W = ATT_HEADS * ATT_HEAD_DIM
ATT_KV_W = ATT_KV_HEADS * ATT_HEAD_DIM
IDX_Q_W = IDX_HEADS * IDX_HEAD_DIM
HG_W = HG_HEADS * HG_KEY_DIM
ATT_LOGIT_SCALE = (ATT_HEAD_DIM ** -0.5) * float(np.log2(np.e))

LANES = 128
SUBLANES = 8
HALO_ROWS = 16
VMEM_LIMIT_BYTES = 56 * 1024 * 1024

INT_MIN = -2 ** 31
SEARCH_VARIANTS = 8
HG_CHUNK = 64
HG_LEVELS = 6
HG_SPLIT = 3

BF16 = jnp.bfloat16
F32 = jnp.float32


def _cparams(sem):
    return pltpu.CompilerParams(dimension_semantics=sem, vmem_limit_bytes=VMEM_LIMIT_BYTES)


def _dot(a, b):
    return jnp.dot(a, b, preferred_element_type=F32)


def _dot_nt(a, b):
    return lax.dot_general(a, b, (((1,), (1,)), ((), ())), preferred_element_type=F32)


def _sigmoid(x):
    return 1.0 / (1.0 + jnp.exp(-x))


def _mod_kernel(c_ref, w_ref, b_ref, o_ref):
    c = c_ref[...]
    ca = (c * _sigmoid(c)).astype(BF16)
    o_ref[...] = _dot(ca, w_ref[...].astype(BF16)) + b_ref[...]


def _modulation(c, ada_w, ada_b):
    depth, d, n = ada_w.shape
    b = c.shape[0]
    bp = -(-b // SUBLANES) * SUBLANES
    cp = jnp.pad(c, ((0, bp - b), (0, 0)))
    tn = 1024
    out = pl.pallas_call(
        _mod_kernel,
        grid=(depth, n // tn),
        in_specs=[
            pl.BlockSpec((bp, d), lambda l, j: (0, 0)),
            pl.BlockSpec((None, d, tn), lambda l, j: (l, 0, j)),
            pl.BlockSpec((None, 1, tn), lambda l, j: (l, 0, j)),
        ],
        out_specs=pl.BlockSpec((None, bp, tn), lambda l, j: (l, 0, j)),
        out_shape=jax.ShapeDtypeStruct((depth, bp, n), F32),
        compiler_params=_cparams(("arbitrary", "arbitrary")),
        name="adaln_mod",
    )(cp, ada_w, ada_b.reshape(depth, 1, n))
    return out[:, :b]


def _norm_kernel(x_ref, g_ref, sh_ref, sc_ref, o_ref):
    x = x_ref[...]
    y = x * lax.rsqrt(jnp.mean(x * x, axis=-1, keepdims=True) + NORM_EPS) * g_ref[...]
    o_ref[...] = (y * (1.0 + sc_ref[0]) + sh_ref[0]).astype(o_ref.dtype)


def _norm(x2, g, seq, shift, scale, out_dtype=BF16, tm=512):
    m, d = x2.shape
    tpb = seq // tm
    row = pl.BlockSpec((1, 1, d), lambda i: (i // tpb, 0, 0))
    return pl.pallas_call(
        _norm_kernel,
        grid=(m // tm,),
        in_specs=[pl.BlockSpec((tm, d), lambda i: (i, 0)), pl.BlockSpec((1, d), lambda i: (0, 0)), row, row],
        out_specs=pl.BlockSpec((tm, d), lambda i: (i, 0)),
        out_shape=jax.ShapeDtypeStruct((m, d), out_dtype),
        compiler_params=_cparams(("arbitrary",)),
        name="rmsnorm",
    )(x2, g.reshape(1, d), shift[:, None, :], scale[:, None, :])


def _conv_tile(cb_ref, cc_ref, cx_ref, hc_ref, hx_ref, w_ref, at_seq_start):
    u = cc_ref[...].astype(F32) * cx_ref[...].astype(F32)
    halo = hc_ref[...].astype(F32) * hx_ref[...].astype(F32)
    halo = jnp.where(at_seq_start, 0.0, halo)
    row = lax.broadcasted_iota(jnp.int32, u.shape, 0)
    h1 = halo[HALO_ROWS - 1:HALO_ROWS, :]
    h2 = halo[HALO_ROWS - 2:HALO_ROWS - 1, :]
    u1 = jnp.where(row == 0, h1, pltpu.roll(u, 1, axis=0))
    u2 = jnp.where(row == 0, h2, jnp.where(row == 1, h1, pltpu.roll(u, 2, axis=0)))
    w = w_ref[...]
    conv = u2 * w[0:1, :] + u1 * w[1:2, :] + u * w[2:3, :]
    return cb_ref[...].astype(F32) * conv


def _merge_kernel(att_ref, cb_ref, cc_ref, cx_ref, hc_ref, hx_ref, cw_ref, hg_ref, wa_ref, wc_ref, wh_ref,
                  ga_ref, gc_ref, gh_ref, wo_ref, x_ref, gr_ref, ng_ref, sh_ref, sc_ref, o_ref, h_ref,
                  *, tm, seq):
    at_start = (pl.program_id(0) * tm) % seq == 0
    cv = _conv_tile(cb_ref, cc_ref, cx_ref, hc_ref, hx_ref, cw_ref, at_start).astype(BF16)
    merged = (_sigmoid(ga_ref[...].astype(F32)) * _dot(att_ref[...], wa_ref[...])
              + _sigmoid(gc_ref[...].astype(F32)) * _dot(cv, wc_ref[...])
              + _sigmoid(gh_ref[...].astype(F32)) * _dot(hg_ref[...], wh_ref[...]))
    y = x_ref[...] + gr_ref[0] * _dot(merged.astype(BF16), wo_ref[...])
    o_ref[...] = y
    yn = y * lax.rsqrt(jnp.mean(y * y, axis=-1, keepdims=True) + NORM_EPS) * ng_ref[...]
    h_ref[...] = (yn * (1.0 + sc_ref[0]) + sh_ref[0]).astype(h_ref.dtype)


def _merge_out(att, hg, wa, wc, wh, proj, conv_off, conv_w, gate_off, wout, x2, gate_row, norm_g, shift,
               scale, seq, layer, tm=256):
    m, d = x2.shape
    tpb = seq // tm
    g0 = gate_off // d
    c0 = conv_off // CONV_WIDTH
    rpb = tm // HALO_ROWS
    cmain = lambda off: pl.BlockSpec((tm, CONV_WIDTH), lambda i: (i, c0 + off))
    chalo = lambda off: pl.BlockSpec((HALO_ROWS, CONV_WIDTH), lambda i: (jnp.maximum(i * rpb - 1, 0), c0 + off))
    act = lambda a: pl.BlockSpec((tm, a.shape[1]), lambda i: (i, 0))
    resident = lambda w: pl.BlockSpec((None,) + w.shape[1:], lambda i: (layer, 0, 0),
                                      pipeline_mode=pl.Buffered(1))
    gate = lambda b: pl.BlockSpec((tm, d), lambda i: (i, g0 + b))
    row = pl.BlockSpec((1, 1, d), lambda i: (i // tpb, 0, 0))
    full = pl.BlockSpec((tm, d), lambda i: (i, 0))
    return pl.pallas_call(
        functools.partial(_merge_kernel, tm=tm, seq=seq),
        grid=(m // tm,),
        in_specs=[act(att), cmain(0), cmain(1), cmain(2), chalo(1), chalo(2),
                  pl.BlockSpec((CONV_K, CONV_WIDTH), lambda i: (0, 0)),
                  act(hg), resident(wa), resident(wc), resident(wh),
                  gate(0), gate(1), gate(2), resident(wout), full, row,
                  pl.BlockSpec((1, d), lambda i: (0, 0)), row, row],
        out_specs=[full, full],
        out_shape=[jax.ShapeDtypeStruct((m, d), F32), jax.ShapeDtypeStruct((m, d), BF16)],
        compiler_params=_cparams(("arbitrary",)),
        name="merge_out",
    )(att, proj, proj, proj, proj, proj, conv_w, hg, wa, wc, wh, proj, proj, proj, wout, x2,
      gate_row[:, None, :], norm_g.reshape(1, d), shift[:, None, :], scale[:, None, :])


def _mm_kernel(a_ref, w_ref, o_ref, *, relu2):
    y = _dot(a_ref[...], w_ref[...])
    if relu2:
        y = jnp.square(jnp.maximum(y, 0.0))
    o_ref[...] = y.astype(o_ref.dtype)


def _matmul(a, w, *, out_dtype, relu2=False, tm=1024, tn=512, layer=0):
    m = a.shape[0]
    kdim, n = w.shape[-2:]
    tn = min(tn, n)
    w_mode = pl.Buffered(1) if tn == n else None
    return pl.pallas_call(
        functools.partial(_mm_kernel, relu2=relu2),
        grid=(m // tm, n // tn),
        in_specs=[pl.BlockSpec((tm, kdim), lambda i, j: (i, 0)),
                  pl.BlockSpec((None, kdim, tn), lambda i, j: (layer, 0, j), pipeline_mode=w_mode)],
        out_specs=pl.BlockSpec((tm, tn), lambda i, j: (i, j)),
        out_shape=jax.ShapeDtypeStruct((m, n), out_dtype),
        compiler_params=_cparams(("arbitrary", "arbitrary")),
        name="matmul_relu2" if relu2 else "matmul",
    )(a, w)


def _mlp_down_kernel(a_ref, w_ref, x_ref, gr_ref, ng_ref, sh_ref, sc_ref, *outs, emit_x):
    y = x_ref[...] + gr_ref[0] * _dot(a_ref[...], w_ref[...])
    if emit_x:
        outs[0][...] = y
    yn = y * lax.rsqrt(jnp.mean(y * y, axis=-1, keepdims=True) + NORM_EPS) * ng_ref[...]
    outs[-1][...] = (yn * (1.0 + sc_ref[0]) + sh_ref[0]).astype(outs[-1].dtype)


def _mlp_down(a, w, x2, gate_row, norm_g, shift, scale, seq, layer, *, emit_x, h_dtype, tm=256):
    m, d = x2.shape
    kdim = w.shape[1]
    tpb = seq // tm
    row = pl.BlockSpec((1, 1, d), lambda i: (i // tpb, 0, 0))
    full = pl.BlockSpec((tm, d), lambda i: (i, 0))
    out_specs, out_shape = [full], [jax.ShapeDtypeStruct((m, d), h_dtype)]
    if emit_x:
        out_specs, out_shape = [full] + out_specs, [jax.ShapeDtypeStruct((m, d), F32)] + out_shape
    outs = pl.pallas_call(
        functools.partial(_mlp_down_kernel, emit_x=emit_x),
        grid=(m // tm,),
        in_specs=[pl.BlockSpec((tm, kdim), lambda i: (i, 0)),
                  pl.BlockSpec((None, kdim, d), lambda i: (layer, 0, 0), pipeline_mode=pl.Buffered(1)),
                  full, row, pl.BlockSpec((1, d), lambda i: (0, 0)), row, row],
        out_specs=out_specs,
        out_shape=out_shape,
        compiler_params=_cparams(("arbitrary",)),
        name="mlp_down",
    )(a, w, x2, gate_row[:, None, :], norm_g.reshape(1, d), shift[:, None, :], scale[:, None, :])
    return outs if emit_x else outs[0]


def _rope_rows(period, half, rot):
    j = np.arange(LANES) % period
    inv = ROPE_THETA ** (-(np.arange(half, dtype=np.float32)) / np.float32(half))
    inv_row = np.where(j < rot, inv.astype(np.float32)[j % half], 0.0).astype(np.float32)
    sign_row = np.where(j < half, -1.0, np.where(j < rot, 1.0, 0.0)).astype(np.float32)
    first_row = (j < half).astype(np.float32)
    return np.stack([inv_row, sign_row, first_row])[:, None, :]


def _rope_apply(x, cos_t, sin_s, first, half):
    up = pltpu.roll(x, LANES - half, axis=1)
    dn = pltpu.roll(x, half, axis=1)
    sw = jnp.where(first > 0.5, up, dn)
    return x * cos_t + sw * sin_s


def _rope_table_kernel(pos_ref, rows_a_ref, rows_i_ref, ca_ref, sa_ref, ci_ref, si_ref):
    pos = pos_ref[...].astype(F32)
    ang_a = pos * rows_a_ref[0]
    ca_ref[...] = jnp.cos(ang_a)
    sa_ref[...] = jnp.sin(ang_a) * rows_a_ref[1]
    ang_i = pos * rows_i_ref[0]
    ci_ref[...] = jnp.cos(ang_i)
    si_ref[...] = jnp.sin(ang_i) * rows_i_ref[1]


def _rope_consts():
    rows_a = jnp.asarray(_rope_rows(ATT_HEAD_DIM, ATT_HEAD_DIM // ROPE_FRACTION // 2,
                                    ATT_HEAD_DIM // ROPE_FRACTION))
    rows_i = jnp.asarray(_rope_rows(IDX_HEAD_DIM, IDX_HEAD_DIM // ROPE_FRACTION // 2,
                                    IDX_HEAD_DIM // ROPE_FRACTION))
    return rows_a, rows_i


def _rope_tables(positions, tm=512):
    m = positions.size
    rows_a, rows_i = _rope_consts()
    rows_spec = pl.BlockSpec((3, 1, LANES), lambda i: (0, 0, 0))
    tab = pl.BlockSpec((tm, LANES), lambda i: (i, 0))
    return pl.pallas_call(
        _rope_table_kernel,
        grid=(m // tm,),
        in_specs=[pl.BlockSpec((tm, 1), lambda i: (i, 0)), rows_spec, rows_spec],
        out_specs=[tab] * 4,
        out_shape=[jax.ShapeDtypeStruct((m, LANES), F32)] * 4,
        compiler_params=_cparams(("arbitrary",)),
        name="rope_tables",
    )(positions.reshape(m, 1), rows_a, rows_i)


def _rope_kernel(ca_ref, sa_ref, ci_ref, si_ref, rows_a_ref, rows_i_ref, q_ref, kv_ref, qi_ref, sm_ref,
                 qo_ref, ko_ref, vo_ref, qio_ref, kio_ref, wio_ref, *, w_scale):
    cos_a, sin_a, first_a = ca_ref[...], sa_ref[...], rows_a_ref[2]
    cos_i, sin_i, first_i = ci_ref[...], si_ref[...], rows_i_ref[2]
    half_a = ATT_HEAD_DIM // ROPE_FRACTION // 2
    half_i = IDX_HEAD_DIM // ROPE_FRACTION // 2
    for h in range(ATT_HEADS):
        sl = slice(h * LANES, (h + 1) * LANES)
        qo_ref[:, sl] = (_rope_apply(q_ref[:, sl].astype(F32), cos_a, sin_a, first_a, half_a)
                         * ATT_LOGIT_SCALE).astype(BF16)
    for h in range(ATT_KV_HEADS):
        sl = slice(h * LANES, (h + 1) * LANES)
        ko_ref[:, sl] = _rope_apply(kv_ref[:, sl].astype(F32), cos_a, sin_a, first_a, half_a).astype(BF16)
    vo_ref[...] = kv_ref[:, ATT_KV_W:].astype(F32).T.astype(BF16)
    for h in range(IDX_Q_W // LANES):
        sl = slice(h * LANES, (h + 1) * LANES)
        qio_ref[:, sl] = _rope_apply(qi_ref[:, sl].astype(F32), cos_i, sin_i, first_i, half_i).astype(BF16)
    sm = sm_ref[...]
    ki = _rope_apply(sm, cos_i, sin_i, first_i, half_i)
    kio_ref[...] = ki[:, :IDX_HEAD_DIM].astype(BF16)
    wio_ref[...] = sm.T[IDX_HEAD_DIM:IDX_HEAD_DIM + IDX_HEADS, :] * w_scale


def _rope_split(proj, side, tables, tm=512):
    m = proj.shape[0]
    rows_a, rows_i = _rope_consts()
    w_scale = (IDX_HEADS ** -0.5) * (IDX_HEAD_DIM ** -0.5)
    rows_spec = pl.BlockSpec((3, 1, LANES), lambda i: (0, 0, 0))
    tab = pl.BlockSpec((tm, LANES), lambda i: (i, 0))
    outs = pl.pallas_call(
        functools.partial(_rope_kernel, w_scale=w_scale),
        grid=(m // tm,),
        in_specs=[
            tab, tab, tab, tab,
            rows_spec, rows_spec,
            pl.BlockSpec((tm, ATT_Q_W), lambda i: (i, 0)),
            pl.BlockSpec((tm, 2 * ATT_KV_W), lambda i: (i, ATT_Q_W // (2 * ATT_KV_W))),
            pl.BlockSpec((tm, IDX_Q_W), lambda i: (i, (ATT_Q_W + 2 * ATT_KV_W) // IDX_Q_W)),
            pl.BlockSpec((tm, LANES), lambda i: (i, _SIDE_SMALL // LANES)),
        ],
        out_specs=[
            pl.BlockSpec((tm, ATT_Q_W), lambda i: (i, 0)),
            pl.BlockSpec((tm, ATT_KV_W), lambda i: (i, 0)),
            pl.BlockSpec((ATT_KV_W, tm), lambda i: (0, i)),
            pl.BlockSpec((tm, IDX_Q_W), lambda i: (i, 0)),
            pl.BlockSpec((tm, IDX_HEAD_DIM), lambda i: (i, 0)),
            pl.BlockSpec((IDX_HEADS, tm), lambda i: (0, i)),
        ],
        out_shape=[
            jax.ShapeDtypeStruct((m, ATT_Q_W), BF16),
            jax.ShapeDtypeStruct((m, ATT_KV_W), BF16),
            jax.ShapeDtypeStruct((ATT_KV_W, m), BF16),
            jax.ShapeDtypeStruct((m, IDX_Q_W), BF16),
            jax.ShapeDtypeStruct((m, IDX_HEAD_DIM), BF16),
            jax.ShapeDtypeStruct((IDX_HEADS, m), F32),
        ],
        compiler_params=_cparams(("arbitrary",)),
        name="rope_split",
    )(*tables, rows_a, rows_i, proj, proj, proj, side)
    return outs


def _attn_kernel(qi_ref, wi_ref, ki_ref, q_ref, k_ref, vt_ref, o_ref, skey_ref,
                 da_ref, db_ref, sa_ref, sb_ref, *, tq, kc, topk, seq):
    i = pl.program_id(1)
    nchunk = lax.shift_right_logical((i + 1) * tq + (kc - 1), int(np.log2(kc)))
    qpos = i * tq + lax.broadcasted_iota(jnp.int32, (1, tq), 1)
    rb = 8 * SUBLANES

    qi = qi_ref[0]
    wi = wi_ref[...]
    qi_s = jnp.concatenate([qi[:, h * IDX_HEAD_DIM:(h + 1) * IDX_HEAD_DIM] for h in range(IDX_HEADS)],
                           axis=0)

    kh = kc // 2
    last_half = seq // kh - 1

    def score_dots(half, dst_ref):
        k0 = pl.multiple_of(jnp.minimum(half, last_half) * kh, kh)
        dst_ref[...] = _dot_nt(ki_ref[0, pl.ds(k0, kh), :], qi_s)

    def score_keys(half, src_ref):
        k0 = pl.multiple_of(half * kh, kh)
        for j in range(kh // rb):
            rows = slice(j * rb, (j + 1) * rb)
            acc = jnp.zeros((rb, tq), F32)
            for h in range(IDX_HEADS):
                acc = acc + jnp.maximum(src_ref[rows, h * tq:(h + 1) * tq], 0.0) * wi[h:h + 1, :]
            kpos = k0 + j * rb + lax.broadcasted_iota(jnp.int32, (rb, tq), 0)
            bits = pltpu.bitcast(acc, jnp.int32)
            skey = jnp.where(bits < 0, bits ^ jnp.int32(0x7FFFFFFF), bits)
            skey = jnp.where(kpos <= qpos, skey, jnp.int32(INT_MIN))
            skey_ref[pl.ds(k0 + j * rb, rb), :] = skey

    def score_chunk(c, carry):
        score_dots(2 * c + 1, db_ref)
        score_keys(2 * c, da_ref)
        score_dots(2 * c + 2, da_ref)
        score_keys(2 * c + 1, db_ref)
        return carry

    score_dots(0, da_ref)
    lax.fori_loop(0, nchunk, score_chunk, 0)

    nacc = 4

    def count_where(pred_fn):
        row_iota = lax.broadcasted_iota(jnp.int32, (SUBLANES, tq), 0)

        def chunk_body(c, parts):
            k0 = pl.multiple_of(c * kc, kc)
            parts = list(parts)
            xs = skey_ref[pl.ds(k0, kc), :]
            for j in range(kc // SUBLANES):
                x = xs[j * SUBLANES:(j + 1) * SUBLANES]
                parts[j % nacc] = parts[j % nacc] + jnp.where(pred_fn(x, k0 + j * SUBLANES + row_iota), 1, 0)
            return tuple(parts)
        zero = jnp.zeros((SUBLANES, tq), jnp.int32)
        parts = lax.fori_loop(0, nchunk, chunk_body, (zero,) * nacc)
        return jnp.sum(sum(parts[1:], parts[0]), axis=0, keepdims=True)

    nfull = seq // kc
    step = max(nfull // SEARCH_VARIANTS, 1)
    variant = lax.div(nchunk + (step - 1), step) - 1

    def fill_chunk(c, carry):
        rows = pl.ds(pl.multiple_of(c * kc, kc), kc)
        skey_ref[rows, :] = jnp.full((kc, tq), INT_MIN, jnp.int32)
        return carry

    lax.fori_loop(nchunk, (variant + 1) * step, fill_chunk, 0)

    def search_all(nc):
        def bit_body(b, prefix):
            cand = prefix | lax.shift_left(jnp.int32(1), 31 - b)
            cand_s = jnp.broadcast_to(cand ^ jnp.int32(INT_MIN), (SUBLANES, tq))
            parts = [jnp.zeros((SUBLANES, tq), jnp.int32)] * nacc
            for j in range(nc * kc // SUBLANES):
                x = skey_ref[j * SUBLANES:(j + 1) * SUBLANES, :]
                parts[j % nacc] = parts[j % nacc] + jnp.where(x >= cand_s, 1, 0)
            cnt = jnp.sum(sum(parts[1:], parts[0]), axis=0, keepdims=True)
            return jnp.where(cnt >= topk, cand, prefix)
        return lax.fori_loop(0, 32, bit_body, jnp.zeros((1, tq), jnp.int32))

    prefix = lax.switch(variant, [functools.partial(search_all, nc) for nc in range(step, nfull + 1, step)])
    thr = prefix ^ jnp.int32(INT_MIN)

    n_gt = count_where(lambda x, _: x > thr)
    n_eq = count_where(lambda x, _: x == thr)
    need = topk - n_gt
    excess = jnp.where((prefix != 0) & (n_eq > need), 1, 0)
    any_excess = jnp.max(excess) > 0
    idx_bits = int(np.log2(seq))

    def tie_search():
        def tie_bit(b, x):
            cand = x | lax.shift_left(jnp.int32(1), idx_bits - 1 - b)
            below = count_where(lambda xk, kidx: (xk == thr) & (kidx < cand))
            return jnp.where(below < need, cand, x)
        return lax.fori_loop(0, idx_bits, tie_bit, jnp.zeros((1, tq), jnp.int32))

    tie_last = lax.cond(any_excess, tie_search, lambda: jnp.full((1, tq), seq, jnp.int32))

    groups = ATT_KV_HEADS
    rep = ATT_HEADS // ATT_KV_HEADS
    q = q_ref[0]
    q_g = [jnp.concatenate([q[:, (g * rep + r) * LANES:(g * rep + r + 1) * LANES] for r in range(rep)], axis=0)
           for g in range(groups)]

    def qk_dots(half, dst_ref):
        k0 = pl.multiple_of(jnp.minimum(half, last_half) * kh, kh)
        for g in range(groups):
            dst_ref[g] = _dot_nt(k_ref[0, pl.ds(k0, kh), g * LANES:(g + 1) * LANES], q_g[g])

    def softmax_pv(half, src_ref, carry):
        k0 = pl.multiple_of(half * kh, kh)
        x = skey_ref[pl.ds(k0, kh), :]
        kidx = k0 + lax.broadcasted_iota(jnp.int32, (kh, tq), 0)
        sel = (x > thr) | ((x == thr) & (kidx <= tie_last))
        sel = sel & (x != jnp.int32(INT_MIN))
        bias = jnp.where(sel, 0.0, MASK_VALUE)
        bias = jnp.concatenate([bias] * rep, axis=1)
        new = []
        for g in range(groups):
            m_old, l_old, acc_old = carry[g]
            vtg = vt_ref[g * LANES:(g + 1) * LANES, pl.ds(k0, kh)]
            sm = src_ref[g] + bias
            m_new = jnp.maximum(m_old, jnp.max(sm, axis=0, keepdims=True))
            p = jnp.exp2(sm - m_new)
            alpha = jnp.exp2(m_old - m_new)
            l_new = alpha * l_old + jnp.sum(p, axis=0, keepdims=True)
            acc_new = alpha * acc_old + _dot(vtg, p.astype(BF16))
            new.append((m_new, l_new, acc_new))
        return tuple(new)

    def attn_chunk(c, carry):
        qk_dots(2 * c + 1, sb_ref)
        carry = softmax_pv(2 * c, sa_ref, carry)
        qk_dots(2 * c + 2, sa_ref)
        return softmax_pv(2 * c + 1, sb_ref, carry)

    init = tuple((jnp.full((1, rep * tq), MASK_VALUE, F32), jnp.zeros((1, rep * tq), F32),
                  jnp.zeros((LANES, rep * tq), F32)) for _ in range(groups))
    qk_dots(0, sa_ref)
    fin = lax.fori_loop(0, nchunk, attn_chunk, init)
    for g in range(groups):
        _, l_f, acc_f = fin[g]
        o_t = acc_f / l_f
        for r in range(rep):
            h = g * rep + r
            o_ref[0, :, h * LANES:(h + 1) * LANES] = o_t[:, r * tq:(r + 1) * tq].T.astype(o_ref.dtype)


def _dsa_attention(q, k, vt, qi, ki, wit, batch, seq, topk, tq=128, kc=512):
    nblk = seq // tq
    rep = ATT_HEADS // ATT_KV_HEADS
    r3 = lambda a: a.reshape(batch, seq, a.shape[-1])
    qblk = lambda w: pl.BlockSpec((1, tq, w), lambda b, i: (b, i, 0))
    full = lambda w: pl.BlockSpec((1, seq, w), lambda b, i: (b, 0, 0))
    out = pl.pallas_call(
        functools.partial(_attn_kernel, tq=tq, kc=kc, topk=topk, seq=seq),
        grid=(batch, nblk),
        in_specs=[qblk(IDX_Q_W), pl.BlockSpec((IDX_HEADS, tq), lambda b, i: (0, b * nblk + i)),
                  full(IDX_HEAD_DIM), qblk(ATT_Q_W), full(ATT_KV_W),
                  pl.BlockSpec((ATT_KV_W, seq), lambda b, i: (0, b))],
        out_specs=qblk(ATT_Q_W),
        out_shape=jax.ShapeDtypeStruct((batch, seq, ATT_Q_W), BF16),
        scratch_shapes=[pltpu.VMEM((seq, tq), jnp.int32),
                        pltpu.VMEM((kc // 2, IDX_HEADS * tq), F32), pltpu.VMEM((kc // 2, IDX_HEADS * tq), F32),
                        pltpu.VMEM((ATT_KV_HEADS, kc // 2, rep * tq), F32),
                        pltpu.VMEM((ATT_KV_HEADS, kc // 2, rep * tq), F32)],
        compiler_params=_cparams(("arbitrary", "arbitrary")),
        name="dsa_attention",
    )(r3(qi), wit, r3(ki), r3(q), r3(k), vt)
    return out.reshape(batch * seq, ATT_Q_W)


def _hg_tables():
    c = HG_CHUNK
    t = np.arange(c)
    mats, masks = [], []
    for l in range(HG_LEVELS):
        h = 1 << l
        start = (t // (2 * h)) * (2 * h)
        p = start + h - 1
        right = (t - start) >= h
        u = t[None, :]
        r_m = right[:, None] & (u > p[:, None]) & (u <= t[:, None])
        l_m = (~right)[:, None] & (u > t[:, None]) & (u <= p[:, None])
        mats.append(r_m | l_m)
        same = (t[:, None] // (2 * h)) == (t[None, :] // (2 * h))
        masks.append(same & right[:, None] & (~right)[None, :])
    mats.append(t[None, :] <= t[:, None])
    mats.append(t[None, :] > t[:, None])
    mat = np.concatenate(mats, 0).astype(np.float32)
    return np.concatenate([mat] * HG_SPLIT, 1), np.stack(masks).astype(np.float32)


def _split3(x):
    hi = x.astype(BF16)
    r1 = x - hi.astype(F32)
    mid = r1.astype(BF16)
    lo = (r1 - mid.astype(F32)).astype(BF16)
    return jnp.concatenate([hi, mid, lo], axis=0)


def _hg_kernel(hq_ref, hf_ref, hi_ref, hg_ref, lb_ref, g_ref, mat_ref, mask_ref, o_ref, state_ref, *, ts):
    c = HG_CHUNK

    @pl.when(pl.program_id(1) == 0)
    def _():
        state_ref[...] = jnp.zeros_like(state_ref)

    g = g_ref[...]

    def chunk(ci, carry):
        rows = pl.ds(pl.multiple_of(ci * c, c), c)
        lb = lb_ref[...]
        f = lb + (1.0 - lb) * _sigmoid(hf_ref[rows, :])
        logf = jnp.log(jnp.maximum(f, F_MIN))
        kin = 1.0 - f
        q = hq_ref[rows, :].astype(F32)
        v16 = hi_ref[rows, :]
        e = jnp.exp(_dot(mat_ref[...], _split3(logf)))
        qd = [(q * e[l * c:(l + 1) * c]).astype(BF16) for l in range(HG_LEVELS + 1)]
        kd = [(kin * e[l * c:(l + 1) * c]).astype(BF16) for l in range(HG_LEVELS)]
        e_b = e[HG_LEVELS * c:(HG_LEVELS + 1) * c]
        kk = (kin * e[(HG_LEVELS + 1) * c:(HG_LEVELS + 2) * c]).astype(BF16)
        qk = q * kin
        gate = hg_ref[rows, :].astype(F32)
        gate = gate * _sigmoid(gate)
        for h in range(HG_HEADS):
            cols = slice(h * LANES, (h + 1) * LANES)
            attn = jnp.zeros((c, c), F32)
            for l in range(HG_LEVELS):
                attn = attn + _dot_nt(qd[l][:, cols], kd[l][:, cols]) * mask_ref[l]
            state_t = state_ref[h]
            v16h = v16[:, cols]
            diag = jnp.sum(qk[:, cols], axis=1, keepdims=True)
            out = (_dot(attn.astype(BF16), v16h) + diag * v16h.astype(F32)
                   + _dot_nt(qd[HG_LEVELS][:, cols], state_t.astype(BF16)))
            kv_t = lax.dot_general(v16h, kk[:, cols], (((0,), (0,)), ((), ())),
                                   preferred_element_type=F32)
            state_ref[h] = state_t * e_b[c - 1:c, cols] + kv_t
            y = out * lax.rsqrt(jnp.mean(out * out, axis=-1, keepdims=True) + NORM_EPS) * g
            o_ref[rows, cols] = (y * gate[:, cols]).astype(o_ref.dtype)
        return carry

    lax.fori_loop(0, ts // c, chunk, 0)


def _hgrn2(proj, side, lb, norm_g, batch, seq, col0, ts=512):
    m = proj.shape[0]
    c0 = col0 // HG_W
    spb = seq // ts
    mat, masks = _hg_tables()
    col = lambda off: pl.BlockSpec((ts, HG_W), lambda b, s: (b * spb + s, c0 + off))
    return pl.pallas_call(
        functools.partial(_hg_kernel, ts=ts),
        grid=(batch, spb),
        in_specs=[col(0), pl.BlockSpec((ts, HG_W), lambda b, s: (b * spb + s, 0)), col(1), col(2),
                  pl.BlockSpec((1, HG_W), lambda b, s: (0, 0)),
                  pl.BlockSpec((1, LANES), lambda b, s: (0, 0)),
                  pl.BlockSpec(mat.shape, lambda b, s: (0, 0)),
                  pl.BlockSpec(masks.shape, lambda b, s: (0, 0, 0))],
        out_specs=pl.BlockSpec((ts, HG_W), lambda b, s: (b * spb + s, 0)),
        out_shape=jax.ShapeDtypeStruct((m, HG_W), BF16),
        scratch_shapes=[pltpu.VMEM((HG_HEADS, HG_VAL_DIM, HG_KEY_DIM), F32)],
        compiler_params=_cparams(("arbitrary", "arbitrary")),
        name="hgrn2",
    )(proj, side, proj, proj, lb.reshape(1, HG_W), norm_g.reshape(1, HG_VAL_DIM),
      jnp.asarray(mat, BF16), jnp.asarray(masks))


def _lb_kernel(x_ref, o_ref):
    x = x_ref[...]
    e = jnp.exp(x - jnp.max(x, axis=0, keepdims=True))
    p = e / jnp.sum(e, axis=0, keepdims=True)
    depth = x.shape[0]
    run = jnp.zeros_like(p[0:1])
    for l in range(depth):
        run = run + p[l:l + 1]
        o_ref[l:l + 1, :] = run - p[0:1]


def _lower_bounds(hg_lower_bounds):
    return pl.pallas_call(
        _lb_kernel,
        out_shape=jax.ShapeDtypeStruct(hg_lower_bounds.shape, F32),
        name="hg_lower_bounds",
    )(hg_lower_bounds.astype(F32))


_COL_Q = 0
_COL_CONV = ATT_Q_W + 2 * ATT_KV_W + IDX_Q_W
_COL_HG = _COL_CONV + 3 * CONV_WIDTH
_COL_GATE = _COL_HG + 3 * HG_W
_SIDE_SMALL = HG_W
_SIDE_W = HG_W + LANES


def _split_w_in(w_in, d_model):
    sizes = (ATT_Q_W, ATT_KV_W, ATT_KV_W, IDX_Q_W, IDX_HEAD_DIM, IDX_HEADS,
             CONV_WIDTH, CONV_WIDTH, CONV_WIDTH, HG_W, HG_W, HG_W, HG_W, N_BRANCHES * d_model)
    offs = np.cumsum((0,) + sizes)
    part = lambda n: w_in[..., offs[n]:offs[n + 1]]
    main = jnp.concatenate([part(n) for n in (0, 1, 2, 3, 6, 7, 8, 9, 11, 12, 13)], axis=-1).astype(BF16)
    side = jnp.concatenate([part(10), part(4), part(5)], axis=-1)
    side = jnp.pad(side, ((0, 0), (0, 0), (0, _SIDE_W - side.shape[-1]))).astype(BF16)
    return main, side


def kernel(x, c, positions, ada_w, ada_b, norm_mix_g, w_in, conv_w, hg_lower_bounds, hg_norm_g,
           w_o_attn, w_o_conv, w_o_hgrn, w_out, norm_mlp_g, w_mlp1, w_mlp2, final_norm_g):
    batch, seq, d = x.shape
    depth = ada_w.shape[0]
    m = batch * seq
    topk = min(INDEX_TOPK, seq // 4)
    lb_all = _lower_bounds(hg_lower_bounds)
    mod = _modulation(c, ada_w, ada_b)
    tables = _rope_tables(positions)
    w_main, w_side = _split_w_in(w_in, d)
    wa, wc, wh, wo = (w.astype(BF16) for w in (w_o_attn, w_o_conv, w_o_hgrn, w_out))
    w1, w2 = w_mlp1.astype(BF16), w_mlp2.astype(BF16)
    x2 = x.reshape(m, d)
    mods = [[mod[l, :, n * d:(n + 1) * d] for n in range(ADA_CHUNKS)] for l in range(depth)]
    h = _norm(x2, norm_mix_g[0], seq, mods[0][0], mods[0][1])
    for l in range(depth):
        sh1, sc1, g1, sh2, sc2, g2 = mods[l]
        proj = _matmul(h, w_main, out_dtype=BF16, tn=2048, layer=l)
        side = _matmul(h, w_side, out_dtype=F32, tn=_SIDE_W, layer=l)
        q, k, vt, qi, ki, wit = _rope_split(proj, side, tables)
        att = _dsa_attention(q, k, vt, qi, ki, wit, batch, seq, topk)
        hg = _hgrn2(proj, side, lb_all[l], hg_norm_g[l], batch, seq, _COL_HG)
        x2, h = _merge_out(att, hg, wa, wc, wh, proj, _COL_CONV, conv_w[l], _COL_GATE, wo, x2, g1,
                           norm_mlp_g[l], sh2, sc2, seq, l)
        a = _matmul(h, w1, out_dtype=BF16, relu2=True, tm=256, tn=4 * d, layer=l)
        if l + 1 < depth:
            x2, h = _mlp_down(a, w2, x2, g2, norm_mix_g[l + 1], mods[l + 1][0], mods[l + 1][1], seq, l,
                              emit_x=True, h_dtype=BF16)
        else:
            zero = jnp.zeros_like(g2)
            out = _mlp_down(a, w2, x2, g2, final_norm_g, zero, zero, seq, l, emit_x=False, h_dtype=x.dtype)
    return out.reshape(batch, seq, d)
```

```python
import functools

import numpy as np
import jax
import jax.numpy as jnp
from jax import lax
from jax.experimental import pallas as pl
from jax.experimental.pallas import tpu as pltpu

ATT_HEADS = 8
ATT_KV_HEADS = 2
ATT_HEAD_DIM = 128
IDX_HEADS = 8
IDX_HEAD_DIM = 64
INDEX_TOPK = 256
ROPE_THETA = 500000.0
ROPE_FRACTION = 4
MASK_VALUE = -1e30
CONV_WIDTH = 1024
CONV_K = 3
HG_HEADS = 8
HG_KEY_DIM = 128
HG_VAL_DIM = 128
F_MIN = 1e-30
N_BRANCHES = 3
NORM_EPS = 1e-6
ADA_CHUNKS = 6

ATT_Q_W = ATT_HEADS * ATT_HEAD_DIM
ATT_KV_W = ATT_KV_HEADS * ATT_HEAD_DIM
IDX_Q_W = IDX_HEADS * IDX_HEAD_DIM
HG_W = HG_HEADS * HG_KEY_DIM
ATT_LOGIT_SCALE = (ATT_HEAD_DIM ** -0.5) * float(np.log2(np.e))

LANES = 128
SUBLANES = 8
HALO_ROWS = 16
VMEM_LIMIT_BYTES = 56 * 1024 * 1024

INT_MIN = -2 ** 31
SEARCH_VARIANTS = 8
HG_CHUNK = 64
HG_LEVELS = 6
HG_UNROLL = 4
HG_SPLIT = 3

BF16 = jnp.bfloat16
F32 = jnp.float32


def _cparams(sem):
    return pltpu.CompilerParams(dimension_semantics=sem, vmem_limit_bytes=VMEM_LIMIT_BYTES)


def _dot(a, b):
    return jnp.dot(a, b, preferred_element_type=F32)


def _dot_nt(a, b):
    return lax.dot_general(a, b, (((1,), (1,)), ((), ())), preferred_element_type=F32)


def _sigmoid(x):
    return 1.0 / (1.0 + jnp.exp(-x))


def _mod_kernel(c_ref, w_ref, b_ref, o_ref):
    c = c_ref[...]
    ca = (c * _sigmoid(c)).astype(BF16)
    o_ref[...] = _dot(ca, w_ref[...].astype(BF16)) + b_ref[...]


def _modulation(c, ada_w, ada_b):
    depth, d, n = ada_w.shape
    b = c.shape[0]
    bp = -(-b // SUBLANES) * SUBLANES
    cp = jnp.pad(c, ((0, bp - b), (0, 0)))
    tn = 1024
    out = pl.pallas_call(
        _mod_kernel,
        grid=(depth, n // tn),
        in_specs=[
            pl.BlockSpec((bp, d), lambda l, j: (0, 0)),
            pl.BlockSpec((None, d, tn), lambda l, j: (l, 0, j)),
            pl.BlockSpec((None, 1, tn), lambda l, j: (l, 0, j)),
        ],
        out_specs=pl.BlockSpec((None, bp, tn), lambda l, j: (l, 0, j)),
        out_shape=jax.ShapeDtypeStruct((depth, bp, n), F32),
        compiler_params=_cparams(("arbitrary", "arbitrary")),
        name="adaln_mod",
    )(cp, ada_w, ada_b.reshape(depth, 1, n))
    return out[:, :b]


def _norm_kernel(x_ref, g_ref, sh_ref, sc_ref, o_ref):
    x = x_ref[...]
    y = x * lax.rsqrt(jnp.mean(x * x, axis=-1, keepdims=True) + NORM_EPS) * g_ref[...]
    o_ref[...] = (y * (1.0 + sc_ref[0]) + sh_ref[0]).astype(o_ref.dtype)


def _norm(x2, g, seq, shift, scale, out_dtype=BF16, tm=512):
    m, d = x2.shape
    tpb = seq // tm
    row = pl.BlockSpec((1, 1, d), lambda i: (i // tpb, 0, 0))
    return pl.pallas_call(
        _norm_kernel,
        grid=(m // tm,),
        in_specs=[pl.BlockSpec((tm, d), lambda i: (i, 0)), pl.BlockSpec((1, d), lambda i: (0, 0)), row, row],
        out_specs=pl.BlockSpec((tm, d), lambda i: (i, 0)),
        out_shape=jax.ShapeDtypeStruct((m, d), out_dtype),
        compiler_params=_cparams(("arbitrary",)),
        name="rmsnorm",
    )(x2, g.reshape(1, d), shift[:, None, :], scale[:, None, :])


def _conv_tile(cb_ref, cc_ref, cx_ref, hc_ref, hx_ref, w_ref, at_seq_start):
    u = cc_ref[...].astype(F32) * cx_ref[...].astype(F32)
    halo = hc_ref[...].astype(F32) * hx_ref[...].astype(F32)
    halo = jnp.where(at_seq_start, 0.0, halo)
    row = lax.broadcasted_iota(jnp.int32, u.shape, 0)
    h1 = halo[HALO_ROWS - 1:HALO_ROWS, :]
    h2 = halo[HALO_ROWS - 2:HALO_ROWS - 1, :]
    u1 = jnp.where(row == 0, h1, pltpu.roll(u, 1, axis=0))
    u2 = jnp.where(row == 0, h2, jnp.where(row == 1, h1, pltpu.roll(u, 2, axis=0)))
    w = w_ref[...]
    conv = u2 * w[0:1, :] + u1 * w[1:2, :] + u * w[2:3, :]
    return cb_ref[...].astype(F32) * conv


def _merge_kernel(att_ref, cb_ref, cc_ref, cx_ref, hc_ref, hx_ref, cw_ref, hg_ref, wa_ref, wc_ref, wh_ref,
                  ga_ref, gc_ref, gh_ref, wo_ref, x_ref, gr_ref, ng_ref, sh_ref, sc_ref, o_ref, h_ref,
                  *, tm, seq):
    at_start = (pl.program_id(0) * tm) % seq == 0
    cv = _conv_tile(cb_ref, cc_ref, cx_ref, hc_ref, hx_ref, cw_ref, at_start).astype(BF16)
    merged = (_sigmoid(ga_ref[...].astype(F32)) * _dot(att_ref[...], wa_ref[...])
              + _sigmoid(gc_ref[...].astype(F32)) * _dot(cv, wc_ref[...])
              + _sigmoid(gh_ref[...].astype(F32)) * _dot(hg_ref[...], wh_ref[...]))
    y = x_ref[...] + gr_ref[0] * _dot(merged.astype(BF16), wo_ref[...])
    o_ref[...] = y
    yn = y * lax.rsqrt(jnp.mean(y * y, axis=-1, keepdims=True) + NORM_EPS) * ng_ref[...]
    h_ref[...] = (yn * (1.0 + sc_ref[0]) + sh_ref[0]).astype(h_ref.dtype)


def _merge_out(att, hg, wa, wc, wh, proj, conv_off, conv_w, gate_off, wout, x2, gate_row, norm_g, shift,
               scale, seq, layer, tm=256):
    m, d = x2.shape
    tpb = seq // tm
    g0 = gate_off // d
    c0 = conv_off // CONV_WIDTH
    rpb = tm // HALO_ROWS
    cmain = lambda off: pl.BlockSpec((tm, CONV_WIDTH), lambda i: (i, c0 + off))
    chalo = lambda off: pl.BlockSpec((HALO_ROWS, CONV_WIDTH), lambda i: (jnp.maximum(i * rpb - 1, 0), c0 + off))
    act = lambda a: pl.BlockSpec((tm, a.shape[1]), lambda i: (i, 0))
    resident = lambda w: pl.BlockSpec((None,) + w.shape[1:], lambda i: (layer, 0, 0),
                                      pipeline_mode=pl.Buffered(1))
    gate = lambda b: pl.BlockSpec((tm, d), lambda i: (i, g0 + b))
    row = pl.BlockSpec((1, 1, d), lambda i: (i // tpb, 0, 0))
    full = pl.BlockSpec((tm, d), lambda i: (i, 0))
    return pl.pallas_call(
        functools.partial(_merge_kernel, tm=tm, seq=seq),
        grid=(m // tm,),
        in_specs=[act(att), cmain(0), cmain(1), cmain(2), chalo(1), chalo(2),
                  pl.BlockSpec((CONV_K, CONV_WIDTH), lambda i: (0, 0)),
                  act(hg), resident(wa), resident(wc), resident(wh),
                  gate(0), gate(1), gate(2), resident(wout), full, row,
                  pl.BlockSpec((1, d), lambda i: (0, 0)), row, row],
        out_specs=[full, full],
        out_shape=[jax.ShapeDtypeStruct((m, d), F32), jax.ShapeDtypeStruct((m, d), BF16)],
        compiler_params=_cparams(("arbitrary",)),
        name="merge_out",
    )(att, proj, proj, proj, proj, proj, conv_w, hg, wa, wc, wh, proj, proj, proj, wout, x2,
      gate_row[:, None, :], norm_g.reshape(1, d), shift[:, None, :], scale[:, None, :])


def _mm_kernel(a_ref, w_ref, o_ref, *, relu2):
    y = _dot(a_ref[...], w_ref[...])
    if relu2:
        y = jnp.square(jnp.maximum(y, 0.0))
    o_ref[...] = y.astype(o_ref.dtype)


def _matmul(a, w, *, out_dtype, relu2=False, tm=1024, tn=512, layer=0):
    m = a.shape[0]
    kdim, n = w.shape[-2:]
    tn = min(tn, n)
    w_mode = pl.Buffered(1) if tn == n else None
    return pl.pallas_call(
        functools.partial(_mm_kernel, relu2=relu2),
        grid=(m // tm, n // tn),
        in_specs=[pl.BlockSpec((tm, kdim), lambda i, j: (i, 0)),
                  pl.BlockSpec((None, kdim, tn), lambda i, j: (layer, 0, j), pipeline_mode=w_mode)],
        out_specs=pl.BlockSpec((tm, tn), lambda i, j: (i, j)),
        out_shape=jax.ShapeDtypeStruct((m, n), out_dtype),
        compiler_params=_cparams(("arbitrary", "arbitrary")),
        name="matmul_relu2" if relu2 else "matmul",
    )(a, w)


def _mlp_down_kernel(a_ref, w_ref, x_ref, gr_ref, ng_ref, sh_ref, sc_ref, *outs, emit_x):
    y = x_ref[...] + gr_ref[0] * _dot(a_ref[...], w_ref[...])
    if emit_x:
        outs[0][...] = y
    yn = y * lax.rsqrt(jnp.mean(y * y, axis=-1, keepdims=True) + NORM_EPS) * ng_ref[...]
    outs[-1][...] = (yn * (1.0 + sc_ref[0]) + sh_ref[0]).astype(outs[-1].dtype)


def _mlp_down(a, w, x2, gate_row, norm_g, shift, scale, seq, layer, *, emit_x, h_dtype, tm=256):
    m, d = x2.shape
    kdim = w.shape[1]
    tpb = seq // tm
    row = pl.BlockSpec((1, 1, d), lambda i: (i // tpb, 0, 0))
    full = pl.BlockSpec((tm, d), lambda i: (i, 0))
    out_specs, out_shape = [full], [jax.ShapeDtypeStruct((m, d), h_dtype)]
    if emit_x:
        out_specs, out_shape = [full] + out_specs, [jax.ShapeDtypeStruct((m, d), F32)] + out_shape
    outs = pl.pallas_call(
        functools.partial(_mlp_down_kernel, emit_x=emit_x),
        grid=(m // tm,),
        in_specs=[pl.BlockSpec((tm, kdim), lambda i: (i, 0)),
                  pl.BlockSpec((None, kdim, d), lambda i: (layer, 0, 0), pipeline_mode=pl.Buffered(1)),
                  full, row, pl.BlockSpec((1, d), lambda i: (0, 0)), row, row],
        out_specs=out_specs,
        out_shape=out_shape,
        compiler_params=_cparams(("arbitrary",)),
        name="mlp_down",
    )(a, w, x2, gate_row[:, None, :], norm_g.reshape(1, d), shift[:, None, :], scale[:, None, :])
    return outs if emit_x else outs[0]


def _rope_rows(period, half, rot):
    j = np.arange(LANES) % period
    inv = ROPE_THETA ** (-(np.arange(half, dtype=np.float32)) / np.float32(half))
    inv_row = np.where(j < rot, inv.astype(np.float32)[j % half], 0.0).astype(np.float32)
    sign_row = np.where(j < half, -1.0, np.where(j < rot, 1.0, 0.0)).astype(np.float32)
    first_row = (j < half).astype(np.float32)
    return np.stack([inv_row, sign_row, first_row])[:, None, :]


def _rope_apply(x, cos_t, sin_s, first, half):
    up = pltpu.roll(x, LANES - half, axis=1)
    dn = pltpu.roll(x, half, axis=1)
    sw = jnp.where(first > 0.5, up, dn)
    return x * cos_t + sw * sin_s


def _rope_table_kernel(pos_ref, rows_a_ref, rows_i_ref, ca_ref, sa_ref, ci_ref, si_ref):
    pos = pos_ref[...].astype(F32)
    ang_a = pos * rows_a_ref[0]
    ca_ref[...] = jnp.cos(ang_a)
    sa_ref[...] = jnp.sin(ang_a) * rows_a_ref[1]
    ang_i = pos * rows_i_ref[0]
    ci_ref[...] = jnp.cos(ang_i)
    si_ref[...] = jnp.sin(ang_i) * rows_i_ref[1]


def _rope_consts():
    rows_a = jnp.asarray(_rope_rows(ATT_HEAD_DIM, ATT_HEAD_DIM // ROPE_FRACTION // 2,
                                    ATT_HEAD_DIM // ROPE_FRACTION))
    rows_i = jnp.asarray(_rope_rows(IDX_HEAD_DIM, IDX_HEAD_DIM // ROPE_FRACTION // 2,
                                    IDX_HEAD_DIM // ROPE_FRACTION))
    return rows_a, rows_i


def _rope_tables(positions, tm=512):
    m = positions.size
    rows_a, rows_i = _rope_consts()
    rows_spec = pl.BlockSpec((3, 1, LANES), lambda i: (0, 0, 0))
    tab = pl.BlockSpec((tm, LANES), lambda i: (i, 0))
    return pl.pallas_call(
        _rope_table_kernel,
        grid=(m // tm,),
        in_specs=[pl.BlockSpec((tm, 1), lambda i: (i, 0)), rows_spec, rows_spec],
        out_specs=[tab] * 4,
        out_shape=[jax.ShapeDtypeStruct((m, LANES), F32)] * 4,
        compiler_params=_cparams(("arbitrary",)),
        name="rope_tables",
    )(positions.reshape(m, 1), rows_a, rows_i)


def _rope_kernel(ca_ref, sa_ref, ci_ref, si_ref, rows_a_ref, rows_i_ref, q_ref, kv_ref, qi_ref, sm_ref,
                 qo_ref, ko_ref, vo_ref, qio_ref, kio_ref, wio_ref, *, w_scale):
    cos_a, sin_a, first_a = ca_ref[...], sa_ref[...], rows_a_ref[2]
    cos_i, sin_i, first_i = ci_ref[...], si_ref[...], rows_i_ref[2]
    half_a = ATT_HEAD_DIM // ROPE_FRACTION // 2
    half_i = IDX_HEAD_DIM // ROPE_FRACTION // 2
    for h in range(ATT_HEADS):
        sl = slice(h * LANES, (h + 1) * LANES)
        qo_ref[:, sl] = (_rope_apply(q_ref[:, sl].astype(F32), cos_a, sin_a, first_a, half_a)
                         * ATT_LOGIT_SCALE).astype(BF16)
    for h in range(ATT_KV_HEADS):
        sl = slice(h * LANES, (h + 1) * LANES)
        ko_ref[:, sl] = _rope_apply(kv_ref[:, sl].astype(F32), cos_a, sin_a, first_a, half_a).astype(BF16)
    vo_ref[...] = kv_ref[:, ATT_KV_W:].astype(F32).T.astype(BF16)
    for h in range(IDX_Q_W // LANES):
        sl = slice(h * LANES, (h + 1) * LANES)
        qio_ref[:, sl] = _rope_apply(qi_ref[:, sl].astype(F32), cos_i, sin_i, first_i, half_i).astype(BF16)
    sm = sm_ref[...]
    ki = _rope_apply(sm, cos_i, sin_i, first_i, half_i)
    kio_ref[...] = ki[:, :IDX_HEAD_DIM].astype(BF16)
    wio_ref[...] = sm.T[IDX_HEAD_DIM:IDX_HEAD_DIM + IDX_HEADS, :] * w_scale


def _rope_split(proj, side, tables, tm=512):
    m = proj.shape[0]
    rows_a, rows_i = _rope_consts()
    w_scale = (IDX_HEADS ** -0.5) * (IDX_HEAD_DIM ** -0.5)
    rows_spec = pl.BlockSpec((3, 1, LANES), lambda i: (0, 0, 0))
    tab = pl.BlockSpec((tm, LANES), lambda i: (i, 0))
    outs = pl.pallas_call(
        functools.partial(_rope_kernel, w_scale=w_scale),
        grid=(m // tm,),
        in_specs=[
            tab, tab, tab, tab,
            rows_spec, rows_spec,
            pl.BlockSpec((tm, ATT_Q_W), lambda i: (i, 0)),
            pl.BlockSpec((tm, 2 * ATT_KV_W), lambda i: (i, ATT_Q_W // (2 * ATT_KV_W))),
            pl.BlockSpec((tm, IDX_Q_W), lambda i: (i, (ATT_Q_W + 2 * ATT_KV_W) // IDX_Q_W)),
            pl.BlockSpec((tm, LANES), lambda i: (i, _SIDE_SMALL // LANES)),
        ],
        out_specs=[
            pl.BlockSpec((tm, ATT_Q_W), lambda i: (i, 0)),
            pl.BlockSpec((tm, ATT_KV_W), lambda i: (i, 0)),
            pl.BlockSpec((ATT_KV_W, tm), lambda i: (0, i)),
            pl.BlockSpec((tm, IDX_Q_W), lambda i: (i, 0)),
            pl.BlockSpec((tm, IDX_HEAD_DIM), lambda i: (i, 0)),
            pl.BlockSpec((IDX_HEADS, tm), lambda i: (0, i)),
        ],
        out_shape=[
            jax.ShapeDtypeStruct((m, ATT_Q_W), BF16),
            jax.ShapeDtypeStruct((m, ATT_KV_W), BF16),
            jax.ShapeDtypeStruct((ATT_KV_W, m), BF16),
            jax.ShapeDtypeStruct((m, IDX_Q_W), BF16),
            jax.ShapeDtypeStruct((m, IDX_HEAD_DIM), BF16),
            jax.ShapeDtypeStruct((IDX_HEADS, m), F32),
        ],
        compiler_params=_cparams(("arbitrary",)),
        name="rope_split",
    )(*tables, rows_a, rows_i, proj, proj, proj, side)
    return outs


def _attn_kernel(qi_ref, wi_ref, ki_ref, q_ref, k_ref, vt_ref, o_ref, skey_ref,
                 da_ref, db_ref, sa_ref, sb_ref, *, tq, kc, topk, seq):
    i = pl.program_id(1)
    nchunk = lax.shift_right_logical((i + 1) * tq + (kc - 1), int(np.log2(kc)))
    qpos = i * tq + lax.broadcasted_iota(jnp.int32, (1, tq), 1)
    rb = 8 * SUBLANES

    qi = qi_ref[0]
    wi = wi_ref[...]
    qi_s = jnp.concatenate([qi[:, h * IDX_HEAD_DIM:(h + 1) * IDX_HEAD_DIM] for h in range(IDX_HEADS)],
                           axis=0)

    kh = kc // 2
    last_half = seq // kh - 1

    def score_dots(half, dst_ref):
        k0 = pl.multiple_of(jnp.minimum(half, last_half) * kh, kh)
        dst_ref[...] = _dot_nt(ki_ref[0, pl.ds(k0, kh), :], qi_s)

    def score_keys(half, src_ref):
        k0 = pl.multiple_of(half * kh, kh)
        for j in range(kh // rb):
            rows = slice(j * rb, (j + 1) * rb)
            acc = jnp.zeros((rb, tq), F32)
            for h in range(IDX_HEADS):
                acc = acc + jnp.maximum(src_ref[rows, h * tq:(h + 1) * tq], 0.0) * wi[h:h + 1, :]
            kpos = k0 + j * rb + lax.broadcasted_iota(jnp.int32, (rb, tq), 0)
            bits = pltpu.bitcast(acc, jnp.int32)
            skey = jnp.where(bits < 0, bits ^ jnp.int32(0x7FFFFFFF), bits)
            skey = jnp.where(kpos <= qpos, skey, jnp.int32(INT_MIN))
            skey_ref[pl.ds(k0 + j * rb, rb), :] = skey

    def score_chunk(c, carry):
        score_dots(2 * c + 1, db_ref)
        score_keys(2 * c, da_ref)
        score_dots(2 * c + 2, da_ref)
        score_keys(2 * c + 1, db_ref)
        return carry

    score_dots(0, da_ref)
    lax.fori_loop(0, nchunk, score_chunk, 0)

    nacc = 4

    def count_where(pred_fn):
        row_iota = lax.broadcasted_iota(jnp.int32, (SUBLANES, tq), 0)

        def chunk_body(c, parts):
            k0 = pl.multiple_of(c * kc, kc)
            parts = list(parts)
            xs = skey_ref[pl.ds(k0, kc), :]
            for j in range(kc // SUBLANES):
                x = xs[j * SUBLANES:(j + 1) * SUBLANES]
                parts[j % nacc] = parts[j % nacc] + jnp.where(pred_fn(x, k0 + j * SUBLANES + row_iota), 1, 0)
            return tuple(parts)
        zero = jnp.zeros((SUBLANES, tq), jnp.int32)
        parts = lax.fori_loop(0, nchunk, chunk_body, (zero,) * nacc)
        return jnp.sum(sum(parts[1:], parts[0]), axis=0, keepdims=True)

    nfull = seq // kc
    step = max(nfull // SEARCH_VARIANTS, 1)
    variant = lax.div(nchunk + (step - 1), step) - 1

    def fill_chunk(c, carry):
        rows = pl.ds(pl.multiple_of(c * kc, kc), kc)
        skey_ref[rows, :] = jnp.full((kc, tq), INT_MIN, jnp.int32)
        return carry

    lax.fori_loop(nchunk, (variant + 1) * step, fill_chunk, 0)

    def search_all(nc):
        def bit_body(b, prefix):
            cand = prefix | lax.shift_left(jnp.int32(1), 31 - b)
            cand_s = jnp.broadcast_to(cand ^ jnp.int32(INT_MIN), (SUBLANES, tq))
            parts = [jnp.zeros((SUBLANES, tq), jnp.int32)] * nacc
            for j in range(nc * kc // SUBLANES):
                x = skey_ref[j * SUBLANES:(j + 1) * SUBLANES, :]
                parts[j % nacc] = parts[j % nacc] + jnp.where(x >= cand_s, 1, 0)
            cnt = jnp.sum(sum(parts[1:], parts[0]), axis=0, keepdims=True)
            return jnp.where(cnt >= topk, cand, prefix)
        return lax.fori_loop(0, 32, bit_body, jnp.zeros((1, tq), jnp.int32))

    prefix = lax.switch(variant, [functools.partial(search_all, nc) for nc in range(step, nfull + 1, step)])
    thr = prefix ^ jnp.int32(INT_MIN)

    n_gt = count_where(lambda x, _: x > thr)
    n_eq = count_where(lambda x, _: x == thr)
    need = topk - n_gt
    excess = jnp.where((prefix != 0) & (n_eq > need), 1, 0)
    any_excess = jnp.max(excess) > 0
    idx_bits = int(np.log2(seq))

    def tie_search():
        def tie_bit(b, x):
            cand = x | lax.shift_left(jnp.int32(1), idx_bits - 1 - b)
            below = count_where(lambda xk, kidx: (xk == thr) & (kidx < cand))
            return jnp.where(below < need, cand, x)
        return lax.fori_loop(0, idx_bits, tie_bit, jnp.zeros((1, tq), jnp.int32))

    tie_last = lax.cond(any_excess, tie_search, lambda: jnp.full((1, tq), seq, jnp.int32))

    groups = ATT_KV_HEADS
    rep = ATT_HEADS // ATT_KV_HEADS
    q = q_ref[0]
    q_g = [jnp.concatenate([q[:, (g * rep + r) * LANES:(g * rep + r + 1) * LANES] for r in range(rep)], axis=0)
           for g in range(groups)]

    def qk_dots(half, dst_ref):
        k0 = pl.multiple_of(jnp.minimum(half, last_half) * kh, kh)
        for g in range(groups):
            dst_ref[g] = _dot_nt(k_ref[0, pl.ds(k0, kh), g * LANES:(g + 1) * LANES], q_g[g])

    def softmax_pv(half, src_ref, carry):
        k0 = pl.multiple_of(half * kh, kh)
        x = skey_ref[pl.ds(k0, kh), :]
        kidx = k0 + lax.broadcasted_iota(jnp.int32, (kh, tq), 0)
        sel = (x > thr) | ((x == thr) & (kidx <= tie_last))
        sel = sel & (x != jnp.int32(INT_MIN))
        bias = jnp.where(sel, 0.0, MASK_VALUE)
        bias = jnp.concatenate([bias] * rep, axis=1)
        new = []
        for g in range(groups):
            m_old, l_old, acc_old = carry[g]
            vtg = vt_ref[g * LANES:(g + 1) * LANES, pl.ds(k0, kh)]
            sm = src_ref[g] + bias
            m_new = jnp.maximum(m_old, jnp.max(sm, axis=0, keepdims=True))
            p = jnp.exp2(sm - m_new)
            alpha = jnp.exp2(m_old - m_new)
            l_new = alpha * l_old + jnp.sum(p, axis=0, keepdims=True)
            acc_new = alpha * acc_old + _dot(vtg, p.astype(BF16))
            new.append((m_new, l_new, acc_new))
        return tuple(new)

    def attn_chunk(c, carry):
        qk_dots(2 * c + 1, sb_ref)
        carry = softmax_pv(2 * c, sa_ref, carry)
        qk_dots(2 * c + 2, sa_ref)
        return softmax_pv(2 * c + 1, sb_ref, carry)

    init = tuple((jnp.full((1, rep * tq), MASK_VALUE, F32), jnp.zeros((1, rep * tq), F32),
                  jnp.zeros((LANES, rep * tq), F32)) for _ in range(groups))
    qk_dots(0, sa_ref)
    fin = lax.fori_loop(0, nchunk, attn_chunk, init)
    for g in range(groups):
        _, l_f, acc_f = fin[g]
        o_t = acc_f / l_f
        for r in range(rep):
            h = g * rep + r
            o_ref[0, :, h * LANES:(h + 1) * LANES] = o_t[:, r * tq:(r + 1) * tq].T.astype(o_ref.dtype)


def _dsa_attention(q, k, vt, qi, ki, wit, batch, seq, topk, tq=128, kc=512):
    nblk = seq // tq
    rep = ATT_HEADS // ATT_KV_HEADS
    r3 = lambda a: a.reshape(batch, seq, a.shape[-1])
    qblk = lambda w: pl.BlockSpec((1, tq, w), lambda b, i: (b, i, 0))
    full = lambda w: pl.BlockSpec((1, seq, w), lambda b, i: (b, 0, 0))
    out = pl.pallas_call(
        functools.partial(_attn_kernel, tq=tq, kc=kc, topk=topk, seq=seq),
        grid=(batch, nblk),
        in_specs=[qblk(IDX_Q_W), pl.BlockSpec((IDX_HEADS, tq), lambda b, i: (0, b * nblk + i)),
                  full(IDX_HEAD_DIM), qblk(ATT_Q_W), full(ATT_KV_W),
                  pl.BlockSpec((ATT_KV_W, seq), lambda b, i: (0, b))],
        out_specs=qblk(ATT_Q_W),
        out_shape=jax.ShapeDtypeStruct((batch, seq, ATT_Q_W), BF16),
        scratch_shapes=[pltpu.VMEM((seq, tq), jnp.int32),
                        pltpu.VMEM((kc // 2, IDX_HEADS * tq), F32), pltpu.VMEM((kc // 2, IDX_HEADS * tq), F32),
                        pltpu.VMEM((ATT_KV_HEADS, kc // 2, rep * tq), F32),
                        pltpu.VMEM((ATT_KV_HEADS, kc // 2, rep * tq), F32)],
        compiler_params=_cparams(("arbitrary", "arbitrary")),
        name="dsa_attention",
    )(r3(qi), wit, r3(ki), r3(q), r3(k), vt)
    return out.reshape(batch * seq, ATT_Q_W)


def _hg_tables():
    c = HG_CHUNK
    t = np.arange(c)
    mats, masks = [], []
    for l in range(HG_LEVELS):
        h = 1 << l
        start = (t // (2 * h)) * (2 * h)
        p = start + h - 1
        right = (t - start) >= h
        u = t[None, :]
        r_m = right[:, None] & (u > p[:, None]) & (u <= t[:, None])
        l_m = (~right)[:, None] & (u > t[:, None]) & (u <= p[:, None])
        mats.append(r_m | l_m)
        same = (t[:, None] // (2 * h)) == (t[None, :] // (2 * h))
        masks.append(same & right[:, None] & (~right)[None, :])
    mats.append(t[None, :] <= t[:, None])
    mats.append(t[None, :] > t[:, None])
    mat = np.concatenate(mats, 0).astype(np.float32)
    return np.concatenate([mat] * HG_SPLIT, 1), np.stack(masks).astype(np.float32)


def _split3(x):
    hi = x.astype(BF16)
    r1 = x - hi.astype(F32)
    mid = r1.astype(BF16)
    lo = (r1 - mid.astype(F32)).astype(BF16)
    return jnp.concatenate([hi, mid, lo], axis=0)


def _hg_kernel(hq_ref, hf_ref, hi_ref, hg_ref, lb_ref, g_ref, mat_ref, mask_ref, o_ref, state_ref, *, ts):
    c = HG_CHUNK

    @pl.when(pl.program_id(1) == 0)
    def _():
        state_ref[...] = jnp.zeros_like(state_ref)

    g = g_ref[...]

    def chunk(ci, carry):
        rows = pl.ds(pl.multiple_of(ci * c, c), c)
        lb = lb_ref[...]
        f = lb + (1.0 - lb) * _sigmoid(hf_ref[rows, :])
        logf = jnp.log(jnp.maximum(f, F_MIN))
        kin = 1.0 - f
        q = hq_ref[rows, :].astype(F32)
        v16 = hi_ref[rows, :]
        e = jnp.exp(_dot(mat_ref[...], _split3(logf)))
        qd = [(q * e[l * c:(l + 1) * c]).astype(BF16) for l in range(HG_LEVELS + 1)]
        kd = [(kin * e[l * c:(l + 1) * c]).astype(BF16) for l in range(HG_LEVELS)]
        e_b = e[HG_LEVELS * c:(HG_LEVELS + 1) * c]
        kk = (kin * e[(HG_LEVELS + 1) * c:(HG_LEVELS + 2) * c]).astype(BF16)
        qk = q * kin
        gate = hg_ref[rows, :].astype(F32)
        gate = gate * _sigmoid(gate)
        for h in range(HG_HEADS):
            cols = slice(h * LANES, (h + 1) * LANES)
            attn = jnp.zeros((c, c), F32)
            for l in range(HG_LEVELS):
                attn = attn + _dot_nt(qd[l][:, cols], kd[l][:, cols]) * mask_ref[l]
            state_t = state_ref[h]
            v16h = v16[:, cols]
            diag = jnp.sum(qk[:, cols], axis=1, keepdims=True)
            out = (_dot(attn.astype(BF16), v16h) + diag * v16h.astype(F32)
                   + _dot_nt(qd[HG_LEVELS][:, cols], state_t.astype(BF16)))
            kv_t = lax.dot_general(v16h, kk[:, cols], (((0,), (0,)), ((), ())),
                                   preferred_element_type=F32)
            state_ref[h] = state_t * e_b[c - 1:c, cols] + kv_t
            y = out * lax.rsqrt(jnp.mean(out * out, axis=-1, keepdims=True) + NORM_EPS) * g
            o_ref[rows, cols] = (y * gate[:, cols]).astype(o_ref.dtype)
        return carry

    def chunk_group(gi, carry):
        for u in range(HG_UNROLL):
            carry = chunk(HG_UNROLL * gi + u, carry)
        return carry

    lax.fori_loop(0, ts // (HG_UNROLL * c), chunk_group, 0)


def _hgrn2(proj, side, lb, norm_g, batch, seq, col0, ts=512):
    m = proj.shape[0]
    c0 = col0 // HG_W
    spb = seq // ts
    mat, masks = _hg_tables()
    col = lambda off: pl.BlockSpec((ts, HG_W), lambda b, s: (b * spb + s, c0 + off))
    return pl.pallas_call(
        functools.partial(_hg_kernel, ts=ts),
        grid=(batch, spb),
        in_specs=[col(0), pl.BlockSpec((ts, HG_W), lambda b, s: (b * spb + s, 0)), col(1), col(2),
                  pl.BlockSpec((1, HG_W), lambda b, s: (0, 0)),
                  pl.BlockSpec((1, LANES), lambda b, s: (0, 0)),
                  pl.BlockSpec(mat.shape, lambda b, s: (0, 0)),
                  pl.BlockSpec(masks.shape, lambda b, s: (0, 0, 0))],
        out_specs=pl.BlockSpec((ts, HG_W), lambda b, s: (b * spb + s, 0)),
        out_shape=jax.ShapeDtypeStruct((m, HG_W), BF16),
        scratch_shapes=[pltpu.VMEM((HG_HEADS, HG_VAL_DIM, HG_KEY_DIM), F32)],
        compiler_params=_cparams(("arbitrary", "arbitrary")),
        name="hgrn2",
    )(proj, side, proj, proj, lb.reshape(1, HG_W), norm_g.reshape(1, HG_VAL_DIM),
      jnp.asarray(mat, BF16), jnp.asarray(masks))


def _lb_kernel(x_ref, o_ref):
    x = x_ref[...]
    e = jnp.exp(x - jnp.max(x, axis=0, keepdims=True))
    p = e / jnp.sum(e, axis=0, keepdims=True)
    depth = x.shape[0]
    run = jnp.zeros_like(p[0:1])
    for l in range(depth):
        run = run + p[l:l + 1]
        o_ref[l:l + 1, :] = run - p[0:1]


def _lower_bounds(hg_lower_bounds):
    return pl.pallas_call(
        _lb_kernel,
        out_shape=jax.ShapeDtypeStruct(hg_lower_bounds.shape, F32),
        name="hg_lower_bounds",
    )(hg_lower_bounds.astype(F32))


_COL_Q = 0
_COL_CONV = ATT_Q_W + 2 * ATT_KV_W + IDX_Q_W
_COL_HG = _COL_CONV + 3 * CONV_WIDTH
_COL_GATE = _COL_HG + 3 * HG_W
_SIDE_SMALL = HG_W
_SIDE_W = HG_W + LANES


def _split_w_in(w_in, d_model):
    sizes = (ATT_Q_W, ATT_KV_W, ATT_KV_W, IDX_Q_W, IDX_HEAD_DIM, IDX_HEADS,
             CONV_WIDTH, CONV_WIDTH, CONV_WIDTH, HG_W, HG_W, HG_W, HG_W, N_BRANCHES * d_model)
    offs = np.cumsum((0,) + sizes)
    part = lambda n: w_in[..., offs[n]:offs[n + 1]]
    main = jnp.concatenate([part(n) for n in (0, 1, 2, 3, 6, 7, 8, 9, 11, 12, 13)], axis=-1).astype(BF16)
    side = jnp.concatenate([part(10), part(4), part(5)], axis=-1)
    side = jnp.pad(side, ((0, 0), (0, 0), (0, _SIDE_W - side.shape[-1]))).astype(BF16)
    return main, side


def kernel(x, c, positions, ada_w, ada_b, norm_mix_g, w_in, conv_w, hg_lower_bounds, hg_norm_g,
           w_o_attn, w_o_conv, w_o_hgrn, w_out, norm_mlp_g, w_mlp1, w_mlp2, final_norm_g):
    batch, seq, d = x.shape
    depth = ada_w.shape[0]
    m = batch * seq
    topk = min(INDEX_TOPK, seq // 4)
    lb_all = _lower_bounds(hg_lower_bounds)
    mod = _modulation(c, ada_w, ada_b)
    tables = _rope_tables(positions)
    w_main, w_side = _split_w_in(w_in, d)
    wa, wc, wh, wo = (w.astype(BF16) for w in (w_o_attn, w_o_conv, w_o_hgrn, w_out))
    w1, w2 = w_mlp1.astype(BF16), w_mlp2.astype(BF16)
    x2 = x.reshape(m, d)
    mods = [[mod[l, :, n * d:(n + 1) * d] for n in range(ADA_CHUNKS)] for l in range(depth)]
    h = _norm(x2, norm_mix_g[0], seq, mods[0][0], mods[0][1])
    for l in range(depth):
        sh1, sc1, g1, sh2, sc2, g2 = mods[l]
        proj = _matmul(h, w_main, out_dtype=BF16, tn=2048, layer=l)
        side = _matmul(h, w_side, out_dtype=F32, tn=_SIDE_W, layer=l)
        q, k, vt, qi, ki, wit = _rope_split(proj, side, tables)
        att = _dsa_attention(q, k, vt, qi, ki, wit, batch, seq, topk)
        hg = _hgrn2(proj, side, lb_all[l], hg_norm_g[l], batch, seq, _COL_HG)
        x2, h = _merge_out(att, hg, wa, wc, wh, proj, _COL_CONV, conv_w[l], _COL_GATE, wo, x2, g1,
                           norm_mlp_g[l], sh2, sc2, seq, l)
        a = _matmul(h, w1, out_dtype=BF16, relu2=True, tm=256, tn=4 * d, layer=l)
        if l + 1 < depth:
            x2, h = _mlp_down(a, w2, x2, g2, norm_mix_g[l + 1], mods[l + 1][0], mods[l + 1][1], seq, l,
                              emit_x=True, h_dtype=BF16)
        else:
            zero = jnp.zeros_like(g2)
            out = _mlp_down(a, w2, x2, g2, final_norm_g, zero, zero, seq, l, emit_x=False, h_dtype=x.dtype)
    return out.reshape(batch, seq, d)
```
